```python
import math
import jax
import jax.numpy as jnp
from jax import lax
import numpy as np

D_MODEL = 1024
BATCH = 1
SEQ = 16384
DEPTH = 2
DEC_BATCH = 16
DEC_SEQ = 64
PAST_LEN = 4096

CHUNK = 64
MIX_WIDTH = D_MODEL
GDN_DK = 128
GDN_DV = 128
GDN_HEADS = (MIX_WIDTH // 2) // GDN_DV
GDN_CONV = 4
RWKV_N = 64
RWKV_HEADS = (MIX_WIDTH - GDN_HEADS * GDN_DV) // RWKV_N
GDN_WIDTH = GDN_HEADS * GDN_DV
RWKV_WIDTH = RWKV_HEADS * RWKV_N
GDN_QK = GDN_HEADS * GDN_DK
GDN_QKV = 2 * GDN_QK + GDN_WIDTH
GDN_COLS = GDN_QKV + GDN_WIDTH + 2 * GDN_HEADS
DECAY_LORA = 64
AAA_LORA = 64
GATE_LORA = 128
RWKV_COLS = 3 * RWKV_WIDTH + DECAY_LORA + AAA_LORA + GATE_LORA
IN_COLS = GDN_COLS + RWKV_COLS
D_FF = 2816
N_EXPERTS = 8
TOP_K = 2
D_FF_EXPERT = 3584
MOE_BLOCK = 128
N_DENSE = (DEPTH + 1) // 2
N_MOE = DEPTH // 2
NORM_EPS = 1e-6
L2_EPS = 1e-6
GN_EPS = 64e-5

kernel_name = 'chunk_stream_gdn_rwkv7_hybrid_step'


def _rmsnorm(x, g):
    xf = x.astype(jnp.float32)
    xf = xf * lax.rsqrt(jnp.mean(xf * xf, axis=-1, keepdims=True) + NORM_EPS)
    return (xf * g.astype(jnp.float32)).astype(x.dtype)


def _l2norm(x):
    return x * lax.rsqrt(jnp.sum(x * x, axis=-1, keepdims=True) + L2_EPS)


def _swiglu(h, w_gate, w_up, w_down):
    return (jax.nn.silu(h @ w_gate) * (h @ w_up)) @ w_down


def _causal_conv(x, buf, w):
    t = x.shape[1]
    xc = jnp.concatenate([buf.astype(x.dtype), x], axis=1)
    y = xc[:, 0:t] * w[0]
    for i in range(1, GDN_CONV):
        y = y + xc[:, i:i + t] * w[i]
    return y, xc[:, t:]


def _to_blocks(u, n):
    u = u.reshape((u.shape[0], n, CHUNK) + u.shape[2:])
    return jnp.transpose(u, (1, 0, 3, 2) + tuple(range(4, u.ndim)))


def _gated_delta_rule(q, k, v, beta, g, s0):
    nb, nt, nh, dk = q.shape
    dv = v.shape[-1]
    pad = (-nt) % CHUNK
    if pad:
        q, k, v = (jnp.pad(u, ((0, 0), (0, pad), (0, 0), (0, 0))) for u in (q, k, v))
        beta, g = (jnp.pad(u, ((0, 0), (0, pad), (0, 0))) for u in (beta, g))
    n = (nt + pad) // CHUNK
    qc, kc, vc = _to_blocks(q, n), _to_blocks(k, n), _to_blocks(v, n)
    bc = _to_blocks(beta, n)
    gc = jnp.cumsum(_to_blocks(g, n), axis=-1)
    idx = jnp.arange(CHUNK)
    incl = idx[:, None] >= idx[None, :]
    strict = idx[:, None] > idx[None, :]
    decay = jnp.exp(jnp.where(incl, gc[..., :, None] - gc[..., None, :], -jnp.inf))
    kb = kc * bc[..., None]
    m = jnp.where(strict, jnp.einsum('nbhcd,nbhsd->nbhcs', kb, kc) * decay, 0.0)
    rhs = jnp.concatenate([vc * bc[..., None], kb * jnp.exp(gc)[..., None]], axis=-1)
    sol = lax.linalg.triangular_solve(m, rhs, left_side=True, lower=True, unit_diagonal=True)
    u, w = sol[..., :dv], sol[..., dv:]
    aqk = jnp.einsum('nbhcd,nbhsd->nbhcs', qc, kc) * decay
    qg = qc * jnp.exp(gc)[..., None]
    g_last = gc[..., -1]
    kd = kc * jnp.exp(g_last[..., None] - gc)[..., None]

    def step(s, xs):
        u_i, w_i, aqk_i, qg_i, kd_i, gl_i = xs
        v_new = u_i - jnp.einsum('bhcd,bhde->bhce', w_i, s)
        o = jnp.einsum('bhcd,bhde->bhce', qg_i, s) + jnp.einsum('bhcs,bhse->bhce', aqk_i, v_new)
        s = s * jnp.exp(gl_i)[..., None, None] + jnp.einsum('bhcd,bhce->bhde', kd_i, v_new)
        return s, o

    s_fin, o = lax.scan(step, s0, (u, w, aqk, qg, kd, g_last))
    o = jnp.transpose(o, (1, 0, 3, 2, 4)).reshape(nb, n * CHUNK, nh, dv)[:, :nt]
    return o, s_fin


def _rwkv7_scan(r, w, k, v, a, b, s0):
    def step(s, xs):
        r_t, w_t, k_t, v_t, a_t, b_t = xs
        sa = jnp.einsum('bhij,bhj->bhi', s, a_t)
        s = s * w_t[..., None, :] + sa[..., :, None] * b_t[..., None, :] + v_t[..., :, None] * k_t[..., None, :]
        return s, jnp.einsum('bhij,bhj->bhi', s, r_t)

    xs = tuple(jnp.moveaxis(u, 1, 0) for u in (r, w, k, v, a, b))
    s_fin, y = lax.scan(step, s0, xs)
    return jnp.moveaxis(y, 0, 1), s_fin


def _token_mix(h, conv_buf, s_gdn, shift_buf, s_rwkv, p, l):
    f32 = jnp.float32
    nb, nt, _ = h.shape
    proj = (h @ p['w_in'][l]).astype(f32)
    pa, pb = proj[..., :GDN_COLS], proj[..., GDN_COLS:]

    qkv, conv_new = _causal_conv(pa[..., :GDN_QKV], conv_buf.astype(f32), p['gdn_conv_w'][l].astype(f32))
    qkv = jax.nn.silu(qkv)
    q = _l2norm(qkv[..., :GDN_QK].reshape(nb, nt, GDN_HEADS, GDN_DK)) * (GDN_DK ** -0.5)
    k = _l2norm(qkv[..., GDN_QK:2 * GDN_QK].reshape(nb, nt, GDN_HEADS, GDN_DK))
    v = qkv[..., 2 * GDN_QK:].reshape(nb, nt, GDN_HEADS, GDN_DV)
    z = pa[..., GDN_QKV:GDN_QKV + GDN_WIDTH].reshape(nb, nt, GDN_HEADS, GDN_DV)
    a_in = pa[..., GDN_QKV + GDN_WIDTH:GDN_QKV + GDN_WIDTH + GDN_HEADS]
    b_in = pa[..., GDN_COLS - GDN_HEADS:]
    beta = jax.nn.sigmoid(b_in)
    g = -jnp.exp(p['gdn_a_log'][l].astype(f32)) * jax.nn.softplus(a_in + p['gdn_dt_bias'][l].astype(f32))
    o_a, s_gdn_new = _gated_delta_rule(q, k, v, beta, g, s_gdn.astype(f32))
    o_a = o_a * lax.rsqrt(jnp.mean(o_a * o_a, axis=-1, keepdims=True) + NORM_EPS)
    o_a = (o_a * p['gdn_onorm_g'][l].astype(f32) * jax.nn.silu(z)).reshape(nb, nt, GDN_WIDTH)

    xc = jnp.concatenate([shift_buf.astype(f32), pb], axis=1)
    shift_new = xc[:, nt:]
    xs = pb + (xc[:, :nt] - pb) * p['rwkv_mu'][l].astype(f32)
    off = 3 * RWKV_WIDTH
    r = xs[..., :RWKV_WIDTH]
    kr = xs[..., RWKV_WIDTH:2 * RWKV_WIDTH]
    vr = xs[..., 2 * RWKV_WIDTH:off]
    wd = xs[..., off:off + DECAY_LORA]
    ad = xs[..., off + DECAY_LORA:off + DECAY_LORA + AAA_LORA]
    gd = xs[..., off + DECAY_LORA + AAA_LORA:]
    wlog = -jax.nn.softplus(-(p['rwkv_w0'][l] + jnp.tanh(wd) @ p['rwkv_w2'][l])) - 0.5
    decay = jnp.exp(-jnp.exp(wlog))
    a = jax.nn.sigmoid(p['rwkv_a0'][l] + ad @ p['rwkv_a2'][l])
    gate = jax.nn.sigmoid(gd) @ p['rwkv_g2'][l]
    hs = lambda u: u.astype(f32).reshape(nb, nt, RWKV_HEADS, RWKV_N)
    kk = _l2norm(hs(kr * p['rwkv_k_k'][l]))
    kb = hs(kr * (1.0 + (a - 1.0) * p['rwkv_k_a'][l]))
    r_h, v_h, a_h = hs(r), hs(vr), hs(a)
    y, s_rwkv_new = _rwkv7_scan(r_h, hs(decay), kb, v_h, -kk, kk * a_h, s_rwkv.astype(f32))
    mu = jnp.mean(y, axis=-1, keepdims=True)
    var = jnp.mean(jnp.square(y - mu), axis=-1, keepdims=True)
    y = ((y - mu) * lax.rsqrt(var + GN_EPS)).reshape(nb, nt, RWKV_WIDTH)
    y = y * p['rwkv_ln_w'][l] + p['rwkv_ln_b'][l]
    bonus = jnp.sum(r_h * kb * p['rwkv_r_k'][l].astype(f32), axis=-1, keepdims=True) * v_h
    o_b = (y + bonus.reshape(nb, nt, RWKV_WIDTH)) * gate

    out = jnp.concatenate([o_a, o_b], axis=-1).astype(h.dtype) @ p['w_out'][l]
    return out, (conv_new, s_gdn_new, shift_new, s_rwkv_new)


def _moe_swiglu(h, w_router, w_gate, w_up, w_down):
    f32 = jnp.float32
    nb, nt, d = h.shape
    n = nb * nt
    hf = h.reshape(n, d)
    logits = (hf @ w_router).astype(f32)
    top_logit, top_e = lax.top_k(logits, TOP_K)
    top_p = jax.nn.softmax(top_logit, axis=-1)
    nk = n * TOP_K
    flat_e = top_e.reshape(nk)
    flat_tok = jnp.repeat(jnp.arange(n, dtype=jnp.int32), TOP_K)
    flat_p = top_p.reshape(nk)
    order = jnp.argsort(flat_e)
    e_sorted = flat_e[order]
    counts = jnp.bincount(flat_e, length=N_EXPERTS)
    padded = (counts + MOE_BLOCK - 1) // MOE_BLOCK * MOE_BLOCK
    start = jnp.cumsum(counts) - counts
    end_pad = jnp.cumsum(padded)
    start_pad = end_pad - padded
    dest = start_pad[e_sorted] + jnp.arange(nk) - start[e_sorted]
    n_blocks = -(-nk // MOE_BLOCK) + N_EXPERTS
    rows = n_blocks * MOE_BLOCK
    buf_tok = jnp.zeros((rows,), jnp.int32).at[dest].set(flat_tok[order])
    buf_p = jnp.zeros((rows,), f32).at[dest].set(flat_p[order])
    block_e = jnp.minimum(jnp.searchsorted(end_pad, jnp.arange(n_blocks) * MOE_BLOCK, side='right'), N_EXPERTS - 1)
    xb = hf[buf_tok].reshape(n_blocks, MOE_BLOCK, d)

    def expert_block(args):
        xg, e = args
        return _swiglu(xg, w_gate[e], w_up[e], w_down[e])

    yb = lax.map(expert_block, (xb, block_e)).reshape(rows, d)
    out = jnp.zeros((n, d), f32).at[buf_tok].add(yb.astype(f32) * buf_p[:, None])
    return out.astype(h.dtype).reshape(nb, nt, d)


def _layer(x, c, l, conv_buf, s_gdn, shift_buf, s_rwkv, p):
    mod = (jax.nn.silu(c) @ p['w_ada'][l] + p['b_ada'][l])[:, None, :]
    sh1, sc1, gt1, sh2, sc2, gt2 = jnp.split(mod, 6, axis=-1)
    h = _rmsnorm(x, p['norm1_g'][l]) * (1 + sc1) + sh1
    mix, st = _token_mix(h, conv_buf, s_gdn, shift_buf, s_rwkv, p, l)
    x = x + gt1 * mix
    h = _rmsnorm(x, p['norm2_g'][l]) * (1 + sc2) + sh2
    j = l // 2
    if l % 2 == 0:
        f = _swiglu(h, p['ffn_w_gate'][j], p['ffn_w_up'][j], p['ffn_w_down'][j])
    else:
        f = _moe_swiglu(h, p['moe_router'][j], p['moe_w_gate'][j], p['moe_w_up'][j], p['moe_w_down'][j])
    x = x + gt2 * f
    return x, tuple(s.astype(x.dtype) for s in st)


def _trunk(x, c, conv0, gdn0, shift0, rwkv0, p):
    convs, gdns, shifts, rwkvs = [], [], [], []
    for l in range(DEPTH):
        x, (cb, sg, sb, sr) = _layer(x, c, l, conv0[l], gdn0[l], shift0[l], rwkv0[l], p)
        convs.append(cb)
        gdns.append(sg)
        shifts.append(sb)
        rwkvs.append(sr)
    y = _rmsnorm(x, p['final_g'])
    return y, jnp.stack(convs), jnp.stack(gdns), jnp.stack(shifts), jnp.stack(rwkvs)


def setup_inputs(seed: int = 0) -> dict:
    key = jax.random.key(seed)
    ks = iter(jax.random.split(key, 48))
    f32 = jnp.float32
    nrm = lambda shape, s: jax.random.normal(next(ks), shape, f32) * s
    uni = lambda shape, lo, hi: jax.random.uniform(next(ks), shape, f32, lo, hi)
    D = D_MODEL
    dt = jnp.exp(uni((DEPTH, GDN_HEADS), math.log(1e-3), math.log(1e-1)))
    return {
        'x_prompt': nrm((BATCH, SEQ, D), 1.0),
        'x_sample': nrm((DEC_BATCH, DEC_SEQ, D), 1.0),
        'c_prompt': nrm((BATCH, D), 1.0),
        'c_sample': nrm((DEC_BATCH, D), 1.0),
        'state_gdn_conv': nrm((DEPTH, DEC_BATCH, GDN_CONV - 1, GDN_QKV), 1.0),
        'state_gdn': nrm((DEPTH, DEC_BATCH, GDN_HEADS, GDN_DK, GDN_DV), 0.1),
        'state_rwkv_shift': nrm((DEPTH, DEC_BATCH, 1, RWKV_COLS), 1.0),
        'state_rwkv': nrm((DEPTH, DEC_BATCH, RWKV_HEADS, RWKV_N, RWKV_N), 0.1),
        'w_ada': nrm((DEPTH, D, 6 * D), 0.5 * D ** -0.5),
        'b_ada': nrm((DEPTH, 6 * D), 0.02),
        'norm1_g': 1.0 + nrm((DEPTH, D), 0.05),
        'norm2_g': 1.0 + nrm((DEPTH, D), 0.05),
        'w_in': nrm((DEPTH, D, IN_COLS), D ** -0.5),
        'gdn_conv_w': nrm((DEPTH, GDN_CONV, GDN_QKV), GDN_CONV ** -0.5),
        'gdn_a_log': jnp.log(uni((DEPTH, GDN_HEADS), 1.0, 16.0)),
        'gdn_dt_bias': dt + jnp.log(-jnp.expm1(-dt)),
        'gdn_onorm_g': 1.0 + nrm((DEPTH, GDN_DV), 0.05),
        'rwkv_mu': uni((DEPTH, RWKV_COLS), 0.0, 1.0),
        'rwkv_w0': uni((DEPTH, RWKV_WIDTH), -6.0, -1.0),
        'rwkv_w2': nrm((DEPTH, DECAY_LORA, RWKV_WIDTH), 0.5 * DECAY_LORA ** -0.5),
        'rwkv_a0': nrm((DEPTH, RWKV_WIDTH), 0.1),
        'rwkv_a2': nrm((DEPTH, AAA_LORA, RWKV_WIDTH), 0.5 * AAA_LORA ** -0.5),
        'rwkv_g2': nrm((DEPTH, GATE_LORA, RWKV_WIDTH), GATE_LORA ** -0.5),
        'rwkv_k_k': 0.85 + nrm((DEPTH, RWKV_WIDTH), 0.05),
        'rwkv_k_a': 1.0 + nrm((DEPTH, RWKV_WIDTH), 0.05),
        'rwkv_r_k': nrm((DEPTH, RWKV_HEADS, RWKV_N), 0.1),
        'rwkv_ln_w': 1.0 + nrm((DEPTH, RWKV_WIDTH), 0.05),
        'rwkv_ln_b': nrm((DEPTH, RWKV_WIDTH), 0.02),
        'w_out': nrm((DEPTH, MIX_WIDTH, D), MIX_WIDTH ** -0.5),
        'ffn_w_gate': nrm((N_DENSE, D, D_FF), D ** -0.5),
        'ffn_w_up': nrm((N_DENSE, D, D_FF), D ** -0.5),
        'ffn_w_down': nrm((N_DENSE, D_FF, D), D_FF ** -0.5),
        'moe_router': nrm((N_MOE, D, N_EXPERTS), D ** -0.5),
        'moe_w_gate': nrm((N_MOE, N_EXPERTS, D, D_FF_EXPERT), D ** -0.5),
        'moe_w_up': nrm((N_MOE, N_EXPERTS, D, D_FF_EXPERT), D ** -0.5),
        'moe_w_down': nrm((N_MOE, N_EXPERTS, D_FF_EXPERT, D), D_FF_EXPERT ** -0.5),
        'final_g': 1.0 + nrm((D,), 0.05),
    }


def reference(x_prompt, x_sample, c_prompt, c_sample, state_gdn_conv, state_gdn, state_rwkv_shift, state_rwkv,
              w_ada, b_ada, norm1_g, norm2_g, w_in, gdn_conv_w, gdn_a_log, gdn_dt_bias, gdn_onorm_g,
              rwkv_mu, rwkv_w0, rwkv_w2, rwkv_a0, rwkv_a2, rwkv_g2, rwkv_k_k, rwkv_k_a, rwkv_r_k,
              rwkv_ln_w, rwkv_ln_b, w_out, ffn_w_gate, ffn_w_up, ffn_w_down,
              moe_router, moe_w_gate, moe_w_up, moe_w_down, final_g):
    p = dict(w_ada=w_ada, b_ada=b_ada, norm1_g=norm1_g, norm2_g=norm2_g, w_in=w_in,
             gdn_conv_w=gdn_conv_w, gdn_a_log=gdn_a_log, gdn_dt_bias=gdn_dt_bias, gdn_onorm_g=gdn_onorm_g,
             rwkv_mu=rwkv_mu, rwkv_w0=rwkv_w0, rwkv_w2=rwkv_w2, rwkv_a0=rwkv_a0, rwkv_a2=rwkv_a2,
             rwkv_g2=rwkv_g2, rwkv_k_k=rwkv_k_k, rwkv_k_a=rwkv_k_a, rwkv_r_k=rwkv_r_k,
             rwkv_ln_w=rwkv_ln_w, rwkv_ln_b=rwkv_ln_b, w_out=w_out,
             ffn_w_gate=ffn_w_gate, ffn_w_up=ffn_w_up, ffn_w_down=ffn_w_down,
             moe_router=moe_router, moe_w_gate=moe_w_gate, moe_w_up=moe_w_up, moe_w_down=moe_w_down,
             final_g=final_g)
    nb = x_prompt.shape[0]
    dt = x_prompt.dtype
    zc = jnp.zeros((DEPTH, nb, GDN_CONV - 1, GDN_QKV), dt)
    zg = jnp.zeros((DEPTH, nb, GDN_HEADS, GDN_DK, GDN_DV), dt)
    zs = jnp.zeros((DEPTH, nb, 1, RWKV_COLS), dt)
    zr = jnp.zeros((DEPTH, nb, RWKV_HEADS, RWKV_N, RWKV_N), dt)
    y_prompt, p_conv, p_gdn, p_shift, p_rwkv = _trunk(x_prompt, c_prompt, zc, zg, zs, zr, p)
    y_sample, s_conv, s_gdn, s_shift, s_rwkv = _trunk(
        x_sample, c_sample, state_gdn_conv, state_gdn, state_rwkv_shift, state_rwkv, p)
    return (y_prompt, y_sample, p_conv, p_gdn, p_shift, p_rwkv, s_conv, s_gdn, s_shift, s_rwkv)
```

```python
import functools

import jax
import jax.numpy as jnp
from jax import lax
from jax.experimental import pallas as pl
from jax.experimental.pallas import tpu as pltpu

F32 = jnp.float32
BF16 = jnp.bfloat16
HI = lax.Precision.HIGHEST

LANES = 128
SUBLANES = 8
CHUNK = 64
GDN_HEADS = 4
GDN_D = 128
RWKV_HEADS = 8
RWKV_N = 64
GDN_CONV = 4
N_EXPERTS = 8
NORM_EPS = 1e-6
L2_EPS = 1e-6
GN_EPS = 64e-5
NEG_BIG = -1e30
MOE_ROWS = 512
MIB = 1024 * 1024


def _cparams(sem, vmem_mib):
    return pltpu.CompilerParams(dimension_semantics=sem, vmem_limit_bytes=vmem_mib * MIB)


def _mm(a, b, prec=None):
    return jnp.dot(a, b, preferred_element_type=F32, precision=prec)


def _mm_nt(a, b, prec=None):
    return lax.dot_general(a, b, (((1,), (1,)), ((), ())), preferred_element_type=F32, precision=prec)


def _mm_tn(a, b, prec=None):
    return lax.dot_general(a, b, (((0,), (0,)), ((), ())), preferred_element_type=F32, precision=prec)


def _silu(x):
    return x * jax.nn.sigmoid(x)


def _softplus(x):
    return jnp.maximum(x, 0.0) + jnp.log1p(jnp.exp(-jnp.abs(x)))


def _iota2(shape, dim):
    return lax.broadcasted_iota(jnp.int32, shape, dim)


def _neumann_inverse(x, eye):
    t = eye + x
    p = x
    for _ in range(5):
        p = _mm(p, p, HI)
        t = t + _mm(t, p, HI)
    return t


def _ada_kernel(c_ref, w_ref, b_ref, o_ref):
    o_ref[0] = _mm(_silu(c_ref[...]), w_ref[0], HI) + b_ref[0]


def _ada_mod(c_all, w_ada, b_ada):
    depth, d, n6 = w_ada.shape
    bp = c_all.shape[0]
    tn = 1536
    return pl.pallas_call(
        _ada_kernel,
        grid=(depth, n6 // tn),
        in_specs=[pl.BlockSpec((bp, d), lambda l, j: (0, 0)),
                  pl.BlockSpec((1, d, tn), lambda l, j: (l, 0, j)),
                  pl.BlockSpec((1, 1, tn), lambda l, j: (l, 0, j))],
        out_specs=pl.BlockSpec((1, bp, tn), lambda l, j: (l, 0, j)),
        out_shape=jax.ShapeDtypeStruct((depth, bp, n6), F32),
        compiler_params=_cparams(("arbitrary", "arbitrary"), 40),
        name="ada_mod",
    )(c_all, w_ada, b_ada.reshape(depth, 1, n6))


def _norm_proj_kernel(x_ref, sh_ref, sc_ref, g_ref, wa_ref, wab_ref, wb_ref, pa_ref, pab_ref, pb_ref):
    x = x_ref[0]
    rs = lax.rsqrt(jnp.mean(x * x, axis=-1, keepdims=True) + NORM_EPS)
    h = (x * rs * g_ref[...]) * (1.0 + sc_ref[0]) + sh_ref[0]
    hb = h.astype(BF16)
    pa_ref[0] = _mm(hb, wa_ref[...])
    pab_ref[0] = _mm(hb, wab_ref[...])
    pb_ref[0] = _mm(hb, wb_ref[...])


def _norm_proj(x, mod, g, wa, wab, wb, tm):
    b, t, d = x.shape
    na, nab, nb = wa.shape[1], wab.shape[1], wb.shape[1]
    row = lambda i, j: (i, j, 0)
    const = lambda i, j: (0, 0)
    return pl.pallas_call(
        _norm_proj_kernel,
        grid=(b, t // tm),
        in_specs=[pl.BlockSpec((1, tm, d), row),
                  pl.BlockSpec((1, 1, d), lambda i, j: (i, 0, 0)),
                  pl.BlockSpec((1, 1, d), lambda i, j: (i, 0, 1)),
                  pl.BlockSpec((1, d), const),
                  pl.BlockSpec((d, na), const),
                  pl.BlockSpec((d, nab), const),
                  pl.BlockSpec((d, nb), const)],
        out_specs=[pl.BlockSpec((1, tm, na), row), pl.BlockSpec((1, tm, nab), row), pl.BlockSpec((1, tm, nb), row)],
        out_shape=[jax.ShapeDtypeStruct((b, t, na), F32), jax.ShapeDtypeStruct((b, t, nab), F32),
                   jax.ShapeDtypeStruct((b, t, nb), F32)],
        compiler_params=_cparams(("arbitrary", "arbitrary"), 48),
        name="norm_proj",
    )(x, mod, mod, g, wa, wab, wb)


def _gdn_kernel(pa_ref, pab_ref, cbuf_ref, s0_ref, cw_ref, alog_ref, dtb_ref, og_ref,
                o_ref, sfin_ref, xc_ref, s_ref, *, tc):
    t = pl.program_id(1)
    nqk = GDN_HEADS * GDN_D
    nconv = 3 * nqk

    @pl.when(t == 0)
    def _():
        xc_ref[0:SUBLANES, :] = cbuf_ref[0]
        s_ref[...] = s0_ref[0]

    xc_ref[SUBLANES:SUBLANES + tc, :] = pa_ref[0, :, 0:nconv]
    cw = cw_ref[...]
    y = xc_ref[5:5 + tc, :] * cw[0:1, :]
    for i in range(1, GDN_CONV):
        y = y + xc_ref[5 + i:5 + i + tc, :] * cw[i:i + 1, :]
    xc_ref[0:SUBLANES, :] = xc_ref[tc:tc + SUBLANES, :]
    qkv = _silu(y)

    ab = pab_ref[0]
    gmat = -jnp.exp(alog_ref[...]) * _softplus(ab + dtb_ref[...])
    bmat = jax.nn.sigmoid(ab)

    c = CHUNK
    ri = _iota2((c, c), 0)
    ci = _iota2((c, c), 1)
    incl = ri >= ci
    strict = ri > ci
    eye_c = (ri == ci).astype(F32)
    lincl = incl.astype(F32)
    r128 = _iota2((GDN_D, GDN_D), 0)
    c128 = _iota2((GDN_D, GDN_D), 1)
    eye_d = (r128 == c128).astype(F32)
    lane = _iota2((c, LANES), 1)
    og = og_ref[...]
    gcms = [_mm(lincl, gmat[ch * c:(ch + 1) * c], HI) for ch in range(tc // c)]

    for h in range(GDN_HEADS):
        sl = slice(h * GDN_D, (h + 1) * GDN_D)
        q_all = qkv[:, sl]
        k_all = qkv[:, nqk + h * GDN_D: nqk + (h + 1) * GDN_D]
        v_all = qkv[:, 2 * nqk + h * GDN_D: 2 * nqk + (h + 1) * GDN_D]
        q_all = q_all * lax.rsqrt(jnp.sum(q_all * q_all, axis=-1, keepdims=True) + L2_EPS) * (GDN_D ** -0.5)
        k_all = k_all * lax.rsqrt(jnp.sum(k_all * k_all, axis=-1, keepdims=True) + L2_EPS)
        z_all = pa_ref[0, :, nconv + h * GDN_D: nconv + (h + 1) * GDN_D]
        s = s_ref[h]
        for ch in range(tc // c):
            rows = slice(ch * c, (ch + 1) * c)
            q, k, v = q_all[rows], k_all[rows], v_all[rows]
            gc = gcms[ch][:, h:h + 1]
            beta = bmat[rows, GDN_HEADS + h:GDN_HEADS + h + 1]
            lh = jnp.where(lane == 0, gc, jnp.where(lane == 1, 1.0, 0.0))
            rh = jnp.where(lane == 0, 1.0, jnp.where(lane == 1, -gc, 0.0))
            diff = _mm_nt(lh, rh, HI)
            decay = jnp.exp(jnp.where(incl, diff, NEG_BIG))
            kb = k * beta
            m = jnp.where(strict, _mm_nt(kb, k, HI) * decay, 0.0)
            tinv = _neumann_inverse(-m, eye_c)
            egc = jnp.exp(gc)
            sol = _mm(tinv, jnp.concatenate([v * beta, kb * egc], axis=1), HI)
            u, w = sol[:, :GDN_D], sol[:, GDN_D:]
            aqk = _mm_nt(q, k, HI) * decay
            gl = gc[c - 1:c, :]
            kd = k * jnp.exp(gl - gc)
            v_new = u - _mm(w, s, HI)
            o = _mm(q * egc, s, HI) + _mm(aqk, v_new, HI)
            s = s * jnp.exp(gl) + _mm_tn(kd, v_new, HI)
            o = o * lax.rsqrt(jnp.mean(o * o, axis=-1, keepdims=True) + NORM_EPS)
            o = o * og * _silu(z_all[rows])
            o_ref[0, rows, sl] = o.astype(o_ref.dtype)
        s_ref[h] = s

    @pl.when(t == pl.num_programs(1) - 1)
    def _():
        sfin_ref[0] = s_ref[...]


def _gdn(pa, pab, cbuf8, s0, cw, alog, dtb, og, tc):
    b, t, na = pa.shape
    nconv = cw.shape[1]
    row = lambda i, j: (i, j, 0)
    const2 = lambda i, j: (0, 0)
    return pl.pallas_call(
        functools.partial(_gdn_kernel, tc=tc),
        grid=(b, t // tc),
        in_specs=[pl.BlockSpec((1, tc, na), row),
                  pl.BlockSpec((1, tc, LANES), row),
                  pl.BlockSpec((1, SUBLANES, nconv), lambda i, j: (i, 0, 0)),
                  pl.BlockSpec((1, GDN_HEADS, GDN_D, GDN_D), lambda i, j: (i, 0, 0, 0)),
                  pl.BlockSpec((GDN_CONV, nconv), const2),
                  pl.BlockSpec((1, LANES), const2),
                  pl.BlockSpec((1, LANES), const2),
                  pl.BlockSpec((1, GDN_D), const2)],
        out_specs=[pl.BlockSpec((1, tc, GDN_HEADS * GDN_D), row),
                   pl.BlockSpec((1, GDN_HEADS, GDN_D, GDN_D), lambda i, j: (i, 0, 0, 0))],
        out_shape=[jax.ShapeDtypeStruct((b, t, GDN_HEADS * GDN_D), BF16),
                   jax.ShapeDtypeStruct((b, GDN_HEADS, GDN_D, GDN_D), F32)],
        scratch_shapes=[pltpu.VMEM((tc + SUBLANES, nconv), F32),
                        pltpu.VMEM((GDN_HEADS, GDN_D, GDN_D), F32)],
        compiler_params=_cparams(("arbitrary", "arbitrary"), 40),
        name="gdn_mixer",
    )(pa, pab, cbuf8, s0, cw, alog, dtb, og)


def _rwkv_kernel(pb_ref, sbuf_ref, s0_ref, mu_ref, w0_ref, w2_ref, a0_ref, a2_ref, g2_ref, kk_ref, ka_ref,
                 rk_ref, lnw_ref, lnb_ref, bd_ref, o_ref, sfin_ref, xc_ref, s_ref, y_ref, *, tc):
    t = pl.program_id(1)
    nw = RWKV_HEADS * RWKV_N

    @pl.when(t == 0)
    def _():
        xc_ref[0:SUBLANES, :] = sbuf_ref[0]
        s_ref[...] = s0_ref[0]

    x = pb_ref[0]
    xc_ref[SUBLANES:SUBLANES + tc, :] = x
    prev = xc_ref[SUBLANES - 1:SUBLANES - 1 + tc, :]
    xc_ref[0:SUBLANES, :] = xc_ref[tc:tc + SUBLANES, :]
    xs = x + (prev - x) * mu_ref[...]
    r = xs[:, 0:nw]
    kr = xs[:, nw:2 * nw]
    vr = xs[:, 2 * nw:3 * nw]
    wa = xs[:, 3 * nw:3 * nw + LANES]
    gd = xs[:, 3 * nw + LANES:3 * nw + 2 * LANES]

    bd = bd_ref[...]
    wlog = -_softplus(-(w0_ref[...] + _mm(jnp.tanh(wa), w2_ref[...], HI))) - 0.5
    lw = -jnp.exp(wlog)
    a = jax.nn.sigmoid(a0_ref[...] + _mm(wa, a2_ref[...], HI))
    gate = _mm(jax.nn.sigmoid(gd), g2_ref[...], HI)
    kkr = kr * kk_ref[...]
    kk = kkr * lax.rsqrt(_mm(kkr * kkr, bd, HI) + L2_EPS)
    kb = kr * (1.0 + (a - 1.0) * ka_ref[...])
    aa = -kk
    bb = kk * a
    bonus = _mm(r * kb * rk_ref[...], bd, HI) * vr

    c = CHUNK
    n = RWKV_N
    ri = _iota2((c, c), 0)
    ci = _iota2((c, c), 1)
    incl = ri >= ci
    strict = ri > ci
    eye_c = (ri == ci).astype(F32)
    lincl = incl.astype(F32)

    for ch in range(tc // c):
        rows = slice(ch * c, (ch + 1) * c)
        lwc = lw[rows]
        g = _mm(lincl, lwc, HI)
        gfin = g[c - 1:c, :]
        eg = jnp.exp(g)
        eng = jnp.exp(-g)
        at_all = aa[rows] * jnp.exp(g - lwc)
        bt_all = bb[rows] * eng
        kt_all = kb[rows] * eng
        rt_all = r[rows] * eg
        tail = jnp.exp(gfin - g)
        bh_all = bb[rows] * tail
        kh_all = kb[rows] * tail
        egf = jnp.exp(gfin)
        v_allc = vr[rows]
        for h in range(RWKV_HEADS):
            sl = slice(h * n, (h + 1) * n)
            at, bt, kt, rt = at_all[:, sl], bt_all[:, sl], kt_all[:, sl], rt_all[:, sl]
            bh, kh, v = bh_all[:, sl], kh_all[:, sl], v_allc[:, sl]
            s = s_ref[h]
            ar = jnp.concatenate([at, rt], axis=0)
            xb_ = _mm_nt(ar, bt, HI)
            xk_ = _mm_nt(ar, kt, HI)
            lab = jnp.where(strict, xb_[:c], 0.0)
            lak = jnp.where(strict, xk_[:c], 0.0)
            mrb = jnp.where(incl, xb_[c:], 0.0)
            mrk = jnp.where(incl, xk_[c:], 0.0)
            tinv = _neumann_inverse(lab, eye_c)
            wt = _mm(tinv, at, HI)
            u0 = _mm(tinv, _mm(lak, v, HI), HI)
            qt = rt + _mm(mrb, wt, HI)
            y0 = _mm(mrb, u0, HI) + _mm(mrk, v, HI)
            p = jnp.where(ri == ci, egf[:, sl], 0.0) + _mm_tn(bh, wt, HI)
            inc = _mm_tn(u0, bh, HI) + _mm_tn(v, kh, HI)
            y_ref[rows, sl] = _mm_nt(qt, s, HI) + y0
            s_ref[h] = _mm_nt(s, p, HI) + inc

    y = y_ref[...]
    mean = _mm(y, bd, HI) * (1.0 / n)
    dlt = y - mean
    var = _mm(dlt * dlt, bd, HI) * (1.0 / n)
    yn = dlt * lax.rsqrt(var + GN_EPS) * lnw_ref[...] + lnb_ref[...]
    o_ref[0] = ((yn + bonus) * gate).astype(o_ref.dtype)

    @pl.when(t == pl.num_programs(1) - 1)
    def _():
        sfin_ref[0] = s_ref[...]


def _rwkv(pb, sbuf8, s0, vecs, w2p, a2p, g2, bd, tc):
    b, t, nb = pb.shape
    nw = RWKV_HEADS * RWKV_N
    row = lambda i, j: (i, j, 0)
    const2 = lambda i, j: (0, 0)
    mu, w0, a0, kk, ka, rk, lnw, lnb = vecs
    vec_spec = pl.BlockSpec((1, nw), const2)
    return pl.pallas_call(
        functools.partial(_rwkv_kernel, tc=tc),
        grid=(b, t // tc),
        in_specs=[pl.BlockSpec((1, tc, nb), row),
                  pl.BlockSpec((1, SUBLANES, nb), lambda i, j: (i, 0, 0)),
                  pl.BlockSpec((1, RWKV_HEADS, RWKV_N, RWKV_N), lambda i, j: (i, 0, 0, 0)),
                  pl.BlockSpec((1, nb), const2),
                  vec_spec,
                  pl.BlockSpec((LANES, nw), const2),
                  vec_spec,
                  pl.BlockSpec((LANES, nw), const2),
                  pl.BlockSpec((LANES, nw), const2),
                  vec_spec, vec_spec, vec_spec, vec_spec, vec_spec,
                  pl.BlockSpec((nw, nw), const2)],
        out_specs=[pl.BlockSpec((1, tc, nw), row),
                   pl.BlockSpec((1, RWKV_HEADS, RWKV_N, RWKV_N), lambda i, j: (i, 0, 0, 0))],
        out_shape=[jax.ShapeDtypeStruct((b, t, nw), BF16),
                   jax.ShapeDtypeStruct((b, RWKV_HEADS, RWKV_N, RWKV_N), F32)],
        scratch_shapes=[pltpu.VMEM((tc + SUBLANES, nb), F32),
                        pltpu.VMEM((RWKV_HEADS, RWKV_N, RWKV_N), F32),
                        pltpu.VMEM((tc, nw), F32)],
        compiler_params=_cparams(("arbitrary", "arbitrary"), 40),
        name="rwkv_mixer",
    )(pb, sbuf8, s0, mu, w0, w2p, a0, a2p, g2, kk, ka, rk, lnw, lnb, bd)


def _out_proj_kernel(*refs, with_router):
    if with_router:
        oa_ref, ob_ref, x_ref, gt_ref, sh_ref, sc_ref, g_ref, woa_ref, wob_ref, wr_ref, x1_ref, h_ref, lg_ref = refs
    else:
        oa_ref, ob_ref, x_ref, gt_ref, sh_ref, sc_ref, g_ref, woa_ref, wob_ref, x1_ref, h_ref = refs
    mix = _mm(oa_ref[0], woa_ref[...]) + _mm(ob_ref[0], wob_ref[...])
    x1 = x_ref[0] + gt_ref[0] * mix
    x1_ref[0] = x1
    rs = lax.rsqrt(jnp.mean(x1 * x1, axis=-1, keepdims=True) + NORM_EPS)
    h = (x1 * rs * g_ref[...]) * (1.0 + sc_ref[0]) + sh_ref[0]
    h_ref[0] = h.astype(h_ref.dtype)
    if with_router:
        lg_ref[0] = _mm(h, wr_ref[...], HI)


def _out_proj(oa, ob, x, mod, g, woa, wob, router, tm):
    b, t, d = x.shape
    nh = oa.shape[2]
    row = lambda i, j: (i, j, 0)
    const = lambda i, j: (0, 0)
    with_router = router is not None
    in_specs = [pl.BlockSpec((1, tm, nh), row), pl.BlockSpec((1, tm, nh), row), pl.BlockSpec((1, tm, d), row),
                pl.BlockSpec((1, 1, d), lambda i, j: (i, 0, 2)),
                pl.BlockSpec((1, 1, d), lambda i, j: (i, 0, 3)),
                pl.BlockSpec((1, 1, d), lambda i, j: (i, 0, 4)),
                pl.BlockSpec((1, d), const), pl.BlockSpec((nh, d), const), pl.BlockSpec((nh, d), const)]
    out_specs = [pl.BlockSpec((1, tm, d), row), pl.BlockSpec((1, tm, d), row)]
    args = [oa, ob, x, mod, mod, mod, g, woa, wob]
    if with_router:
        out_shape = [jax.ShapeDtypeStruct((b, t, d), F32), jax.ShapeDtypeStruct((b, t, d), F32),
                     jax.ShapeDtypeStruct((b, t, LANES), F32)]
        in_specs.append(pl.BlockSpec((d, LANES), const))
        out_specs.append(pl.BlockSpec((1, tm, LANES), row))
        args.append(router)
    else:
        out_shape = [jax.ShapeDtypeStruct((b, t, d), F32), jax.ShapeDtypeStruct((b, t, d), BF16)]
    return pl.pallas_call(
        functools.partial(_out_proj_kernel, with_router=with_router),
        grid=(b, t // tm),
        in_specs=in_specs, out_specs=out_specs, out_shape=out_shape,
        compiler_params=_cparams(("arbitrary", "arbitrary"), 40),
        name="out_proj_router" if with_router else "out_proj",
    )(*args)


def _final_norm(x2, fg_ref):
    rs = lax.rsqrt(jnp.mean(x2 * x2, axis=-1, keepdims=True) + NORM_EPS)
    return x2 * rs * fg_ref[...]


def _ffn_kernel(h_ref, x_ref, gt_ref, wg_ref, wu_ref, wd_ref, fg_ref, o_ref, *, nf, final):
    h = h_ref[0]
    ff = wg_ref.shape[1]
    tf = ff // nf
    acc = None
    for f in range(nf):
        cols = slice(f * tf, (f + 1) * tf)
        act = (_silu(_mm(h, wg_ref[:, cols])) * _mm(h, wu_ref[:, cols])).astype(BF16)
        part = _mm(act, wd_ref[cols, :])
        acc = part if acc is None else acc + part
    x2 = x_ref[0] + gt_ref[0] * acc
    o_ref[0] = _final_norm(x2, fg_ref) if final else x2


def _ffn(h, x, mod, wg, wu, wd, fg, final, tm):
    b, t, d = x.shape
    ff = wg.shape[1]
    row = lambda i, j: (i, j, 0)
    const = lambda i, j: (0, 0)
    once = pl.Buffered(1)
    return pl.pallas_call(
        functools.partial(_ffn_kernel, nf=2, final=final),
        grid=(b, t // tm),
        in_specs=[pl.BlockSpec((1, tm, d), row), pl.BlockSpec((1, tm, d), row),
                  pl.BlockSpec((1, 1, d), lambda i, j: (i, 0, 5)),
                  pl.BlockSpec((d, ff), const, pipeline_mode=once),
                  pl.BlockSpec((d, ff), const, pipeline_mode=once),
                  pl.BlockSpec((ff, d), const, pipeline_mode=once),
                  pl.BlockSpec((1, d), const)],
        out_specs=pl.BlockSpec((1, tm, d), row),
        out_shape=jax.ShapeDtypeStruct((b, t, d), F32),
        compiler_params=_cparams(("arbitrary", "arbitrary"), 52),
        name="ffn_dense",
    )(h, x, mod, wg, wu, wd, fg)


def _route_kernel(lg_ref, info_ref, p_ref, cnt_ref, carry_ref, *, tr):
    i = pl.program_id(0)

    @pl.when(i == 0)
    def _():
        carry_ref[...] = jnp.zeros_like(carry_ref)

    lane = _iota2((tr, LANES), 1)
    lg = jnp.where(lane < N_EXPERTS, lg_ref[...], NEG_BIG)
    m1 = jnp.max(lg, axis=-1, keepdims=True)
    e1 = jnp.min(jnp.where(lg == m1, lane, LANES), axis=-1, keepdims=True)
    lg2 = jnp.where(lane == e1, NEG_BIG, lg)
    m2 = jnp.max(lg2, axis=-1, keepdims=True)
    e2 = jnp.min(jnp.where(lg2 == m2, lane, LANES), axis=-1, keepdims=True)
    ex = jnp.exp(m2 - m1)
    den = 1.0 + ex
    p1 = 1.0 / den
    p2 = ex / den
    oh1 = lane == e1
    oh2 = lane == e2
    oh = jnp.where(oh1 | oh2, 1.0, 0.0)
    lstrict = (_iota2((tr, tr), 0) > _iota2((tr, tr), 1)).astype(BF16)
    cex = _mm(lstrict, oh.astype(BF16)) + carry_ref[...]
    rank1 = jnp.sum(jnp.where(oh1, cex, 0.0), axis=-1, keepdims=True).astype(jnp.int32)
    rank2 = jnp.sum(jnp.where(oh2, cex, 0.0), axis=-1, keepdims=True).astype(jnp.int32)
    carry_ref[...] = carry_ref[...] + jnp.sum(oh, axis=0, keepdims=True)
    info_ref[...] = jnp.where(lane == 0, e1, jnp.where(lane == 1, e2, jnp.where(lane == 2, rank1,
                              jnp.where(lane == 3, rank2, 0))))
    p_ref[...] = jnp.where(lane == 0, p1, jnp.where(lane == 1, p2, 0.0))

    @pl.when(i == pl.num_programs(0) - 1)
    def _():
        cnt_ref[...] = carry_ref[...].astype(jnp.int32)


def _route(logits, tr):
    n = logits.shape[0]
    blk = pl.BlockSpec((tr, LANES), lambda i: (i, 0))
    return pl.pallas_call(
        functools.partial(_route_kernel, tr=tr),
        grid=(n // tr,),
        in_specs=[blk],
        out_specs=[blk, blk, pl.BlockSpec((1, LANES), lambda i: (0, 0))],
        out_shape=[jax.ShapeDtypeStruct((n, LANES), jnp.int32), jax.ShapeDtypeStruct((n, LANES), F32),
                   jax.ShapeDtypeStruct((1, LANES), jnp.int32)],
        scratch_shapes=[pltpu.VMEM((1, LANES), F32)],
        compiler_params=_cparams(("arbitrary",), 24),
        name="moe_route",
    )(logits)


def _dispatch_kernel(e1_ref, e2_ref, r1_ref, r2_ref, sp_ref, ep_ref, h_ref, xb_ref, zero_ref, sem, zsem, *, td):
    i = pl.program_id(0)

    @pl.when(i == 0)
    def _():
        zero_ref[...] = jnp.zeros_like(zero_ref)

        def zero_block(start):
            cp = pltpu.make_async_copy(zero_ref, xb_ref.at[pl.ds(pl.multiple_of(start, MOE_ROWS), MOE_ROWS), :], zsem)
            cp.start()
            cp.wait()

        for e in range(N_EXPERTS):
            @pl.when(ep_ref[e] > sp_ref[e])
            def _():
                zero_block(ep_ref[e] - MOE_ROWS)

        def tail(blk, carry):
            zero_block(blk * MOE_ROWS)
            return carry

        lax.fori_loop(ep_ref[N_EXPERTS - 1] // MOE_ROWS, xb_ref.shape[0] // MOE_ROWS, tail, 0)

    base = i * td

    def row_copy(j, dst):
        return pltpu.make_async_copy(h_ref.at[pl.ds(j, 1), :], xb_ref.at[pl.ds(dst, 1), :], sem)

    def issue(j, carry):
        tok = base + j
        row_copy(j, sp_ref[e1_ref[tok]] + r1_ref[tok]).start()
        row_copy(j, sp_ref[e2_ref[tok]] + r2_ref[tok]).start()
        return carry

    lax.fori_loop(0, td, issue, 0)

    def drain(j, carry):
        row_copy(0, 0).wait()
        row_copy(0, 0).wait()
        return carry

    lax.fori_loop(0, td, drain, 0)


def _dispatch(e1, e2, r1, r2, sp, ep, h2, n_rows, td):
    n, d = h2.shape
    return pl.pallas_call(
        functools.partial(_dispatch_kernel, td=td),
        grid_spec=pltpu.PrefetchScalarGridSpec(
            num_scalar_prefetch=6,
            grid=(n // td,),
            in_specs=[pl.BlockSpec((td, d), lambda i, *_: (i, 0))],
            out_specs=pl.BlockSpec(memory_space=pl.ANY),
            scratch_shapes=[pltpu.VMEM((MOE_ROWS, d), F32), pltpu.SemaphoreType.DMA, pltpu.SemaphoreType.DMA]),
        out_shape=jax.ShapeDtypeStruct((n_rows, d), F32),
        compiler_params=_cparams(("arbitrary",), 24),
        name="moe_dispatch",
    )(e1, e2, r1, r2, sp, ep, h2)


def _expert_kernel(be_ref, nu_ref, xb_ref, wg_ref, wu_ref, wd_ref, yb_ref):
    j = pl.program_id(0)
    f = pl.program_id(1)

    @pl.when(j < nu_ref[0])
    def _():
        xg = xb_ref[...].astype(BF16)
        act = (_silu(_mm(xg, wg_ref[0])) * _mm(xg, wu_ref[0])).astype(BF16)
        part = _mm(act, wd_ref[0])

        @pl.when(f == 0)
        def _():
            yb_ref[...] = part

        @pl.when(f > 0)
        def _():
            yb_ref[...] = yb_ref[...] + part

    @pl.when((j >= nu_ref[0]) & (f == 0))
    def _():
        yb_ref[...] = jnp.zeros_like(yb_ref)


def _experts(block_e, n_used, xb, wg, wu, wd, tf):
    n_rows, d = xb.shape
    ff = wg.shape[2]
    nf = ff // tf
    n_blocks = n_rows // MOE_ROWS

    def blk(j, f, be, nu):
        return (jnp.minimum(j, nu[0] - 1), 0)

    def fcol(j, f, be, nu):
        return jnp.where(j < nu[0], f, nf - 1)

    return pl.pallas_call(
        _expert_kernel,
        grid_spec=pltpu.PrefetchScalarGridSpec(
            num_scalar_prefetch=2,
            grid=(n_blocks, nf),
            in_specs=[pl.BlockSpec((MOE_ROWS, d), blk),
                      pl.BlockSpec((1, d, tf), lambda j, f, be, nu: (be[j], 0, fcol(j, f, be, nu))),
                      pl.BlockSpec((1, d, tf), lambda j, f, be, nu: (be[j], 0, fcol(j, f, be, nu))),
                      pl.BlockSpec((1, tf, d), lambda j, f, be, nu: (be[j], fcol(j, f, be, nu), 0))],
            out_specs=pl.BlockSpec((MOE_ROWS, d), lambda j, f, be, nu: (j, 0))),
        out_shape=jax.ShapeDtypeStruct((n_rows, d), F32),
        compiler_params=_cparams(("arbitrary", "arbitrary"), 48),
        name="moe_experts",
    )(block_e, n_used, xb, wg, wu, wd)


def _combine_kernel(e1_ref, e2_ref, r1_ref, r2_ref, sp_ref, x_ref, gt_ref, p_ref, fg_ref, yb_ref,
                    o_ref, y1_ref, y2_ref, sem, *, tm, final):
    base = (pl.program_id(0) * pl.num_programs(1) + pl.program_id(1)) * tm

    def row_copy(src, buf_ref, j):
        return pltpu.make_async_copy(yb_ref.at[pl.ds(src, 1), :], buf_ref.at[pl.ds(j, 1), :], sem)

    def issue(j, carry):
        tok = base + j
        row_copy(sp_ref[e1_ref[tok]] + r1_ref[tok], y1_ref, j).start()
        row_copy(sp_ref[e2_ref[tok]] + r2_ref[tok], y2_ref, j).start()
        return carry

    lax.fori_loop(0, tm, issue, 0)

    def drain(j, carry):
        row_copy(0, y1_ref, 0).wait()
        row_copy(0, y2_ref, 0).wait()
        return carry

    lax.fori_loop(0, tm, drain, 0)

    p = p_ref[...]
    f = y1_ref[...] * p[:, 0:1] + y2_ref[...] * p[:, 1:2]
    x2 = x_ref[0] + gt_ref[0] * f
    o_ref[0] = _final_norm(x2, fg_ref) if final else x2


def _combine(e1, e2, r1, r2, sp, x, mod, probs, fg, yb, final, tm):
    b, t, d = x.shape
    nt = t // tm
    return pl.pallas_call(
        functools.partial(_combine_kernel, tm=tm, final=final),
        grid_spec=pltpu.PrefetchScalarGridSpec(
            num_scalar_prefetch=5,
            grid=(b, nt),
            in_specs=[pl.BlockSpec((1, tm, d), lambda i, j, *_: (i, j, 0)),
                      pl.BlockSpec((1, 1, d), lambda i, j, *_: (i, 0, 5)),
                      pl.BlockSpec((tm, LANES), lambda i, j, *_: (i * nt + j, 0)),
                      pl.BlockSpec((1, d), lambda i, j, *_: (0, 0)),
                      pl.BlockSpec(memory_space=pl.ANY)],
            out_specs=pl.BlockSpec((1, tm, d), lambda i, j, *_: (i, j, 0)),
            scratch_shapes=[pltpu.VMEM((tm, d), F32), pltpu.VMEM((tm, d), F32), pltpu.SemaphoreType.DMA]),
        out_shape=jax.ShapeDtypeStruct((b, t, d), F32),
        compiler_params=_cparams(("arbitrary", "arbitrary"), 24),
        name="moe_combine",
    )(e1, e2, r1, r2, sp, x, mod, probs, fg, yb)


def _moe(h2, logits, x1, mod, wg, wu, wd, fg, final, tm):
    b, t, d = x1.shape
    n = b * t
    info, probs, cnt = _route(logits.reshape(n, LANES), min(n, 256))
    e1, e2, r1, r2 = info[:, 0], info[:, 1], info[:, 2], info[:, 3]
    counts = cnt[0, :N_EXPERTS]
    padded = (counts + MOE_ROWS - 1) // MOE_ROWS * MOE_ROWS
    ep = jnp.cumsum(padded).astype(jnp.int32)
    sp = ep - padded
    n_blocks = -(-(2 * n) // MOE_ROWS) + N_EXPERTS
    n_used = jnp.maximum(ep[-1] // MOE_ROWS, 1).astype(jnp.int32)
    blk_start = jnp.minimum(jnp.arange(n_blocks, dtype=jnp.int32), n_used - 1) * MOE_ROWS
    block_e = jnp.minimum(jnp.sum(blk_start[:, None] >= ep[None, :], axis=1), N_EXPERTS - 1).astype(jnp.int32)
    xb = _dispatch(e1, e2, r1, r2, sp, ep, h2.reshape(n, d), n_blocks * MOE_ROWS, min(n, 256))
    yb = _experts(block_e, n_used.reshape(1), xb, wg, wu, wd, 896)
    return _combine(e1, e2, r1, r2, sp, x1, mod, probs, fg, yb, final, tm)


def _pad_rows_front(a, rows):
    pad = [(0, 0)] * a.ndim
    pad[-2] = (rows - a.shape[-2], 0)
    return jnp.pad(a, pad)


def _trunk(x, mod, conv0, gdn0, shift0, rwkv0, w):
    b, t, d = x.shape
    depth = w["wa"].shape[0]
    tm = min(t, 512)
    tc = min(t, 256)
    convs, gdns, shifts, rwkvs = [], [], [], []
    for l in range(depth):
        ml = mod[l]
        pa, pab, pb = _norm_proj(x, ml, w["norm1_g"][l], w["wa"][l], w["wab"][l], w["wb"][l], tm)
        oa, sg = _gdn(pa, pab, _pad_rows_front(conv0[l], SUBLANES), gdn0[l], w["conv_w"][l], w["alog"][l],
                      w["dtb"][l], w["onorm_g"][l], tc)
        ob, sr = _rwkv(pb, _pad_rows_front(shift0[l], SUBLANES), rwkv0[l], [v[l] for v in w["rwkv_vecs"]],
                       w["w2p"][l], w["a2p"][l], w["g2"][l], w["bd"], tc)
        convs.append(pa[:, t - (GDN_CONV - 1):, :w["conv_w"].shape[2]])
        shifts.append(pb[:, t - 1:, :])
        gdns.append(sg)
        rwkvs.append(sr)
        final = l == depth - 1
        j = l // 2
        if l % 2 == 0:
            x1, h2 = _out_proj(oa, ob, x, ml, w["norm2_g"][l], w["woa"][l], w["wob"][l], None, tm)
            x = _ffn(h2, x1, ml, w["ffn_g"][j], w["ffn_u"][j], w["ffn_d"][j], w["final_g"], final, tm)
        else:
            x1, h2, lg = _out_proj(oa, ob, x, ml, w["norm2_g"][l], w["woa"][l], w["wob"][l], w["router"][j], tm)
            x = _moe(h2, lg, x1, ml, w["moe_g"][j], w["moe_u"][j], w["moe_d"][j], w["final_g"], final, min(t, 256))
    return x, jnp.stack(convs), jnp.stack(gdns), jnp.stack(shifts), jnp.stack(rwkvs)


def kernel(x_prompt, x_sample, c_prompt, c_sample, state_gdn_conv, state_gdn, state_rwkv_shift, state_rwkv, w_ada, b_ada, norm1_g, norm2_g, w_in, gdn_conv_w, gdn_a_log, gdn_dt_bias, gdn_onorm_g, rwkv_mu, rwkv_w0, rwkv_w2, rwkv_a0, rwkv_a2, rwkv_g2, rwkv_k_k, rwkv_k_a, rwkv_r_k, rwkv_ln_w, rwkv_ln_b, w_out, ffn_w_gate, ffn_w_up, ffn_w_down, moe_router, moe_w_gate, moe_w_up, moe_w_down, final_g):
    depth, d, _ = w_in.shape
    nbp = x_prompt.shape[0]
    nbs = x_sample.shape[0]
    nqkvz = 4 * GDN_HEADS * GDN_D
    nab = 2 * GDN_HEADS
    nw = RWKV_HEADS * RWKV_N
    lora_w = rwkv_w2.shape[1]

    def lane_pad(v):
        return jnp.pad(v, ((0, 0), (0, LANES - v.shape[1])))[:, None, :]

    rows = lambda v: v[:, None, :]
    hi = jnp.arange(nw) // RWKV_N
    w = dict(
        norm1_g=rows(norm1_g), norm2_g=rows(norm2_g), final_g=final_g[None, :],
        wa=w_in[:, :, :nqkvz].astype(BF16),
        wab=jnp.pad(w_in[:, :, nqkvz:nqkvz + nab], ((0, 0), (0, 0), (0, LANES - nab))).astype(BF16),
        wb=w_in[:, :, nqkvz + nab:].astype(BF16),
        conv_w=gdn_conv_w, alog=lane_pad(gdn_a_log), dtb=lane_pad(gdn_dt_bias), onorm_g=rows(gdn_onorm_g),
        rwkv_vecs=[rows(rwkv_mu), rows(rwkv_w0), rows(rwkv_a0), rows(rwkv_k_k), rows(rwkv_k_a),
                   rwkv_r_k.reshape(depth, 1, nw), rows(rwkv_ln_w), rows(rwkv_ln_b)],
        w2p=jnp.pad(rwkv_w2, ((0, 0), (0, LANES - lora_w), (0, 0))),
        a2p=jnp.pad(rwkv_a2, ((0, 0), (lora_w, LANES - lora_w - rwkv_a2.shape[1]), (0, 0))),
        g2=rwkv_g2,
        bd=(hi[:, None] == hi[None, :]).astype(F32),
        woa=w_out[:, :GDN_HEADS * GDN_D, :].astype(BF16), wob=w_out[:, GDN_HEADS * GDN_D:, :].astype(BF16),
        ffn_g=ffn_w_gate.astype(BF16), ffn_u=ffn_w_up.astype(BF16), ffn_d=ffn_w_down.astype(BF16),
        router=jnp.pad(moe_router, ((0, 0), (0, 0), (0, LANES - moe_router.shape[2]))),
        moe_g=moe_w_gate.astype(BF16), moe_u=moe_w_up.astype(BF16), moe_d=moe_w_down.astype(BF16),
    )

    nb_all = nbp + nbs
    bp = -(-nb_all // SUBLANES) * SUBLANES
    c_all = jnp.pad(jnp.concatenate([c_prompt, c_sample], axis=0), ((0, bp - nb_all), (0, 0)))
    mod = _ada_mod(c_all, w_ada, b_ada)[:, :, None, :]
    mod_p, mod_s = mod[:, :nbp], mod[:, nbp:nb_all]

    dt = x_prompt.dtype
    zc = jnp.zeros((depth, nbp) + state_gdn_conv.shape[2:], dt)
    zg = jnp.zeros((depth, nbp) + state_gdn.shape[2:], dt)
    zs = jnp.zeros((depth, nbp) + state_rwkv_shift.shape[2:], dt)
    zr = jnp.zeros((depth, nbp) + state_rwkv.shape[2:], dt)
    y_p, p_conv, p_gdn, p_shift, p_rwkv = _trunk(x_prompt, mod_p, zc, zg, zs, zr, w)
    y_s, s_conv, s_gdn, s_shift, s_rwkv = _trunk(x_sample, mod_s, state_gdn_conv, state_gdn, state_rwkv_shift,
                                                 state_rwkv, w)
    return (y_p, y_s, p_conv, p_gdn, p_shift, p_rwkv, s_conv, s_gdn, s_shift, s_rwkv)
```

```python
import functools

import jax
import jax.numpy as jnp
from jax import lax
from jax.experimental import pallas as pl
from jax.experimental.pallas import tpu as pltpu

F32 = jnp.float32
BF16 = jnp.bfloat16
HI = lax.Precision.HIGHEST

LANES = 128
SUBLANES = 8
CHUNK = 64
GDN_HEADS = 4
GDN_D = 128
RWKV_HEADS = 8
RWKV_N = 64
GDN_CONV = 4
N_EXPERTS = 8
NORM_EPS = 1e-6
L2_EPS = 1e-6
GN_EPS = 64e-5
NEG_BIG = -1e30
MOE_ROWS = 512
MIB = 1024 * 1024

P_GDN_SCORE = 1
P_GDN_INV = 3
P_GDN_SOLVE = 1
P_GDN_STATE = 3
P_RWKV_SMALL = 1
P_RWKV_SCORE = 1
P_RWKV_INV = 3
P_RWKV_MID = 1
P_RWKV_STATE = 3


def _cparams(sem, vmem_mib):
    return pltpu.CompilerParams(dimension_semantics=sem, vmem_limit_bytes=vmem_mib * MIB)


def _split_bf16(x):
    hi = x.astype(BF16)
    return hi, (x - hi.astype(F32)).astype(BF16)


def _split3_bf16(x):
    hi = x.astype(BF16)
    r1 = x - hi.astype(F32)
    mid = r1.astype(BF16)
    return hi, mid, (r1 - mid.astype(F32)).astype(BF16)


_CONTRACT = {"nn": (1, 0), "nt": (1, 1), "tn": (0, 0)}


def _dg(a, b, kind, prec):
    off = a.ndim - 2
    ca, cb = _CONTRACT[kind]
    dn = (((ca + off,), (cb + off,)), (((0,), (0,)) if off else ((), ())))
    dot = lambda x, y: lax.dot_general(x, y, dn, preferred_element_type=F32)
    if prec is None or prec is HI:
        return lax.dot_general(a, b, dn, preferred_element_type=F32, precision=prec)
    if prec == 1:
        return dot(a.astype(BF16), b.astype(BF16))
    if prec == "l3":
        bb = b.astype(BF16)
        a0, a1, a2 = _split3_bf16(a)
        return dot(a0, bb) + dot(a1, bb) + dot(a2, bb)
    if prec == "r3":
        ab = a.astype(BF16)
        b0, b1, b2 = _split3_bf16(b)
        return dot(ab, b0) + dot(ab, b1) + dot(ab, b2)
    ah, al = _split_bf16(a)
    bh, bl = _split_bf16(b)
    return dot(ah, bh) + dot(ah, bl) + dot(al, bh)


def _mm(a, b, prec=None):
    return _dg(a, b, "nn", prec)


def _mm_nt(a, b, prec=None):
    return _dg(a, b, "nt", prec)


def _mm_tn(a, b, prec=None):
    return _dg(a, b, "tn", prec)


def _silu(x):
    return x * jax.nn.sigmoid(x)


def _softplus(x):
    return jnp.maximum(x, 0.0) + jnp.log1p(jnp.exp(-jnp.abs(x)))


def _iota2(shape, dim):
    return lax.broadcasted_iota(jnp.int32, shape, dim)


def _neumann_inverse(x, eye, prec):
    t = eye + x
    p = x
    for _ in range(5):
        p = _mm(p, p, prec)
        t = t + _mm(t, p, prec)
    return t


def _ada_kernel(c_ref, w_ref, b_ref, o_ref):
    o_ref[0] = _mm(_silu(c_ref[...]), w_ref[0], HI) + b_ref[0]


def _ada_mod(c_all, w_ada, b_ada):
    depth, d, n6 = w_ada.shape
    bp = c_all.shape[0]
    tn = 1536
    return pl.pallas_call(
        _ada_kernel,
        grid=(depth, n6 // tn),
        in_specs=[pl.BlockSpec((bp, d), lambda l, j: (0, 0)),
                  pl.BlockSpec((1, d, tn), lambda l, j: (l, 0, j)),
                  pl.BlockSpec((1, 1, tn), lambda l, j: (l, 0, j))],
        out_specs=pl.BlockSpec((1, bp, tn), lambda l, j: (l, 0, j)),
        out_shape=jax.ShapeDtypeStruct((depth, bp, n6), F32),
        compiler_params=_cparams(("arbitrary", "arbitrary"), 40),
        name="ada_mod",
    )(c_all, w_ada, b_ada.reshape(depth, 1, n6))


def _norm_proj_kernel(x_ref, sh_ref, sc_ref, g_ref, wa_ref, wab_ref, wb_ref, pa_ref, pab_ref, pb_ref):
    x = x_ref[0]
    rs = lax.rsqrt(jnp.mean(x * x, axis=-1, keepdims=True) + NORM_EPS)
    h = (x * rs * g_ref[...]) * (1.0 + sc_ref[0]) + sh_ref[0]
    hb = h.astype(BF16)
    pa_ref[0] = _mm(hb, wa_ref[...])
    pab_ref[0] = _mm(hb, wab_ref[...])
    pb_ref[0] = _mm(hb, wb_ref[...])


def _norm_proj(x, mod, g, wa, wab, wb, tm):
    b, t, d = x.shape
    na, nab, nb = wa.shape[1], wab.shape[1], wb.shape[1]
    row = lambda i, j: (i, j, 0)
    const = lambda i, j: (0, 0)
    return pl.pallas_call(
        _norm_proj_kernel,
        grid=(b, t // tm),
        in_specs=[pl.BlockSpec((1, tm, d), row),
                  pl.BlockSpec((1, 1, d), lambda i, j: (i, 0, 0)),
                  pl.BlockSpec((1, 1, d), lambda i, j: (i, 0, 1)),
                  pl.BlockSpec((1, d), const),
                  pl.BlockSpec((d, na), const),
                  pl.BlockSpec((d, nab), const),
                  pl.BlockSpec((d, nb), const)],
        out_specs=[pl.BlockSpec((1, tm, na), row), pl.BlockSpec((1, tm, nab), row), pl.BlockSpec((1, tm, nb), row)],
        out_shape=[jax.ShapeDtypeStruct((b, t, na), F32), jax.ShapeDtypeStruct((b, t, nab), F32),
                   jax.ShapeDtypeStruct((b, t, nb), F32)],
        compiler_params=_cparams(("arbitrary", "arbitrary"), 48),
        name="norm_proj",
    )(x, mod, mod, g, wa, wab, wb)


def _gdn_kernel(pa_ref, pab_ref, cbuf_ref, s0_ref, cw_ref, alog_ref, dtb_ref, og_ref,
                o_ref, sfin_ref, xc_ref, s_ref, *, tc):
    t = pl.program_id(1)
    nqk = GDN_HEADS * GDN_D
    nconv = 3 * nqk

    @pl.when(t == 0)
    def _():
        xc_ref[0:SUBLANES, :] = cbuf_ref[0]
        s_ref[...] = s0_ref[0]

    xc_ref[SUBLANES:SUBLANES + tc, :] = pa_ref[0, :, 0:nconv]
    cw = cw_ref[...]
    y = xc_ref[5:5 + tc, :] * cw[0:1, :]
    for i in range(1, GDN_CONV):
        y = y + xc_ref[5 + i:5 + i + tc, :] * cw[i:i + 1, :]
    xc_ref[0:SUBLANES, :] = xc_ref[tc:tc + SUBLANES, :]
    qkv = _silu(y)

    ab = pab_ref[0]
    gmat = -jnp.exp(alog_ref[...]) * _softplus(ab + dtb_ref[...])
    bmat = jax.nn.sigmoid(ab)

    c = CHUNK
    nch = tc // c
    nh = GDN_HEADS
    nb = nch * nh
    ri = _iota2((c, c), 0)
    ci = _iota2((c, c), 1)
    incl = ri >= ci
    strict = ri > ci
    eye_c = (ri == ci).astype(F32)

    def heads(x, width):
        x = x.reshape(nch, c, nh * width)
        return jnp.stack([x[:, :, h * width:(h + 1) * width] for h in range(nh)], axis=1).reshape(nb, c, width)

    q = heads(qkv[:, 0:nqk], GDN_D)
    k = heads(qkv[:, nqk:2 * nqk], GDN_D)
    v = heads(qkv[:, 2 * nqk:3 * nqk], GDN_D)
    q = q * lax.rsqrt(jnp.sum(q * q, axis=-1, keepdims=True) + L2_EPS) * (GDN_D ** -0.5)
    k = k * lax.rsqrt(jnp.sum(k * k, axis=-1, keepdims=True) + L2_EPS)
    beta = heads(bmat[:, nh:2 * nh], 1)

    g4 = gmat.reshape(nch, c, LANES)
    gcol4 = _mm(jnp.broadcast_to(incl.astype(F32), (nch, c, c)), g4, "r3")
    grow4 = _mm_tn(g4, jnp.broadcast_to((ri <= ci).astype(F32), (nch, c, c)), "l3")
    gc = jnp.stack([gcol4[:, :, h:h + 1] for h in range(nh)], axis=1).reshape(nb, c, 1)
    gr = jnp.stack([grow4[:, h:h + 1, :] for h in range(nh)], axis=1).reshape(nb, 1, c)
    decay = jnp.exp(jnp.where(incl, gc - gr, NEG_BIG))

    kb = k * beta
    m = jnp.where(strict, _mm_nt(kb, k, P_GDN_SCORE) * decay, 0.0)
    tinv = _neumann_inverse(-m, eye_c, P_GDN_INV)
    egc = jnp.exp(gc)
    sol = _mm(tinv, jnp.concatenate([v * beta, kb * egc], axis=2), P_GDN_SOLVE)
    u, w = sol[:, :, :GDN_D], sol[:, :, GDN_D:]
    aqk = _mm_nt(q, k, P_GDN_SCORE) * decay
    gl = gc[:, c - 1:c, :]
    kd = k * jnp.exp(gl - gc)
    qg = q * egc
    egl = jnp.exp(gl)

    s = s_ref[...]
    outs = []
    for ch in range(nch):
        bs = slice(ch * nh, (ch + 1) * nh)
        v_new = u[bs] - _mm(w[bs], s, P_GDN_STATE)
        outs.append(_mm(qg[bs], s, P_GDN_STATE) + _mm(aqk[bs], v_new, P_GDN_STATE))
        s = s * egl[bs] + _mm_tn(kd[bs], v_new, P_GDN_STATE)
    s_ref[...] = s

    o = jnp.stack(outs, axis=0)
    o = o * lax.rsqrt(jnp.mean(o * o, axis=-1, keepdims=True) + NORM_EPS) * og_ref[...]
    for h in range(nh):
        sl = slice(h * GDN_D, (h + 1) * GDN_D)
        z = pa_ref[0, :, nconv + h * GDN_D: nconv + (h + 1) * GDN_D]
        o_ref[0, :, sl] = (o[:, h].reshape(tc, GDN_D) * _silu(z)).astype(o_ref.dtype)

    @pl.when(t == pl.num_programs(1) - 1)
    def _():
        sfin_ref[0] = s_ref[...]


def _gdn(pa, pab, cbuf8, s0, cw, alog, dtb, og, tc):
    b, t, na = pa.shape
    nconv = cw.shape[1]
    row = lambda i, j: (i, j, 0)
    const2 = lambda i, j: (0, 0)
    return pl.pallas_call(
        functools.partial(_gdn_kernel, tc=tc),
        grid=(b, t // tc),
        in_specs=[pl.BlockSpec((1, tc, na), row),
                  pl.BlockSpec((1, tc, LANES), row),
                  pl.BlockSpec((1, SUBLANES, nconv), lambda i, j: (i, 0, 0)),
                  pl.BlockSpec((1, GDN_HEADS, GDN_D, GDN_D), lambda i, j: (i, 0, 0, 0)),
                  pl.BlockSpec((GDN_CONV, nconv), const2),
                  pl.BlockSpec((1, LANES), const2),
                  pl.BlockSpec((1, LANES), const2),
                  pl.BlockSpec((1, GDN_D), const2)],
        out_specs=[pl.BlockSpec((1, tc, GDN_HEADS * GDN_D), row),
                   pl.BlockSpec((1, GDN_HEADS, GDN_D, GDN_D), lambda i, j: (i, 0, 0, 0))],
        out_shape=[jax.ShapeDtypeStruct((b, t, GDN_HEADS * GDN_D), BF16),
                   jax.ShapeDtypeStruct((b, GDN_HEADS, GDN_D, GDN_D), F32)],
        scratch_shapes=[pltpu.VMEM((tc + SUBLANES, nconv), F32),
                        pltpu.VMEM((GDN_HEADS, GDN_D, GDN_D), F32)],
        compiler_params=_cparams(("arbitrary", "arbitrary"), 40),
        name="gdn_mixer",
    )(pa, pab, cbuf8, s0, cw, alog, dtb, og)


def _rwkv_kernel(pb_ref, sbuf_ref, s0_ref, mu_ref, w0_ref, w2_ref, a0_ref, a2_ref, g2_ref, kk_ref, ka_ref,
                 rk_ref, lnw_ref, lnb_ref, bd_ref, o_ref, sfin_ref, xc_ref, s_ref, y_ref, *, tc):
    t = pl.program_id(1)
    nw = RWKV_HEADS * RWKV_N

    @pl.when(t == 0)
    def _():
        xc_ref[0:SUBLANES, :] = sbuf_ref[0]
        s_ref[...] = s0_ref[0]

    x = pb_ref[0]
    xc_ref[SUBLANES:SUBLANES + tc, :] = x
    prev = xc_ref[SUBLANES - 1:SUBLANES - 1 + tc, :]
    xc_ref[0:SUBLANES, :] = xc_ref[tc:tc + SUBLANES, :]
    xs = x + (prev - x) * mu_ref[...]
    r = xs[:, 0:nw]
    kr = xs[:, nw:2 * nw]
    vr = xs[:, 2 * nw:3 * nw]
    wa = xs[:, 3 * nw:3 * nw + LANES]
    gd = xs[:, 3 * nw + LANES:3 * nw + 2 * LANES]

    bd = bd_ref[...]
    ps = P_RWKV_SMALL
    wlog = -_softplus(-(w0_ref[...] + _mm(jnp.tanh(wa), w2_ref[...], ps))) - 0.5
    lw = -jnp.exp(wlog)
    a = jax.nn.sigmoid(a0_ref[...] + _mm(wa, a2_ref[...], ps))
    gate = _mm(jax.nn.sigmoid(gd), g2_ref[...], ps)
    kkr = kr * kk_ref[...]
    kk = kkr * lax.rsqrt(_mm(kkr * kkr, bd, ps) + L2_EPS)
    kb = kr * (1.0 + (a - 1.0) * ka_ref[...])
    aa = -kk
    bb = kk * a
    bonus = _mm(r * kb * rk_ref[...], bd, ps) * vr

    c = CHUNK
    n = RWKV_N
    nch = tc // c
    nh = RWKV_HEADS
    nb = nch * nh
    ri = _iota2((c, c), 0)
    ci = _iota2((c, c), 1)
    incl = ri >= ci
    strict = ri > ci
    eye = ri == ci
    eye_c = eye.astype(F32)

    def heads(x):
        return jnp.stack([x[:, :, h * n:(h + 1) * n] for h in range(nh)], axis=1).reshape(nb, x.shape[1], n)

    chunks = lambda x: x.reshape(nch, c, nw)
    lw4 = chunks(lw)
    g = _mm(jnp.broadcast_to(incl.astype(F32), (nch, c, c)), lw4, "r3")
    gfin = g[:, c - 1:c, :]
    eng = jnp.exp(-g)
    tail = jnp.exp(gfin - g)
    aa4, bb4, kb4 = chunks(aa), chunks(bb), chunks(kb)
    at = heads(aa4 * jnp.exp(g - lw4))
    bt = heads(bb4 * eng)
    kt = heads(kb4 * eng)
    rt = heads(chunks(r) * jnp.exp(g))
    bh = heads(bb4 * tail)
    kh = heads(kb4 * tail)
    v = heads(chunks(vr))
    egf = heads(jnp.exp(gfin))

    pm = P_RWKV_MID
    ar = jnp.concatenate([at, rt], axis=1)
    xb_ = _mm_nt(ar, bt, P_RWKV_SCORE)
    xk_ = _mm_nt(ar, kt, P_RWKV_SCORE)
    lab = jnp.where(strict, xb_[:, :c], 0.0)
    lak = jnp.where(strict, xk_[:, :c], 0.0)
    mrb = jnp.where(incl, xb_[:, c:], 0.0)
    mrk = jnp.where(incl, xk_[:, c:], 0.0)
    tinv = _neumann_inverse(lab, eye_c, P_RWKV_INV)
    wt = _mm(tinv, at, pm)
    u0 = _mm(tinv, _mm(lak, v, pm), pm)
    qt = rt + _mm(mrb, wt, pm)
    y0 = _mm(mrb, u0, pm) + _mm(mrk, v, pm)
    p = jnp.where(eye, egf, 0.0) + _mm_tn(bh, wt, pm)
    inc = _mm_tn(u0, bh, pm) + _mm_tn(v, kh, pm)

    s = s_ref[...]
    ys = []
    for ch in range(nch):
        bs = slice(ch * nh, (ch + 1) * nh)
        ys.append(_mm_nt(qt[bs], s, P_RWKV_STATE) + y0[bs])
        s = _mm_nt(s, p[bs], P_RWKV_STATE) + inc[bs]
    s_ref[...] = s
    y4 = jnp.stack(ys, axis=0)
    for h in range(nh):
        y_ref[:, h * n:(h + 1) * n] = y4[:, h].reshape(tc, n)

    y = y_ref[...]
    mean = _mm(y, bd, ps) * (1.0 / n)
    dlt = y - mean
    var = _mm(dlt * dlt, bd, ps) * (1.0 / n)
    yn = dlt * lax.rsqrt(var + GN_EPS) * lnw_ref[...] + lnb_ref[...]
    o_ref[0] = ((yn + bonus) * gate).astype(o_ref.dtype)

    @pl.when(t == pl.num_programs(1) - 1)
    def _():
        sfin_ref[0] = s_ref[...]


def _rwkv(pb, sbuf8, s0, vecs, w2p, a2p, g2, bd, tc):
    b, t, nb = pb.shape
    nw = RWKV_HEADS * RWKV_N
    row = lambda i, j: (i, j, 0)
    const2 = lambda i, j: (0, 0)
    mu, w0, a0, kk, ka, rk, lnw, lnb = vecs
    vec_spec = pl.BlockSpec((1, nw), const2)
    return pl.pallas_call(
        functools.partial(_rwkv_kernel, tc=tc),
        grid=(b, t // tc),
        in_specs=[pl.BlockSpec((1, tc, nb), row),
                  pl.BlockSpec((1, SUBLANES, nb), lambda i, j: (i, 0, 0)),
                  pl.BlockSpec((1, RWKV_HEADS, RWKV_N, RWKV_N), lambda i, j: (i, 0, 0, 0)),
                  pl.BlockSpec((1, nb), const2),
                  vec_spec,
                  pl.BlockSpec((LANES, nw), const2),
                  vec_spec,
                  pl.BlockSpec((LANES, nw), const2),
                  pl.BlockSpec((LANES, nw), const2),
                  vec_spec, vec_spec, vec_spec, vec_spec, vec_spec,
                  pl.BlockSpec((nw, nw), const2)],
        out_specs=[pl.BlockSpec((1, tc, nw), row),
                   pl.BlockSpec((1, RWKV_HEADS, RWKV_N, RWKV_N), lambda i, j: (i, 0, 0, 0))],
        out_shape=[jax.ShapeDtypeStruct((b, t, nw), BF16),
                   jax.ShapeDtypeStruct((b, RWKV_HEADS, RWKV_N, RWKV_N), F32)],
        scratch_shapes=[pltpu.VMEM((tc + SUBLANES, nb), F32),
                        pltpu.VMEM((RWKV_HEADS, RWKV_N, RWKV_N), F32),
                        pltpu.VMEM((tc, nw), F32)],
        compiler_params=_cparams(("arbitrary", "arbitrary"), 40),
        name="rwkv_mixer",
    )(pb, sbuf8, s0, mu, w0, w2p, a0, a2p, g2, kk, ka, rk, lnw, lnb, bd)


def _out_proj_kernel(*refs, with_router):
    if with_router:
        oa_ref, ob_ref, x_ref, gt_ref, sh_ref, sc_ref, g_ref, woa_ref, wob_ref, wr_ref, x1_ref, h_ref, lg_ref = refs
    else:
        oa_ref, ob_ref, x_ref, gt_ref, sh_ref, sc_ref, g_ref, woa_ref, wob_ref, x1_ref, h_ref = refs
    mix = _mm(oa_ref[0], woa_ref[...]) + _mm(ob_ref[0], wob_ref[...])
    x1 = x_ref[0] + gt_ref[0] * mix
    x1_ref[0] = x1
    rs = lax.rsqrt(jnp.mean(x1 * x1, axis=-1, keepdims=True) + NORM_EPS)
    h = (x1 * rs * g_ref[...]) * (1.0 + sc_ref[0]) + sh_ref[0]
    h_ref[0] = h.astype(h_ref.dtype)
    if with_router:
        lg_ref[0] = _mm(h, wr_ref[...], HI)


def _out_proj(oa, ob, x, mod, g, woa, wob, router, tm):
    b, t, d = x.shape
    nh = oa.shape[2]
    row = lambda i, j: (i, j, 0)
    const = lambda i, j: (0, 0)
    with_router = router is not None
    in_specs = [pl.BlockSpec((1, tm, nh), row), pl.BlockSpec((1, tm, nh), row), pl.BlockSpec((1, tm, d), row),
                pl.BlockSpec((1, 1, d), lambda i, j: (i, 0, 2)),
                pl.BlockSpec((1, 1, d), lambda i, j: (i, 0, 3)),
                pl.BlockSpec((1, 1, d), lambda i, j: (i, 0, 4)),
                pl.BlockSpec((1, d), const), pl.BlockSpec((nh, d), const), pl.BlockSpec((nh, d), const)]
    out_specs = [pl.BlockSpec((1, tm, d), row), pl.BlockSpec((1, tm, d), row)]
    args = [oa, ob, x, mod, mod, mod, g, woa, wob]
    if with_router:
        out_shape = [jax.ShapeDtypeStruct((b, t, d), F32), jax.ShapeDtypeStruct((b, t, d), F32),
                     jax.ShapeDtypeStruct((b, t, LANES), F32)]
        in_specs.append(pl.BlockSpec((d, LANES), const))
        out_specs.append(pl.BlockSpec((1, tm, LANES), row))
        args.append(router)
    else:
        out_shape = [jax.ShapeDtypeStruct((b, t, d), F32), jax.ShapeDtypeStruct((b, t, d), BF16)]
    return pl.pallas_call(
        functools.partial(_out_proj_kernel, with_router=with_router),
        grid=(b, t // tm),
        in_specs=in_specs, out_specs=out_specs, out_shape=out_shape,
        compiler_params=_cparams(("arbitrary", "arbitrary"), 40),
        name="out_proj_router" if with_router else "out_proj",
    )(*args)


def _final_norm(x2, fg_ref):
    rs = lax.rsqrt(jnp.mean(x2 * x2, axis=-1, keepdims=True) + NORM_EPS)
    return x2 * rs * fg_ref[...]


def _ffn_kernel(h_ref, x_ref, gt_ref, wg_ref, wu_ref, wd_ref, fg_ref, o_ref, *, nf, final):
    h = h_ref[0]
    ff = wg_ref.shape[1]
    tf = ff // nf
    acc = None
    for f in range(nf):
        cols = slice(f * tf, (f + 1) * tf)
        act = (_silu(_mm(h, wg_ref[:, cols])) * _mm(h, wu_ref[:, cols])).astype(BF16)
        part = _mm(act, wd_ref[cols, :])
        acc = part if acc is None else acc + part
    x2 = x_ref[0] + gt_ref[0] * acc
    o_ref[0] = _final_norm(x2, fg_ref) if final else x2


def _ffn(h, x, mod, wg, wu, wd, fg, final, tm):
    b, t, d = x.shape
    ff = wg.shape[1]
    row = lambda i, j: (i, j, 0)
    const = lambda i, j: (0, 0)
    once = pl.Buffered(1)
    return pl.pallas_call(
        functools.partial(_ffn_kernel, nf=2, final=final),
        grid=(b, t // tm),
        in_specs=[pl.BlockSpec((1, tm, d), row), pl.BlockSpec((1, tm, d), row),
                  pl.BlockSpec((1, 1, d), lambda i, j: (i, 0, 5)),
                  pl.BlockSpec((d, ff), const, pipeline_mode=once),
                  pl.BlockSpec((d, ff), const, pipeline_mode=once),
                  pl.BlockSpec((ff, d), const, pipeline_mode=once),
                  pl.BlockSpec((1, d), const)],
        out_specs=pl.BlockSpec((1, tm, d), row),
        out_shape=jax.ShapeDtypeStruct((b, t, d), F32),
        compiler_params=_cparams(("arbitrary", "arbitrary"), 52),
        name="ffn_dense",
    )(h, x, mod, wg, wu, wd, fg)


def _route_kernel(lg_ref, info_ref, p_ref, cnt_ref, carry_ref, *, tr):
    i = pl.program_id(0)

    @pl.when(i == 0)
    def _():
        carry_ref[...] = jnp.zeros_like(carry_ref)

    lane = _iota2((tr, LANES), 1)
    lg = jnp.where(lane < N_EXPERTS, lg_ref[...], NEG_BIG)
    m1 = jnp.max(lg, axis=-1, keepdims=True)
    e1 = jnp.min(jnp.where(lg == m1, lane, LANES), axis=-1, keepdims=True)
    lg2 = jnp.where(lane == e1, NEG_BIG, lg)
    m2 = jnp.max(lg2, axis=-1, keepdims=True)
    e2 = jnp.min(jnp.where(lg2 == m2, lane, LANES), axis=-1, keepdims=True)
    ex = jnp.exp(m2 - m1)
    den = 1.0 + ex
    p1 = 1.0 / den
    p2 = ex / den
    oh1 = lane == e1
    oh2 = lane == e2
    oh = jnp.where(oh1 | oh2, 1.0, 0.0)
    lstrict = (_iota2((tr, tr), 0) > _iota2((tr, tr), 1)).astype(BF16)
    cex = _mm(lstrict, oh.astype(BF16)) + carry_ref[...]
    rank1 = jnp.sum(jnp.where(oh1, cex, 0.0), axis=-1, keepdims=True).astype(jnp.int32)
    rank2 = jnp.sum(jnp.where(oh2, cex, 0.0), axis=-1, keepdims=True).astype(jnp.int32)
    carry_ref[...] = carry_ref[...] + jnp.sum(oh, axis=0, keepdims=True)
    info_ref[...] = jnp.where(lane == 0, e1, jnp.where(lane == 1, e2, jnp.where(lane == 2, rank1,
                              jnp.where(lane == 3, rank2, 0))))
    p_ref[...] = jnp.where(lane == 0, p1, jnp.where(lane == 1, p2, 0.0))

    @pl.when(i == pl.num_programs(0) - 1)
    def _():
        cnt_ref[...] = carry_ref[...].astype(jnp.int32)


def _route(logits, tr):
    n = logits.shape[0]
    blk = pl.BlockSpec((tr, LANES), lambda i: (i, 0))
    return pl.pallas_call(
        functools.partial(_route_kernel, tr=tr),
        grid=(n // tr,),
        in_specs=[blk],
        out_specs=[blk, blk, pl.BlockSpec((1, LANES), lambda i: (0, 0))],
        out_shape=[jax.ShapeDtypeStruct((n, LANES), jnp.int32), jax.ShapeDtypeStruct((n, LANES), F32),
                   jax.ShapeDtypeStruct((1, LANES), jnp.int32)],
        scratch_shapes=[pltpu.VMEM((1, LANES), F32)],
        compiler_params=_cparams(("arbitrary",), 24),
        name="moe_route",
    )(logits)


def _dispatch_kernel(e1_ref, e2_ref, r1_ref, r2_ref, sp_ref, ep_ref, h_ref, xb_ref, zero_ref, sem, zsem, *, td):
    i = pl.program_id(0)

    @pl.when(i == 0)
    def _():
        zero_ref[...] = jnp.zeros_like(zero_ref)

        def zero_block(start):
            cp = pltpu.make_async_copy(zero_ref, xb_ref.at[pl.ds(pl.multiple_of(start, MOE_ROWS), MOE_ROWS), :], zsem)
            cp.start()
            cp.wait()

        for e in range(N_EXPERTS):
            @pl.when(ep_ref[e] > sp_ref[e])
            def _():
                zero_block(ep_ref[e] - MOE_ROWS)

        def tail(blk, carry):
            zero_block(blk * MOE_ROWS)
            return carry

        lax.fori_loop(ep_ref[N_EXPERTS - 1] // MOE_ROWS, xb_ref.shape[0] // MOE_ROWS, tail, 0)

    base = i * td

    def row_copy(j, dst):
        return pltpu.make_async_copy(h_ref.at[pl.ds(j, 1), :], xb_ref.at[pl.ds(dst, 1), :], sem)

    def issue(j, carry):
        tok = base + j
        row_copy(j, sp_ref[e1_ref[tok]] + r1_ref[tok]).start()
        row_copy(j, sp_ref[e2_ref[tok]] + r2_ref[tok]).start()
        return carry

    lax.fori_loop(0, td, issue, 0)

    def drain(j, carry):
        row_copy(0, 0).wait()
        row_copy(0, 0).wait()
        return carry

    lax.fori_loop(0, td, drain, 0)


def _dispatch(e1, e2, r1, r2, sp, ep, h2, n_rows, td):
    n, d = h2.shape
    return pl.pallas_call(
        functools.partial(_dispatch_kernel, td=td),
        grid_spec=pltpu.PrefetchScalarGridSpec(
            num_scalar_prefetch=6,
            grid=(n // td,),
            in_specs=[pl.BlockSpec((td, d), lambda i, *_: (i, 0))],
            out_specs=pl.BlockSpec(memory_space=pl.ANY),
            scratch_shapes=[pltpu.VMEM((MOE_ROWS, d), F32), pltpu.SemaphoreType.DMA, pltpu.SemaphoreType.DMA]),
        out_shape=jax.ShapeDtypeStruct((n_rows, d), F32),
        compiler_params=_cparams(("arbitrary",), 24),
        name="moe_dispatch",
    )(e1, e2, r1, r2, sp, ep, h2)


def _expert_kernel(be_ref, nu_ref, xb_ref, wg_ref, wu_ref, wd_ref, yb_ref):
    j = pl.program_id(0)
    f = pl.program_id(1)

    @pl.when(j < nu_ref[0])
    def _():
        xg = xb_ref[...].astype(BF16)
        act = (_silu(_mm(xg, wg_ref[0])) * _mm(xg, wu_ref[0])).astype(BF16)
        part = _mm(act, wd_ref[0])

        @pl.when(f == 0)
        def _():
            yb_ref[...] = part

        @pl.when(f > 0)
        def _():
            yb_ref[...] = yb_ref[...] + part

    @pl.when((j >= nu_ref[0]) & (f == 0))
    def _():
        yb_ref[...] = jnp.zeros_like(yb_ref)


def _experts(block_e, n_used, xb, wg, wu, wd, tf):
    n_rows, d = xb.shape
    ff = wg.shape[2]
    nf = ff // tf
    n_blocks = n_rows // MOE_ROWS

    def blk(j, f, be, nu):
        return (jnp.minimum(j, nu[0] - 1), 0)

    def fcol(j, f, be, nu):
        return jnp.where(j < nu[0], f, nf - 1)

    return pl.pallas_call(
        _expert_kernel,
        grid_spec=pltpu.PrefetchScalarGridSpec(
            num_scalar_prefetch=2,
            grid=(n_blocks, nf),
            in_specs=[pl.BlockSpec((MOE_ROWS, d), blk),
                      pl.BlockSpec((1, d, tf), lambda j, f, be, nu: (be[j], 0, fcol(j, f, be, nu))),
                      pl.BlockSpec((1, d, tf), lambda j, f, be, nu: (be[j], 0, fcol(j, f, be, nu))),
                      pl.BlockSpec((1, tf, d), lambda j, f, be, nu: (be[j], fcol(j, f, be, nu), 0))],
            out_specs=pl.BlockSpec((MOE_ROWS, d), lambda j, f, be, nu: (j, 0))),
        out_shape=jax.ShapeDtypeStruct((n_rows, d), F32),
        compiler_params=_cparams(("arbitrary", "arbitrary"), 48),
        name="moe_experts",
    )(block_e, n_used, xb, wg, wu, wd)


def _combine_kernel(e1_ref, e2_ref, r1_ref, r2_ref, sp_ref, x_ref, gt_ref, p_ref, fg_ref, yb_ref,
                    o_ref, y1_ref, y2_ref, sem, *, tm, final):
    base = (pl.program_id(0) * pl.num_programs(1) + pl.program_id(1)) * tm

    def row_copy(src, buf_ref, j):
        return pltpu.make_async_copy(yb_ref.at[pl.ds(src, 1), :], buf_ref.at[pl.ds(j, 1), :], sem)

    def issue(j, carry):
        tok = base + j
        row_copy(sp_ref[e1_ref[tok]] + r1_ref[tok], y1_ref, j).start()
        row_copy(sp_ref[e2_ref[tok]] + r2_ref[tok], y2_ref, j).start()
        return carry

    lax.fori_loop(0, tm, issue, 0)

    def drain(j, carry):
        row_copy(0, y1_ref, 0).wait()
        row_copy(0, y2_ref, 0).wait()
        return carry

    lax.fori_loop(0, tm, drain, 0)

    p = p_ref[...]
    f = y1_ref[...] * p[:, 0:1] + y2_ref[...] * p[:, 1:2]
    x2 = x_ref[0] + gt_ref[0] * f
    o_ref[0] = _final_norm(x2, fg_ref) if final else x2


def _combine(e1, e2, r1, r2, sp, x, mod, probs, fg, yb, final, tm):
    b, t, d = x.shape
    nt = t // tm
    return pl.pallas_call(
        functools.partial(_combine_kernel, tm=tm, final=final),
        grid_spec=pltpu.PrefetchScalarGridSpec(
            num_scalar_prefetch=5,
            grid=(b, nt),
            in_specs=[pl.BlockSpec((1, tm, d), lambda i, j, *_: (i, j, 0)),
                      pl.BlockSpec((1, 1, d), lambda i, j, *_: (i, 0, 5)),
                      pl.BlockSpec((tm, LANES), lambda i, j, *_: (i * nt + j, 0)),
                      pl.BlockSpec((1, d), lambda i, j, *_: (0, 0)),
                      pl.BlockSpec(memory_space=pl.ANY)],
            out_specs=pl.BlockSpec((1, tm, d), lambda i, j, *_: (i, j, 0)),
            scratch_shapes=[pltpu.VMEM((tm, d), F32), pltpu.VMEM((tm, d), F32), pltpu.SemaphoreType.DMA]),
        out_shape=jax.ShapeDtypeStruct((b, t, d), F32),
        compiler_params=_cparams(("arbitrary", "arbitrary"), 24),
        name="moe_combine",
    )(e1, e2, r1, r2, sp, x, mod, probs, fg, yb)


def _moe(h2, logits, x1, mod, wg, wu, wd, fg, final, tm):
    b, t, d = x1.shape
    n = b * t
    info, probs, cnt = _route(logits.reshape(n, LANES), min(n, 256))
    e1, e2, r1, r2 = info[:, 0], info[:, 1], info[:, 2], info[:, 3]
    counts = cnt[0, :N_EXPERTS]
    padded = (counts + MOE_ROWS - 1) // MOE_ROWS * MOE_ROWS
    ep = jnp.cumsum(padded).astype(jnp.int32)
    sp = ep - padded
    n_blocks = -(-(2 * n) // MOE_ROWS) + N_EXPERTS
    n_used = jnp.maximum(ep[-1] // MOE_ROWS, 1).astype(jnp.int32)
    blk_start = jnp.minimum(jnp.arange(n_blocks, dtype=jnp.int32), n_used - 1) * MOE_ROWS
    block_e = jnp.minimum(jnp.sum(blk_start[:, None] >= ep[None, :], axis=1), N_EXPERTS - 1).astype(jnp.int32)
    xb = _dispatch(e1, e2, r1, r2, sp, ep, h2.reshape(n, d), n_blocks * MOE_ROWS, min(n, 256))
    yb = _experts(block_e, n_used.reshape(1), xb, wg, wu, wd, 896)
    return _combine(e1, e2, r1, r2, sp, x1, mod, probs, fg, yb, final, tm)


def _pad_rows_front(a, rows):
    pad = [(0, 0)] * a.ndim
    pad[-2] = (rows - a.shape[-2], 0)
    return jnp.pad(a, pad)


def _trunk(x, mod, conv0, gdn0, shift0, rwkv0, w):
    b, t, d = x.shape
    depth = w["wa"].shape[0]
    tm = min(t, 512)
    tc = min(t, 256)
    convs, gdns, shifts, rwkvs = [], [], [], []
    for l in range(depth):
        ml = mod[l]
        pa, pab, pb = _norm_proj(x, ml, w["norm1_g"][l], w["wa"][l], w["wab"][l], w["wb"][l], tm)
        oa, sg = _gdn(pa, pab, _pad_rows_front(conv0[l], SUBLANES), gdn0[l], w["conv_w"][l], w["alog"][l],
                      w["dtb"][l], w["onorm_g"][l], tc)
        ob, sr = _rwkv(pb, _pad_rows_front(shift0[l], SUBLANES), rwkv0[l], [v[l] for v in w["rwkv_vecs"]],
                       w["w2p"][l], w["a2p"][l], w["g2"][l], w["bd"], tc)
        convs.append(pa[:, t - (GDN_CONV - 1):, :w["conv_w"].shape[2]])
        shifts.append(pb[:, t - 1:, :])
        gdns.append(sg)
        rwkvs.append(sr)
        final = l == depth - 1
        j = l // 2
        if l % 2 == 0:
            x1, h2 = _out_proj(oa, ob, x, ml, w["norm2_g"][l], w["woa"][l], w["wob"][l], None, tm)
            x = _ffn(h2, x1, ml, w["ffn_g"][j], w["ffn_u"][j], w["ffn_d"][j], w["final_g"], final, tm)
        else:
            x1, h2, lg = _out_proj(oa, ob, x, ml, w["norm2_g"][l], w["woa"][l], w["wob"][l], w["router"][j], tm)
            x = _moe(h2, lg, x1, ml, w["moe_g"][j], w["moe_u"][j], w["moe_d"][j], w["final_g"], final, min(t, 256))
    return x, jnp.stack(convs), jnp.stack(gdns), jnp.stack(shifts), jnp.stack(rwkvs)


def kernel(x_prompt, x_sample, c_prompt, c_sample, state_gdn_conv, state_gdn, state_rwkv_shift, state_rwkv, w_ada, b_ada, norm1_g, norm2_g, w_in, gdn_conv_w, gdn_a_log, gdn_dt_bias, gdn_onorm_g, rwkv_mu, rwkv_w0, rwkv_w2, rwkv_a0, rwkv_a2, rwkv_g2, rwkv_k_k, rwkv_k_a, rwkv_r_k, rwkv_ln_w, rwkv_ln_b, w_out, ffn_w_gate, ffn_w_up, ffn_w_down, moe_router, moe_w_gate, moe_w_up, moe_w_down, final_g):
    depth, d, _ = w_in.shape
    nbp = x_prompt.shape[0]
    nbs = x_sample.shape[0]
    nqkvz = 4 * GDN_HEADS * GDN_D
    nab = 2 * GDN_HEADS
    nw = RWKV_HEADS * RWKV_N
    lora_w = rwkv_w2.shape[1]

    def lane_pad(v):
        return jnp.pad(v, ((0, 0), (0, LANES - v.shape[1])))[:, None, :]

    rows = lambda v: v[:, None, :]
    hi = jnp.arange(nw) // RWKV_N
    w = dict(
        norm1_g=rows(norm1_g), norm2_g=rows(norm2_g), final_g=final_g[None, :],
        wa=w_in[:, :, :nqkvz].astype(BF16),
        wab=jnp.pad(w_in[:, :, nqkvz:nqkvz + nab], ((0, 0), (0, 0), (0, LANES - nab))).astype(BF16),
        wb=w_in[:, :, nqkvz + nab:].astype(BF16),
        conv_w=gdn_conv_w, alog=lane_pad(gdn_a_log), dtb=lane_pad(gdn_dt_bias), onorm_g=rows(gdn_onorm_g),
        rwkv_vecs=[rows(rwkv_mu), rows(rwkv_w0), rows(rwkv_a0), rows(rwkv_k_k), rows(rwkv_k_a),
                   rwkv_r_k.reshape(depth, 1, nw), rows(rwkv_ln_w), rows(rwkv_ln_b)],
        w2p=jnp.pad(rwkv_w2, ((0, 0), (0, LANES - lora_w), (0, 0))),
        a2p=jnp.pad(rwkv_a2, ((0, 0), (lora_w, LANES - lora_w - rwkv_a2.shape[1]), (0, 0))),
        g2=rwkv_g2,
        bd=(hi[:, None] == hi[None, :]).astype(F32),
        woa=w_out[:, :GDN_HEADS * GDN_D, :].astype(BF16), wob=w_out[:, GDN_HEADS * GDN_D:, :].astype(BF16),
        ffn_g=ffn_w_gate.astype(BF16), ffn_u=ffn_w_up.astype(BF16), ffn_d=ffn_w_down.astype(BF16),
        router=jnp.pad(moe_router, ((0, 0), (0, 0), (0, LANES - moe_router.shape[2]))),
        moe_g=moe_w_gate.astype(BF16), moe_u=moe_w_up.astype(BF16), moe_d=moe_w_down.astype(BF16),
    )

    nb_all = nbp + nbs
    bp = -(-nb_all // SUBLANES) * SUBLANES
    c_all = jnp.pad(jnp.concatenate([c_prompt, c_sample], axis=0), ((0, bp - nb_all), (0, 0)))
    mod = _ada_mod(c_all, w_ada, b_ada)[:, :, None, :]
    mod_p, mod_s = mod[:, :nbp], mod[:, nbp:nb_all]

    dt = x_prompt.dtype
    zc = jnp.zeros((depth, nbp) + state_gdn_conv.shape[2:], dt)
    zg = jnp.zeros((depth, nbp) + state_gdn.shape[2:], dt)
    zs = jnp.zeros((depth, nbp) + state_rwkv_shift.shape[2:], dt)
    zr = jnp.zeros((depth, nbp) + state_rwkv.shape[2:], dt)
    y_p, p_conv, p_gdn, p_shift, p_rwkv = _trunk(x_prompt, mod_p, zc, zg, zs, zr, w)
    y_s, s_conv, s_gdn, s_shift, s_rwkv = _trunk(x_sample, mod_s, state_gdn_conv, state_gdn, state_rwkv_shift,
                                                 state_rwkv, w)
    return (y_p, y_s, p_conv, p_gdn, p_shift, p_rwkv, s_conv, s_gdn, s_shift, s_rwkv)
```

```python
import functools

import jax
import jax.numpy as jnp
from jax import lax
from jax.experimental import pallas as pl
from jax.experimental.pallas import tpu as pltpu

F32 = jnp.float32
BF16 = jnp.bfloat16
HI = lax.Precision.HIGHEST

LANES = 128
SUBLANES = 8
CHUNK = 64
GDN_HEADS = 4
GDN_D = 128
RWKV_HEADS = 8
RWKV_N = 64
GDN_CONV = 4
N_EXPERTS = 8
NORM_EPS = 1e-6
L2_EPS = 1e-6
GN_EPS = 64e-5
NEG_BIG = -1e30
MOE_ROWS = 512
MIB = 1024 * 1024

P_GDN_SCORE = 1
P_GDN_INV = 3
P_GDN_SOLVE = 1
P_GDN_STATE = 1
P_RWKV_SMALL = 1
P_RWKV_SCORE = 1
P_RWKV_INV = 1
P_RWKV_MID = 1
P_RWKV_STATE = 1
P_ROUTER = 3


def _cparams(sem, vmem_mib):
    return pltpu.CompilerParams(dimension_semantics=sem, vmem_limit_bytes=vmem_mib * MIB)


def _split_bf16(x):
    hi = x.astype(BF16)
    return hi, (x - hi.astype(F32)).astype(BF16)


def _split3_bf16(x):
    hi = x.astype(BF16)
    r1 = x - hi.astype(F32)
    mid = r1.astype(BF16)
    return hi, mid, (r1 - mid.astype(F32)).astype(BF16)


_CONTRACT = {"nn": (1, 0), "nt": (1, 1), "tn": (0, 0)}


def _dg(a, b, kind, prec):
    off = a.ndim - 2
    ca, cb = _CONTRACT[kind]
    dn = (((ca + off,), (cb + off,)), (((0,), (0,)) if off else ((), ())))
    dot = lambda x, y: lax.dot_general(x, y, dn, preferred_element_type=F32)
    if prec is None or prec is HI:
        return lax.dot_general(a, b, dn, preferred_element_type=F32, precision=prec)
    if prec == 1:
        return dot(a.astype(BF16), b.astype(BF16))
    if prec == "l3":
        bb = b.astype(BF16)
        a0, a1, a2 = _split3_bf16(a)
        return dot(a0, bb) + dot(a1, bb) + dot(a2, bb)
    if prec == "r3":
        ab = a.astype(BF16)
        b0, b1, b2 = _split3_bf16(b)
        return dot(ab, b0) + dot(ab, b1) + dot(ab, b2)
    ah, al = _split_bf16(a)
    bh, bl = _split_bf16(b)
    return dot(ah, bh) + dot(ah, bl) + dot(al, bh)


def _mm(a, b, prec=None):
    return _dg(a, b, "nn", prec)


def _mm_nt(a, b, prec=None):
    return _dg(a, b, "nt", prec)


def _mm_tn(a, b, prec=None):
    return _dg(a, b, "tn", prec)


def _silu(x):
    return x * jax.nn.sigmoid(x)


def _softplus(x):
    return jnp.maximum(x, 0.0) + jnp.log1p(jnp.exp(-jnp.abs(x)))


def _iota2(shape, dim):
    return lax.broadcasted_iota(jnp.int32, shape, dim)


def _neumann_inverse(x, eye, prec):
    c = x.shape[-1]
    z = jnp.concatenate([x, jnp.broadcast_to(eye, x.shape)], axis=-1)
    keep_s = _iota2((c, 2 * c), 1) >= c
    for _ in range(6):
        z = _mm(z[..., :c], z, prec) + jnp.where(keep_s, z, 0.0)
    return z[..., c:]


def _ada_kernel(c_ref, w_ref, b_ref, o_ref):
    o_ref[0] = _mm(_silu(c_ref[...]), w_ref[0], HI) + b_ref[0]


def _ada_mod(c_all, w_ada, b_ada):
    depth, d, n6 = w_ada.shape
    bp = c_all.shape[0]
    tn = 1536
    return pl.pallas_call(
        _ada_kernel,
        grid=(depth, n6 // tn),
        in_specs=[pl.BlockSpec((bp, d), lambda l, j: (0, 0)),
                  pl.BlockSpec((1, d, tn), lambda l, j: (l, 0, j)),
                  pl.BlockSpec((1, 1, tn), lambda l, j: (l, 0, j))],
        out_specs=pl.BlockSpec((1, bp, tn), lambda l, j: (l, 0, j)),
        out_shape=jax.ShapeDtypeStruct((depth, bp, n6), F32),
        compiler_params=_cparams(("arbitrary", "arbitrary"), 40),
        name="ada_mod",
    )(c_all, w_ada, b_ada.reshape(depth, 1, n6))


def _norm_proj_kernel(x_ref, sh_ref, sc_ref, g_ref, wa_ref, wab_ref, wb_ref, pa_ref, pab_ref, pb_ref):
    x = x_ref[0]
    rs = lax.rsqrt(jnp.mean(x * x, axis=-1, keepdims=True) + NORM_EPS)
    h = (x * rs * g_ref[...]) * (1.0 + sc_ref[0]) + sh_ref[0]
    hb = h.astype(BF16)
    pa_ref[0] = _mm(hb, wa_ref[...])
    pab_ref[0] = _mm(hb, wab_ref[...])
    pb_ref[0] = _mm(hb, wb_ref[...])


def _norm_proj(x, mod, g, wa, wab, wb, tm):
    b, t, d = x.shape
    na, nab, nb = wa.shape[1], wab.shape[1], wb.shape[1]
    row = lambda i, j: (i, j, 0)
    const = lambda i, j: (0, 0)
    return pl.pallas_call(
        _norm_proj_kernel,
        grid=(b, t // tm),
        in_specs=[pl.BlockSpec((1, tm, d), row),
                  pl.BlockSpec((1, 1, d), lambda i, j: (i, 0, 0)),
                  pl.BlockSpec((1, 1, d), lambda i, j: (i, 0, 1)),
                  pl.BlockSpec((1, d), const),
                  pl.BlockSpec((d, na), const),
                  pl.BlockSpec((d, nab), const),
                  pl.BlockSpec((d, nb), const)],
        out_specs=[pl.BlockSpec((1, tm, na), row), pl.BlockSpec((1, tm, nab), row), pl.BlockSpec((1, tm, nb), row)],
        out_shape=[jax.ShapeDtypeStruct((b, t, na), F32), jax.ShapeDtypeStruct((b, t, nab), F32),
                   jax.ShapeDtypeStruct((b, t, nb), F32)],
        compiler_params=_cparams(("arbitrary", "arbitrary"), 48),
        name="norm_proj",
    )(x, mod, mod, g, wa, wab, wb)


def _gdn_kernel(pa_ref, pab_ref, cbuf_ref, s0_ref, cw_ref, alog_ref, dtb_ref, og_ref,
                o_ref, sfin_ref, xc_ref, s_ref, *, tc):
    t = pl.program_id(1)
    nqk = GDN_HEADS * GDN_D
    nconv = 3 * nqk

    @pl.when(t == 0)
    def _():
        xc_ref[0:SUBLANES, :] = cbuf_ref[0]
        s_ref[...] = s0_ref[0]

    xc_ref[SUBLANES:SUBLANES + tc, :] = pa_ref[0, :, 0:nconv]
    cw = cw_ref[...]
    y = xc_ref[5:5 + tc, :] * cw[0:1, :]
    for i in range(1, GDN_CONV):
        y = y + xc_ref[5 + i:5 + i + tc, :] * cw[i:i + 1, :]
    xc_ref[0:SUBLANES, :] = xc_ref[tc:tc + SUBLANES, :]
    qkv = _silu(y)

    ab = pab_ref[0]
    gmat = -jnp.exp(alog_ref[...]) * _softplus(ab + dtb_ref[...])
    bmat = jax.nn.sigmoid(ab)

    c = CHUNK
    nch = tc // c
    nh = GDN_HEADS
    nb = nch * nh
    ri = _iota2((c, c), 0)
    ci = _iota2((c, c), 1)
    incl = ri >= ci
    strict = ri > ci
    eye_c = (ri == ci).astype(F32)

    def heads(x, width):
        x = x.reshape(nch, c, nh * width)
        return jnp.stack([x[:, :, h * width:(h + 1) * width] for h in range(nh)], axis=1).reshape(nb, c, width)

    q = heads(qkv[:, 0:nqk], GDN_D)
    k = heads(qkv[:, nqk:2 * nqk], GDN_D)
    v = heads(qkv[:, 2 * nqk:3 * nqk], GDN_D)
    q = q * lax.rsqrt(jnp.sum(q * q, axis=-1, keepdims=True) + L2_EPS) * (GDN_D ** -0.5)
    k = k * lax.rsqrt(jnp.sum(k * k, axis=-1, keepdims=True) + L2_EPS)
    beta = heads(bmat[:, nh:2 * nh], 1)

    g4 = gmat.reshape(nch, c, LANES)
    gcol4 = _mm(jnp.broadcast_to(incl.astype(F32), (nch, c, c)), g4, "r3")
    grow4 = _mm_tn(g4, jnp.broadcast_to((ri <= ci).astype(F32), (nch, c, c)), "l3")
    gc = jnp.stack([gcol4[:, :, h:h + 1] for h in range(nh)], axis=1).reshape(nb, c, 1)
    gr = jnp.stack([grow4[:, h:h + 1, :] for h in range(nh)], axis=1).reshape(nb, 1, c)
    decay = jnp.exp(jnp.where(incl, gc - gr, NEG_BIG))

    kb = k * beta
    m = jnp.where(strict, _mm_nt(kb, k, P_GDN_SCORE) * decay, 0.0)
    tinv = _neumann_inverse(-m, eye_c, P_GDN_INV)
    egc = jnp.exp(gc)
    sol = _mm(tinv, jnp.concatenate([v * beta, kb * egc], axis=2), P_GDN_SOLVE)
    u, w = sol[:, :, :GDN_D], sol[:, :, GDN_D:]
    aqk = _mm_nt(q, k, P_GDN_SCORE) * decay
    gl = gc[:, c - 1:c, :]
    kd = k * jnp.exp(gl - gc)
    qg = q * egc
    egl = jnp.exp(gl)

    s = s_ref[...]
    outs = []
    for ch in range(nch):
        bs = slice(ch * nh, (ch + 1) * nh)
        v_new = u[bs] - _mm(w[bs], s, P_GDN_STATE)
        outs.append(_mm(qg[bs], s, P_GDN_STATE) + _mm(aqk[bs], v_new, P_GDN_STATE))
        s = s * egl[bs] + _mm_tn(kd[bs], v_new, P_GDN_STATE)
    s_ref[...] = s

    o = jnp.stack(outs, axis=0)
    o = o * lax.rsqrt(jnp.mean(o * o, axis=-1, keepdims=True) + NORM_EPS) * og_ref[...]
    for h in range(nh):
        sl = slice(h * GDN_D, (h + 1) * GDN_D)
        z = pa_ref[0, :, nconv + h * GDN_D: nconv + (h + 1) * GDN_D]
        o_ref[0, :, sl] = (o[:, h].reshape(tc, GDN_D) * _silu(z)).astype(o_ref.dtype)

    @pl.when(t == pl.num_programs(1) - 1)
    def _():
        sfin_ref[0] = s_ref[...]


def _gdn(pa, pab, cbuf8, s0, cw, alog, dtb, og, tc):
    b, t, na = pa.shape
    nconv = cw.shape[1]
    row = lambda i, j: (i, j, 0)
    const2 = lambda i, j: (0, 0)
    return pl.pallas_call(
        functools.partial(_gdn_kernel, tc=tc),
        grid=(b, t // tc),
        in_specs=[pl.BlockSpec((1, tc, na), row),
                  pl.BlockSpec((1, tc, LANES), row),
                  pl.BlockSpec((1, SUBLANES, nconv), lambda i, j: (i, 0, 0)),
                  pl.BlockSpec((1, GDN_HEADS, GDN_D, GDN_D), lambda i, j: (i, 0, 0, 0)),
                  pl.BlockSpec((GDN_CONV, nconv), const2),
                  pl.BlockSpec((1, LANES), const2),
                  pl.BlockSpec((1, LANES), const2),
                  pl.BlockSpec((1, GDN_D), const2)],
        out_specs=[pl.BlockSpec((1, tc, GDN_HEADS * GDN_D), row),
                   pl.BlockSpec((1, GDN_HEADS, GDN_D, GDN_D), lambda i, j: (i, 0, 0, 0))],
        out_shape=[jax.ShapeDtypeStruct((b, t, GDN_HEADS * GDN_D), BF16),
                   jax.ShapeDtypeStruct((b, GDN_HEADS, GDN_D, GDN_D), F32)],
        scratch_shapes=[pltpu.VMEM((tc + SUBLANES, nconv), F32),
                        pltpu.VMEM((GDN_HEADS, GDN_D, GDN_D), F32)],
        compiler_params=_cparams(("arbitrary", "arbitrary"), 40),
        name="gdn_mixer",
    )(pa, pab, cbuf8, s0, cw, alog, dtb, og)


def _rwkv_kernel(pb_ref, sbuf_ref, s0_ref, mu_ref, w0_ref, w2_ref, a0_ref, a2_ref, g2_ref, kk_ref, ka_ref,
                 rk_ref, lnw_ref, lnb_ref, bd_ref, o_ref, sfin_ref, xc_ref, s_ref, y_ref, *, tc):
    t = pl.program_id(1)
    nw = RWKV_HEADS * RWKV_N

    @pl.when(t == 0)
    def _():
        xc_ref[0:SUBLANES, :] = sbuf_ref[0]
        s_ref[...] = s0_ref[0]

    x = pb_ref[0]
    xc_ref[SUBLANES:SUBLANES + tc, :] = x
    prev = xc_ref[SUBLANES - 1:SUBLANES - 1 + tc, :]
    xc_ref[0:SUBLANES, :] = xc_ref[tc:tc + SUBLANES, :]
    xs = x + (prev - x) * mu_ref[...]
    r = xs[:, 0:nw]
    kr = xs[:, nw:2 * nw]
    vr = xs[:, 2 * nw:3 * nw]
    wa = xs[:, 3 * nw:3 * nw + LANES]
    gd = xs[:, 3 * nw + LANES:3 * nw + 2 * LANES]

    bd = bd_ref[...]
    ps = P_RWKV_SMALL
    wlog = -_softplus(-(w0_ref[...] + _mm(jnp.tanh(wa), w2_ref[...], ps))) - 0.5
    lw = -jnp.exp(wlog)
    a = jax.nn.sigmoid(a0_ref[...] + _mm(wa, a2_ref[...], ps))
    gate = _mm(jax.nn.sigmoid(gd), g2_ref[...], ps)
    kkr = kr * kk_ref[...]
    kk = kkr * lax.rsqrt(_mm(kkr * kkr, bd, ps) + L2_EPS)
    kb = kr * (1.0 + (a - 1.0) * ka_ref[...])
    aa = -kk
    bb = kk * a
    bonus = _mm(r * kb * rk_ref[...], bd, ps) * vr

    c = CHUNK
    n = RWKV_N
    nch = tc // c
    nh = RWKV_HEADS
    nb = nch * nh
    ri = _iota2((c, c), 0)
    ci = _iota2((c, c), 1)
    incl = ri >= ci
    strict = ri > ci
    eye = ri == ci
    eye_c = eye.astype(F32)

    def heads(x):
        return jnp.stack([x[:, :, h * n:(h + 1) * n] for h in range(nh)], axis=1).reshape(nb, x.shape[1], n)

    chunks = lambda x: x.reshape(nch, c, nw)
    lw4 = chunks(lw)
    g = _mm(jnp.broadcast_to(incl.astype(F32), (nch, c, c)), lw4, "r3")
    gfin = g[:, c - 1:c, :]
    eng = jnp.exp(-g)
    tail = jnp.exp(gfin - g)
    aa4, bb4, kb4 = chunks(aa), chunks(bb), chunks(kb)
    at = heads(aa4 * jnp.exp(g - lw4))
    bt = heads(bb4 * eng)
    kt = heads(kb4 * eng)
    rt = heads(chunks(r) * jnp.exp(g))
    bh = heads(bb4 * tail)
    kh = heads(kb4 * tail)
    v = heads(chunks(vr))
    egf = heads(jnp.exp(gfin))

    pm = P_RWKV_MID
    ar = jnp.concatenate([at, rt], axis=1)
    xb_ = _mm_nt(ar, bt, P_RWKV_SCORE)
    xk_ = _mm_nt(ar, kt, P_RWKV_SCORE)
    lab = jnp.where(strict, xb_[:, :c], 0.0)
    lak = jnp.where(strict, xk_[:, :c], 0.0)
    mrb = jnp.where(incl, xb_[:, c:], 0.0)
    mrk = jnp.where(incl, xk_[:, c:], 0.0)
    tinv = _neumann_inverse(lab, eye_c, P_RWKV_INV)
    wt = _mm(tinv, at, pm)
    u0 = _mm(tinv, _mm(lak, v, pm), pm)
    qt = rt + _mm(mrb, wt, pm)
    y0 = _mm(mrb, u0, pm) + _mm(mrk, v, pm)
    p = jnp.where(eye, egf, 0.0) + _mm_tn(bh, wt, pm)
    inc = _mm_tn(u0, bh, pm) + _mm_tn(v, kh, pm)

    s = s_ref[...]
    ys = []
    for ch in range(nch):
        bs = slice(ch * nh, (ch + 1) * nh)
        ys.append(_mm_nt(qt[bs], s, P_RWKV_STATE) + y0[bs])
        s = _mm_nt(s, p[bs], P_RWKV_STATE) + inc[bs]
    s_ref[...] = s
    y4 = jnp.stack(ys, axis=0)
    for h in range(nh):
        y_ref[:, h * n:(h + 1) * n] = y4[:, h].reshape(tc, n)

    y = y_ref[...]
    mean = _mm(y, bd, ps) * (1.0 / n)
    dlt = y - mean
    var = _mm(dlt * dlt, bd, ps) * (1.0 / n)
    yn = dlt * lax.rsqrt(var + GN_EPS) * lnw_ref[...] + lnb_ref[...]
    o_ref[0] = ((yn + bonus) * gate).astype(o_ref.dtype)

    @pl.when(t == pl.num_programs(1) - 1)
    def _():
        sfin_ref[0] = s_ref[...]


def _rwkv(pb, sbuf8, s0, vecs, w2p, a2p, g2, bd, tc):
    b, t, nb = pb.shape
    nw = RWKV_HEADS * RWKV_N
    row = lambda i, j: (i, j, 0)
    const2 = lambda i, j: (0, 0)
    mu, w0, a0, kk, ka, rk, lnw, lnb = vecs
    vec_spec = pl.BlockSpec((1, nw), const2)
    return pl.pallas_call(
        functools.partial(_rwkv_kernel, tc=tc),
        grid=(b, t // tc),
        in_specs=[pl.BlockSpec((1, tc, nb), row),
                  pl.BlockSpec((1, SUBLANES, nb), lambda i, j: (i, 0, 0)),
                  pl.BlockSpec((1, RWKV_HEADS, RWKV_N, RWKV_N), lambda i, j: (i, 0, 0, 0)),
                  pl.BlockSpec((1, nb), const2),
                  vec_spec,
                  pl.BlockSpec((LANES, nw), const2),
                  vec_spec,
                  pl.BlockSpec((LANES, nw), const2),
                  pl.BlockSpec((LANES, nw), const2),
                  vec_spec, vec_spec, vec_spec, vec_spec, vec_spec,
                  pl.BlockSpec((nw, nw), const2)],
        out_specs=[pl.BlockSpec((1, tc, nw), row),
                   pl.BlockSpec((1, RWKV_HEADS, RWKV_N, RWKV_N), lambda i, j: (i, 0, 0, 0))],
        out_shape=[jax.ShapeDtypeStruct((b, t, nw), BF16),
                   jax.ShapeDtypeStruct((b, RWKV_HEADS, RWKV_N, RWKV_N), F32)],
        scratch_shapes=[pltpu.VMEM((tc + SUBLANES, nb), F32),
                        pltpu.VMEM((RWKV_HEADS, RWKV_N, RWKV_N), F32),
                        pltpu.VMEM((tc, nw), F32)],
        compiler_params=_cparams(("arbitrary", "arbitrary"), 40),
        name="rwkv_mixer",
    )(pb, sbuf8, s0, mu, w0, w2p, a0, a2p, g2, kk, ka, rk, lnw, lnb, bd)


def _out_proj_kernel(*refs, with_router):
    if with_router:
        oa_ref, ob_ref, x_ref, gt_ref, sh_ref, sc_ref, g_ref, woa_ref, wob_ref, wr_ref, x1_ref, h_ref, lg_ref = refs
    else:
        oa_ref, ob_ref, x_ref, gt_ref, sh_ref, sc_ref, g_ref, woa_ref, wob_ref, x1_ref, h_ref = refs
    mix = _mm(oa_ref[0], woa_ref[...]) + _mm(ob_ref[0], wob_ref[...])
    x1 = x_ref[0] + gt_ref[0] * mix
    x1_ref[0] = x1
    rs = lax.rsqrt(jnp.mean(x1 * x1, axis=-1, keepdims=True) + NORM_EPS)
    h = (x1 * rs * g_ref[...]) * (1.0 + sc_ref[0]) + sh_ref[0]
    h_ref[0] = h.astype(h_ref.dtype)
    if with_router:
        lg_ref[0] = _mm(h, wr_ref[...], P_ROUTER)


def _out_proj(oa, ob, x, mod, g, woa, wob, router, tm):
    b, t, d = x.shape
    nh = oa.shape[2]
    row = lambda i, j: (i, j, 0)
    const = lambda i, j: (0, 0)
    with_router = router is not None
    in_specs = [pl.BlockSpec((1, tm, nh), row), pl.BlockSpec((1, tm, nh), row), pl.BlockSpec((1, tm, d), row),
                pl.BlockSpec((1, 1, d), lambda i, j: (i, 0, 2)),
                pl.BlockSpec((1, 1, d), lambda i, j: (i, 0, 3)),
                pl.BlockSpec((1, 1, d), lambda i, j: (i, 0, 4)),
                pl.BlockSpec((1, d), const), pl.BlockSpec((nh, d), const), pl.BlockSpec((nh, d), const)]
    out_specs = [pl.BlockSpec((1, tm, d), row), pl.BlockSpec((1, tm, d), row)]
    args = [oa, ob, x, mod, mod, mod, g, woa, wob]
    if with_router:
        out_shape = [jax.ShapeDtypeStruct((b, t, d), F32), jax.ShapeDtypeStruct((b, t, d), F32),
                     jax.ShapeDtypeStruct((b, t, LANES), F32)]
        in_specs.append(pl.BlockSpec((d, LANES), const))
        out_specs.append(pl.BlockSpec((1, tm, LANES), row))
        args.append(router)
    else:
        out_shape = [jax.ShapeDtypeStruct((b, t, d), F32), jax.ShapeDtypeStruct((b, t, d), BF16)]
    return pl.pallas_call(
        functools.partial(_out_proj_kernel, with_router=with_router),
        grid=(b, t // tm),
        in_specs=in_specs, out_specs=out_specs, out_shape=out_shape,
        compiler_params=_cparams(("arbitrary", "arbitrary"), 40),
        name="out_proj_router" if with_router else "out_proj",
    )(*args)


def _final_norm(x2, fg_ref):
    rs = lax.rsqrt(jnp.mean(x2 * x2, axis=-1, keepdims=True) + NORM_EPS)
    return x2 * rs * fg_ref[...]


def _ffn_kernel(h_ref, x_ref, gt_ref, wg_ref, wu_ref, wd_ref, fg_ref, o_ref, *, nf, final):
    h = h_ref[0]
    ff = wg_ref.shape[1]
    tf = ff // nf
    acc = None
    for f in range(nf):
        cols = slice(f * tf, (f + 1) * tf)
        act = (_silu(_mm(h, wg_ref[:, cols])) * _mm(h, wu_ref[:, cols])).astype(BF16)
        part = _mm(act, wd_ref[cols, :])
        acc = part if acc is None else acc + part
    x2 = x_ref[0] + gt_ref[0] * acc
    o_ref[0] = _final_norm(x2, fg_ref) if final else x2


def _ffn(h, x, mod, wg, wu, wd, fg, final, tm):
    b, t, d = x.shape
    ff = wg.shape[1]
    row = lambda i, j: (i, j, 0)
    const = lambda i, j: (0, 0)
    once = pl.Buffered(1)
    return pl.pallas_call(
        functools.partial(_ffn_kernel, nf=2, final=final),
        grid=(b, t // tm),
        in_specs=[pl.BlockSpec((1, tm, d), row), pl.BlockSpec((1, tm, d), row),
                  pl.BlockSpec((1, 1, d), lambda i, j: (i, 0, 5)),
                  pl.BlockSpec((d, ff), const, pipeline_mode=once),
                  pl.BlockSpec((d, ff), const, pipeline_mode=once),
                  pl.BlockSpec((ff, d), const, pipeline_mode=once),
                  pl.BlockSpec((1, d), const)],
        out_specs=pl.BlockSpec((1, tm, d), row),
        out_shape=jax.ShapeDtypeStruct((b, t, d), F32),
        compiler_params=_cparams(("arbitrary", "arbitrary"), 52),
        name="ffn_dense",
    )(h, x, mod, wg, wu, wd, fg)


def _route_kernel(lg_ref, info_ref, p_ref, cnt_ref, carry_ref, *, tr):
    i = pl.program_id(0)

    @pl.when(i == 0)
    def _():
        carry_ref[...] = jnp.zeros_like(carry_ref)

    lane = _iota2((tr, LANES), 1)
    lg = jnp.where(lane < N_EXPERTS, lg_ref[...], NEG_BIG)
    m1 = jnp.max(lg, axis=-1, keepdims=True)
    e1 = jnp.min(jnp.where(lg == m1, lane, LANES), axis=-1, keepdims=True)
    lg2 = jnp.where(lane == e1, NEG_BIG, lg)
    m2 = jnp.max(lg2, axis=-1, keepdims=True)
    e2 = jnp.min(jnp.where(lg2 == m2, lane, LANES), axis=-1, keepdims=True)
    ex = jnp.exp(m2 - m1)
    den = 1.0 + ex
    p1 = 1.0 / den
    p2 = ex / den
    oh1 = lane == e1
    oh2 = lane == e2
    oh = jnp.where(oh1 | oh2, 1.0, 0.0)
    lstrict = (_iota2((tr, tr), 0) > _iota2((tr, tr), 1)).astype(BF16)
    cex = _mm(lstrict, oh.astype(BF16)) + carry_ref[...]
    rank1 = jnp.sum(jnp.where(oh1, cex, 0.0), axis=-1, keepdims=True).astype(jnp.int32)
    rank2 = jnp.sum(jnp.where(oh2, cex, 0.0), axis=-1, keepdims=True).astype(jnp.int32)
    carry_ref[...] = carry_ref[...] + jnp.sum(oh, axis=0, keepdims=True)
    info_ref[...] = jnp.where(lane == 0, e1, jnp.where(lane == 1, e2, jnp.where(lane == 2, rank1,
                              jnp.where(lane == 3, rank2, 0))))
    p_ref[...] = jnp.where(lane == 0, p1, jnp.where(lane == 1, p2, 0.0))

    @pl.when(i == pl.num_programs(0) - 1)
    def _():
        cnt_ref[...] = carry_ref[...].astype(jnp.int32)


def _route(logits, tr):
    n = logits.shape[0]
    blk = pl.BlockSpec((tr, LANES), lambda i: (i, 0))
    return pl.pallas_call(
        functools.partial(_route_kernel, tr=tr),
        grid=(n // tr,),
        in_specs=[blk],
        out_specs=[blk, blk, pl.BlockSpec((1, LANES), lambda i: (0, 0))],
        out_shape=[jax.ShapeDtypeStruct((n, LANES), jnp.int32), jax.ShapeDtypeStruct((n, LANES), F32),
                   jax.ShapeDtypeStruct((1, LANES), jnp.int32)],
        scratch_shapes=[pltpu.VMEM((1, LANES), F32)],
        compiler_params=_cparams(("arbitrary",), 24),
        name="moe_route",
    )(logits)


def _dest_kernel(info_ref, sp_ref, d_ref):
    info = info_ref[...]
    lane = _iota2(info.shape, 1)
    sp = sp_ref[...].astype(F32)
    start1 = jnp.sum(jnp.where(lane == info[:, 0:1], sp, 0.0), axis=-1, keepdims=True).astype(jnp.int32)
    start2 = jnp.sum(jnp.where(lane == info[:, 1:2], sp, 0.0), axis=-1, keepdims=True).astype(jnp.int32)
    d_ref[...] = jnp.where(lane == 0, start1 + info[:, 2:3], jnp.where(lane == 1, start2 + info[:, 3:4], 0))


def _dest(info, sp_row, tr):
    n = info.shape[0]
    blk = pl.BlockSpec((tr, LANES), lambda i: (i, 0))
    return pl.pallas_call(
        _dest_kernel,
        grid=(n // tr,),
        in_specs=[blk, pl.BlockSpec((1, LANES), lambda i: (0, 0))],
        out_specs=blk,
        out_shape=jax.ShapeDtypeStruct((n, LANES), jnp.int32),
        compiler_params=_cparams(("arbitrary",), 24),
        name="moe_dest",
    )(info, sp_row)


ROW_DMA_UNROLL = 8


def _dispatch_kernel(d1_ref, d2_ref, sp_ref, ep_ref, h_ref, xb_ref, zero_ref, sem, zsem, *, td):
    i = pl.program_id(0)

    @pl.when(i == 0)
    def _():
        zero_ref[...] = jnp.zeros_like(zero_ref)

        def zero_block(start):
            cp = pltpu.make_async_copy(zero_ref, xb_ref.at[pl.ds(pl.multiple_of(start, MOE_ROWS), MOE_ROWS), :], zsem)
            cp.start()
            cp.wait()

        for e in range(N_EXPERTS):
            @pl.when(ep_ref[e] > sp_ref[e])
            def _():
                zero_block(ep_ref[e] - MOE_ROWS)

        def tail(blk, carry):
            zero_block(blk * MOE_ROWS)
            return carry

        lax.fori_loop(ep_ref[N_EXPERTS - 1] // MOE_ROWS, xb_ref.shape[0] // MOE_ROWS, tail, 0)

    base = i * td

    def row_copy(j, dst):
        return pltpu.make_async_copy(h_ref.at[pl.ds(j, 1), :], xb_ref.at[pl.ds(dst, 1), :], sem)

    def issue(j8, carry):
        for u in range(ROW_DMA_UNROLL):
            j = j8 * ROW_DMA_UNROLL + u
            row_copy(j, d1_ref[base + j]).start(priority=0)
            row_copy(j, d2_ref[base + j]).start(priority=1)
        return carry

    lax.fori_loop(0, td // ROW_DMA_UNROLL, issue, 0)
    for _ in range(2):
        pltpu.make_async_copy(h_ref, xb_ref.at[pl.ds(0, td), :], sem).wait()


def _dispatch(d1, d2, sp, ep, h2, n_rows, td):
    n, d = h2.shape
    return pl.pallas_call(
        functools.partial(_dispatch_kernel, td=td),
        grid_spec=pltpu.PrefetchScalarGridSpec(
            num_scalar_prefetch=4,
            grid=(n // td,),
            in_specs=[pl.BlockSpec((td, d), lambda i, *_: (i, 0))],
            out_specs=pl.BlockSpec(memory_space=pl.ANY),
            scratch_shapes=[pltpu.VMEM((MOE_ROWS, d), F32), pltpu.SemaphoreType.DMA, pltpu.SemaphoreType.DMA]),
        out_shape=jax.ShapeDtypeStruct((n_rows, d), F32),
        compiler_params=_cparams(("arbitrary",), 24),
        name="moe_dispatch",
    )(d1, d2, sp, ep, h2)


def _expert_kernel(be_ref, nu_ref, xb_ref, wg_ref, wu_ref, wd_ref, yb_ref):
    j = pl.program_id(0)
    f = pl.program_id(1)

    @pl.when(j < nu_ref[0])
    def _():
        xg = xb_ref[...].astype(BF16)
        act = (_silu(_mm(xg, wg_ref[0])) * _mm(xg, wu_ref[0])).astype(BF16)
        part = _mm(act, wd_ref[0])

        @pl.when(f == 0)
        def _():
            yb_ref[...] = part

        @pl.when(f > 0)
        def _():
            yb_ref[...] = yb_ref[...] + part

    @pl.when((j >= nu_ref[0]) & (f == 0))
    def _():
        yb_ref[...] = jnp.zeros_like(yb_ref)


def _experts(block_e, n_used, xb, wg, wu, wd, tf):
    n_rows, d = xb.shape
    ff = wg.shape[2]
    nf = ff // tf
    n_blocks = n_rows // MOE_ROWS

    def blk(j, f, be, nu):
        return (jnp.minimum(j, nu[0] - 1), 0)

    def fcol(j, f, be, nu):
        return jnp.where(j < nu[0], f, nf - 1)

    return pl.pallas_call(
        _expert_kernel,
        grid_spec=pltpu.PrefetchScalarGridSpec(
            num_scalar_prefetch=2,
            grid=(n_blocks, nf),
            in_specs=[pl.BlockSpec((MOE_ROWS, d), blk),
                      pl.BlockSpec((1, d, tf), lambda j, f, be, nu: (be[j], 0, fcol(j, f, be, nu))),
                      pl.BlockSpec((1, d, tf), lambda j, f, be, nu: (be[j], 0, fcol(j, f, be, nu))),
                      pl.BlockSpec((1, tf, d), lambda j, f, be, nu: (be[j], fcol(j, f, be, nu), 0))],
            out_specs=pl.BlockSpec((MOE_ROWS, d), lambda j, f, be, nu: (j, 0))),
        out_shape=jax.ShapeDtypeStruct((n_rows, d), F32),
        compiler_params=_cparams(("arbitrary", "arbitrary"), 48),
        name="moe_experts",
    )(block_e, n_used, xb, wg, wu, wd)


def _combine_kernel(d1_ref, d2_ref, x_ref, gt_ref, p_ref, fg_ref, yb_ref, o_ref, y1_ref, y2_ref, sem, *, tm, final):
    base = (pl.program_id(0) * pl.num_programs(1) + pl.program_id(1)) * tm

    def row_copy(src, buf_ref, j):
        return pltpu.make_async_copy(yb_ref.at[pl.ds(src, 1), :], buf_ref.at[pl.ds(j, 1), :], sem)

    def issue(j8, carry):
        for u in range(min(ROW_DMA_UNROLL, tm)):
            j = j8 * ROW_DMA_UNROLL + u
            row_copy(d1_ref[base + j], y1_ref, j).start(priority=0)
            row_copy(d2_ref[base + j], y2_ref, j).start(priority=1)
        return carry

    lax.fori_loop(0, tm // ROW_DMA_UNROLL, issue, 0)
    for buf_ref in (y1_ref, y2_ref):
        pltpu.make_async_copy(yb_ref.at[pl.ds(0, tm), :], buf_ref, sem).wait()

    p = p_ref[...]
    f = y1_ref[...] * p[:, 0:1] + y2_ref[...] * p[:, 1:2]
    x2 = x_ref[0] + gt_ref[0] * f
    o_ref[0] = _final_norm(x2, fg_ref) if final else x2


def _combine(d1, d2, x, mod, probs, fg, yb, final, tm):
    b, t, d = x.shape
    nt = t // tm
    return pl.pallas_call(
        functools.partial(_combine_kernel, tm=tm, final=final),
        grid_spec=pltpu.PrefetchScalarGridSpec(
            num_scalar_prefetch=2,
            grid=(b, nt),
            in_specs=[pl.BlockSpec((1, tm, d), lambda i, j, *_: (i, j, 0)),
                      pl.BlockSpec((1, 1, d), lambda i, j, *_: (i, 0, 5)),
                      pl.BlockSpec((tm, LANES), lambda i, j, *_: (i * nt + j, 0)),
                      pl.BlockSpec((1, d), lambda i, j, *_: (0, 0)),
                      pl.BlockSpec(memory_space=pl.ANY)],
            out_specs=pl.BlockSpec((1, tm, d), lambda i, j, *_: (i, j, 0)),
            scratch_shapes=[pltpu.VMEM((tm, d), F32), pltpu.VMEM((tm, d), F32), pltpu.SemaphoreType.DMA]),
        out_shape=jax.ShapeDtypeStruct((b, t, d), F32),
        compiler_params=_cparams(("arbitrary", "arbitrary"), 24),
        name="moe_combine",
    )(d1, d2, x, mod, probs, fg, yb)


def _moe(h2, logits, x1, mod, wg, wu, wd, fg, final, tm):
    b, t, d = x1.shape
    n = b * t
    info, probs, cnt = _route(logits.reshape(n, LANES), min(n, 256))
    counts = cnt[0, :N_EXPERTS]
    padded = (counts + MOE_ROWS - 1) // MOE_ROWS * MOE_ROWS
    ep = jnp.cumsum(padded).astype(jnp.int32)
    sp = ep - padded
    n_blocks = -(-(2 * n) // MOE_ROWS) + N_EXPERTS
    n_used = jnp.maximum(ep[-1] // MOE_ROWS, 1).astype(jnp.int32)
    blk_start = jnp.minimum(jnp.arange(n_blocks, dtype=jnp.int32), n_used - 1) * MOE_ROWS
    block_e = jnp.minimum(jnp.sum(blk_start[:, None] >= ep[None, :], axis=1), N_EXPERTS - 1).astype(jnp.int32)
    dest = _dest(info, jnp.pad(sp, (0, LANES - N_EXPERTS))[None, :], min(n, 1024))
    d1, d2 = dest[:, 0], dest[:, 1]
    xb = _dispatch(d1, d2, sp, ep, h2.reshape(n, d), n_blocks * MOE_ROWS, min(n, 256))
    yb = _experts(block_e, n_used.reshape(1), xb, wg, wu, wd, 896)
    return _combine(d1, d2, x1, mod, probs, fg, yb, final, tm)


def _pad_rows_front(a, rows):
    pad = [(0, 0)] * a.ndim
    pad[-2] = (rows - a.shape[-2], 0)
    return jnp.pad(a, pad)


def _trunk(x, mod, conv0, gdn0, shift0, rwkv0, w):
    b, t, d = x.shape
    depth = w["wa"].shape[0]
    tm = min(t, 512)
    tc = min(t, 256)
    convs, gdns, shifts, rwkvs = [], [], [], []
    for l in range(depth):
        ml = mod[l]
        pa, pab, pb = _norm_proj(x, ml, w["norm1_g"][l], w["wa"][l], w["wab"][l], w["wb"][l], tm)
        oa, sg = _gdn(pa, pab, _pad_rows_front(conv0[l], SUBLANES), gdn0[l], w["conv_w"][l], w["alog"][l],
                      w["dtb"][l], w["onorm_g"][l], tc)
        ob, sr = _rwkv(pb, _pad_rows_front(shift0[l], SUBLANES), rwkv0[l], [v[l] for v in w["rwkv_vecs"]],
                       w["w2p"][l], w["a2p"][l], w["g2"][l], w["bd"], tc)
        convs.append(pa[:, t - (GDN_CONV - 1):, :w["conv_w"].shape[2]])
        shifts.append(pb[:, t - 1:, :])
        gdns.append(sg)
        rwkvs.append(sr)
        final = l == depth - 1
        j = l // 2
        if l % 2 == 0:
            x1, h2 = _out_proj(oa, ob, x, ml, w["norm2_g"][l], w["woa"][l], w["wob"][l], None, tm)
            x = _ffn(h2, x1, ml, w["ffn_g"][j], w["ffn_u"][j], w["ffn_d"][j], w["final_g"], final, tm)
        else:
            x1, h2, lg = _out_proj(oa, ob, x, ml, w["norm2_g"][l], w["woa"][l], w["wob"][l], w["router"][j], tm)
            x = _moe(h2, lg, x1, ml, w["moe_g"][j], w["moe_u"][j], w["moe_d"][j], w["final_g"], final, min(t, 256))
    return x, jnp.stack(convs), jnp.stack(gdns), jnp.stack(shifts), jnp.stack(rwkvs)


def kernel(x_prompt, x_sample, c_prompt, c_sample, state_gdn_conv, state_gdn, state_rwkv_shift, state_rwkv, w_ada, b_ada, norm1_g, norm2_g, w_in, gdn_conv_w, gdn_a_log, gdn_dt_bias, gdn_onorm_g, rwkv_mu, rwkv_w0, rwkv_w2, rwkv_a0, rwkv_a2, rwkv_g2, rwkv_k_k, rwkv_k_a, rwkv_r_k, rwkv_ln_w, rwkv_ln_b, w_out, ffn_w_gate, ffn_w_up, ffn_w_down, moe_router, moe_w_gate, moe_w_up, moe_w_down, final_g):
    depth, d, _ = w_in.shape
    nbp = x_prompt.shape[0]
    nbs = x_sample.shape[0]
    nqkvz = 4 * GDN_HEADS * GDN_D
    nab = 2 * GDN_HEADS
    nw = RWKV_HEADS * RWKV_N
    lora_w = rwkv_w2.shape[1]

    def lane_pad(v):
        return jnp.pad(v, ((0, 0), (0, LANES - v.shape[1])))[:, None, :]

    rows = lambda v: v[:, None, :]
    hi = jnp.arange(nw) // RWKV_N
    w = dict(
        norm1_g=rows(norm1_g), norm2_g=rows(norm2_g), final_g=final_g[None, :],
        wa=w_in[:, :, :nqkvz].astype(BF16),
        wab=jnp.pad(w_in[:, :, nqkvz:nqkvz + nab], ((0, 0), (0, 0), (0, LANES - nab))).astype(BF16),
        wb=w_in[:, :, nqkvz + nab:].astype(BF16),
        conv_w=gdn_conv_w, alog=lane_pad(gdn_a_log), dtb=lane_pad(gdn_dt_bias), onorm_g=rows(gdn_onorm_g),
        rwkv_vecs=[rows(rwkv_mu), rows(rwkv_w0), rows(rwkv_a0), rows(rwkv_k_k), rows(rwkv_k_a),
                   rwkv_r_k.reshape(depth, 1, nw), rows(rwkv_ln_w), rows(rwkv_ln_b)],
        w2p=jnp.pad(rwkv_w2, ((0, 0), (0, LANES - lora_w), (0, 0))),
        a2p=jnp.pad(rwkv_a2, ((0, 0), (lora_w, LANES - lora_w - rwkv_a2.shape[1]), (0, 0))),
        g2=rwkv_g2,
        bd=(hi[:, None] == hi[None, :]).astype(F32),
        woa=w_out[:, :GDN_HEADS * GDN_D, :].astype(BF16), wob=w_out[:, GDN_HEADS * GDN_D:, :].astype(BF16),
        ffn_g=ffn_w_gate.astype(BF16), ffn_u=ffn_w_up.astype(BF16), ffn_d=ffn_w_down.astype(BF16),
        router=jnp.pad(moe_router, ((0, 0), (0, 0), (0, LANES - moe_router.shape[2]))),
        moe_g=moe_w_gate.astype(BF16), moe_u=moe_w_up.astype(BF16), moe_d=moe_w_down.astype(BF16),
    )

    nb_all = nbp + nbs
    bp = -(-nb_all // SUBLANES) * SUBLANES
    c_all = jnp.pad(jnp.concatenate([c_prompt, c_sample], axis=0), ((0, bp - nb_all), (0, 0)))
    mod = _ada_mod(c_all, w_ada, b_ada)[:, :, None, :]
    mod_p, mod_s = mod[:, :nbp], mod[:, nbp:nb_all]

    dt = x_prompt.dtype
    zc = jnp.zeros((depth, nbp) + state_gdn_conv.shape[2:], dt)
    zg = jnp.zeros((depth, nbp) + state_gdn.shape[2:], dt)
    zs = jnp.zeros((depth, nbp) + state_rwkv_shift.shape[2:], dt)
    zr = jnp.zeros((depth, nbp) + state_rwkv.shape[2:], dt)
    y_p, p_conv, p_gdn, p_shift, p_rwkv = _trunk(x_prompt, mod_p, zc, zg, zs, zr, w)
    y_s, s_conv, s_gdn, s_shift, s_rwkv = _trunk(x_sample, mod_s, state_gdn_conv, state_gdn, state_rwkv_shift,
                                                 state_rwkv, w)
    return (y_p, y_s, p_conv, p_gdn, p_shift, p_rwkv, s_conv, s_gdn, s_shift, s_rwkv)
```

```python
import functools

import jax
import jax.numpy as jnp
from jax import lax
from jax.experimental import pallas as pl
from jax.experimental.pallas import tpu as pltpu

F32 = jnp.float32
BF16 = jnp.bfloat16
HI = lax.Precision.HIGHEST

LANES = 128
SUBLANES = 8
CHUNK = 64
GDN_HEADS = 4
GDN_D = 128
RWKV_HEADS = 8
RWKV_N = 64
GDN_CONV = 4
N_EXPERTS = 8
NORM_EPS = 1e-6
L2_EPS = 1e-6
GN_EPS = 64e-5
NEG_BIG = -1e30
TOKEN_TILE = 512
MOE_ROWS = 512
MOE_FF_TILE = 1792
MIB = 1024 * 1024

P_GDN_SCORE = 1
P_GDN_INV = 3
P_GDN_SOLVE = 1
P_GDN_STATE = 1
P_RWKV_SMALL = 1
P_RWKV_SCORE = 1
P_RWKV_INV = 1
P_RWKV_MID = 1
P_RWKV_STATE = 1
P_ROUTER = 3


def _cparams(sem, vmem_mib):
    return pltpu.CompilerParams(dimension_semantics=sem, vmem_limit_bytes=vmem_mib * MIB)


def _split_bf16(x):
    hi = x.astype(BF16)
    return hi, (x - hi.astype(F32)).astype(BF16)


def _split3_bf16(x):
    hi = x.astype(BF16)
    r1 = x - hi.astype(F32)
    mid = r1.astype(BF16)
    return hi, mid, (r1 - mid.astype(F32)).astype(BF16)


_CONTRACT = {"nn": (1, 0), "nt": (1, 1), "tn": (0, 0)}


def _dg(a, b, kind, prec):
    off = a.ndim - 2
    ca, cb = _CONTRACT[kind]
    dn = (((ca + off,), (cb + off,)), (((0,), (0,)) if off else ((), ())))
    dot = lambda x, y: lax.dot_general(x, y, dn, preferred_element_type=F32)
    if prec is None or prec is HI:
        return lax.dot_general(a, b, dn, preferred_element_type=F32, precision=prec)
    if prec == 1:
        return dot(a.astype(BF16), b.astype(BF16))
    if prec == "l3":
        bb = b.astype(BF16)
        a0, a1, a2 = _split3_bf16(a)
        return dot(a0, bb) + dot(a1, bb) + dot(a2, bb)
    if prec == "r3":
        ab = a.astype(BF16)
        b0, b1, b2 = _split3_bf16(b)
        return dot(ab, b0) + dot(ab, b1) + dot(ab, b2)
    ah, al = _split_bf16(a)
    bh, bl = _split_bf16(b)
    return dot(ah, bh) + dot(ah, bl) + dot(al, bh)


def _mm(a, b, prec=None):
    return _dg(a, b, "nn", prec)


def _mm_nt(a, b, prec=None):
    return _dg(a, b, "nt", prec)


def _mm_tn(a, b, prec=None):
    return _dg(a, b, "tn", prec)


def _silu(x):
    return x * jax.nn.sigmoid(x)


def _softplus(x):
    return jnp.maximum(x, 0.0) + jnp.log1p(jnp.exp(-jnp.abs(x)))


def _iota2(shape, dim):
    return lax.broadcasted_iota(jnp.int32, shape, dim)


def _per_segment(x, fn, *mod_refs):
    mods = [r[0] for r in mod_refs]
    nseg = mods[0].shape[0]
    if nseg == 1:
        return fn(x, *mods)
    rows = x.shape[0] // nseg
    return jnp.concatenate([fn(x[s * rows:(s + 1) * rows], *(m[s:s + 1] for m in mods)) for s in range(nseg)], axis=0)


def _scale_shift(xn, sc_ref, sh_ref):
    return _per_segment(xn, lambda x, sc, sh: x * (1.0 + sc) + sh, sc_ref, sh_ref)


def _gated(f, gt_ref):
    return _per_segment(f, lambda x, gt: gt * x, gt_ref)


def _neumann_inverse(x, eye, prec):
    c = x.shape[-1]
    z = jnp.concatenate([x, jnp.broadcast_to(eye, x.shape)], axis=-1)
    keep_s = _iota2((c, 2 * c), 1) >= c
    for _ in range(6):
        z = _mm(z[..., :c], z, prec) + jnp.where(keep_s, z, 0.0)
    return z[..., c:]


def _ada_kernel(c_ref, w_ref, b_ref, o_ref):
    o_ref[0] = _mm(_silu(c_ref[...]), w_ref[0], HI) + b_ref[0]


def _ada_mod(c_all, w_ada, b_ada):
    depth, d, n6 = w_ada.shape
    bp = c_all.shape[0]
    tn = 1536
    return pl.pallas_call(
        _ada_kernel,
        grid=(depth, n6 // tn),
        in_specs=[pl.BlockSpec((bp, d), lambda l, j: (0, 0)),
                  pl.BlockSpec((1, d, tn), lambda l, j: (l, 0, j)),
                  pl.BlockSpec((1, 1, tn), lambda l, j: (l, 0, j))],
        out_specs=pl.BlockSpec((1, bp, tn), lambda l, j: (l, 0, j)),
        out_shape=jax.ShapeDtypeStruct((depth, bp, n6), F32),
        compiler_params=_cparams(("arbitrary", "arbitrary"), 40),
        name="ada_mod",
    )(c_all, w_ada, b_ada.reshape(depth, 1, n6))


def _norm_proj_kernel(x_ref, sh_ref, sc_ref, g_ref, wa_ref, wab_ref, wb_ref, pa_ref, pab_ref, pb_ref):
    x = x_ref[0]
    rs = lax.rsqrt(jnp.mean(x * x, axis=-1, keepdims=True) + NORM_EPS)
    hb = _scale_shift(x * rs * g_ref[...], sc_ref, sh_ref).astype(BF16)
    pa_ref[0] = _mm(hb, wa_ref[...])
    pab_ref[0] = _mm(hb, wab_ref[...])
    pb_ref[0] = _mm(hb, wb_ref[...])


def _norm_proj(x, mod, g, wa, wab, wb, tm):
    b, t, d = x.shape
    na, nab, nb = wa.shape[1], wab.shape[1], wb.shape[1]
    nseg = mod.shape[1]
    row = lambda i, j: (i, j, 0)
    const = lambda i, j: (0, 0)
    return pl.pallas_call(
        _norm_proj_kernel,
        grid=(b, t // tm),
        in_specs=[pl.BlockSpec((1, tm, d), row),
                  pl.BlockSpec((1, nseg, d), lambda i, j: (i, 0, 0)),
                  pl.BlockSpec((1, nseg, d), lambda i, j: (i, 0, 1)),
                  pl.BlockSpec((1, d), const),
                  pl.BlockSpec((d, na), const),
                  pl.BlockSpec((d, nab), const),
                  pl.BlockSpec((d, nb), const)],
        out_specs=[pl.BlockSpec((1, tm, na), row), pl.BlockSpec((1, tm, nab), row), pl.BlockSpec((1, tm, nb), row)],
        out_shape=[jax.ShapeDtypeStruct((b, t, na), F32), jax.ShapeDtypeStruct((b, t, nab), F32),
                   jax.ShapeDtypeStruct((b, t, nb), F32)],
        compiler_params=_cparams(("arbitrary", "arbitrary"), 48),
        name="norm_proj",
    )(x, mod, mod, g, wa, wab, wb)


def _gdn_kernel(pa_ref, pab_ref, cbuf_ref, s0_ref, cw_ref, alog_ref, dtb_ref, og_ref,
                o_ref, sfin_ref, xc_ref, s_ref, *, tc):
    t = pl.program_id(1)
    nqk = GDN_HEADS * GDN_D
    nconv = 3 * nqk

    @pl.when(t == 0)
    def _():
        xc_ref[0:SUBLANES, :] = cbuf_ref[0]
        s_ref[...] = s0_ref[0]

    xc_ref[SUBLANES:SUBLANES + tc, :] = pa_ref[0, :, 0:nconv]
    cw = cw_ref[...]
    y = xc_ref[5:5 + tc, :] * cw[0:1, :]
    for i in range(1, GDN_CONV):
        y = y + xc_ref[5 + i:5 + i + tc, :] * cw[i:i + 1, :]
    xc_ref[0:SUBLANES, :] = xc_ref[tc:tc + SUBLANES, :]
    qkv = _silu(y)

    ab = pab_ref[0]
    gmat = -jnp.exp(alog_ref[...]) * _softplus(ab + dtb_ref[...])
    bmat = jax.nn.sigmoid(ab)

    c = CHUNK
    nch = tc // c
    nh = GDN_HEADS
    nb = nch * nh
    ri = _iota2((c, c), 0)
    ci = _iota2((c, c), 1)
    incl = ri >= ci
    strict = ri > ci
    eye_c = (ri == ci).astype(F32)

    def heads(x, width):
        x = x.reshape(nch, c, nh * width)
        return jnp.stack([x[:, :, h * width:(h + 1) * width] for h in range(nh)], axis=1).reshape(nb, c, width)

    q = heads(qkv[:, 0:nqk], GDN_D)
    k = heads(qkv[:, nqk:2 * nqk], GDN_D)
    v = heads(qkv[:, 2 * nqk:3 * nqk], GDN_D)
    q = q * lax.rsqrt(jnp.sum(q * q, axis=-1, keepdims=True) + L2_EPS) * (GDN_D ** -0.5)
    k = k * lax.rsqrt(jnp.sum(k * k, axis=-1, keepdims=True) + L2_EPS)
    beta = heads(bmat[:, nh:2 * nh], 1)

    g4 = gmat.reshape(nch, c, LANES)
    gcol4 = _mm(jnp.broadcast_to(incl.astype(F32), (nch, c, c)), g4, "r3")
    grow4 = _mm_tn(g4, jnp.broadcast_to((ri <= ci).astype(F32), (nch, c, c)), "l3")
    gc = jnp.stack([gcol4[:, :, h:h + 1] for h in range(nh)], axis=1).reshape(nb, c, 1)
    gr = jnp.stack([grow4[:, h:h + 1, :] for h in range(nh)], axis=1).reshape(nb, 1, c)
    decay = jnp.exp(jnp.where(incl, gc - gr, NEG_BIG))

    kb = k * beta
    m = jnp.where(strict, _mm_nt(kb, k, P_GDN_SCORE) * decay, 0.0)
    tinv = _neumann_inverse(-m, eye_c, P_GDN_INV)
    egc = jnp.exp(gc)
    sol = _mm(tinv, jnp.concatenate([v * beta, kb * egc], axis=2), P_GDN_SOLVE)
    u, w = sol[:, :, :GDN_D], sol[:, :, GDN_D:]
    aqk = _mm_nt(q, k, P_GDN_SCORE) * decay
    gl = gc[:, c - 1:c, :]
    kd = k * jnp.exp(gl - gc)
    qg = q * egc
    egl = jnp.exp(gl)

    s = s_ref[...]
    outs = []
    for ch in range(nch):
        bs = slice(ch * nh, (ch + 1) * nh)
        v_new = u[bs] - _mm(w[bs], s, P_GDN_STATE)
        outs.append(_mm(qg[bs], s, P_GDN_STATE) + _mm(aqk[bs], v_new, P_GDN_STATE))
        s = s * egl[bs] + _mm_tn(kd[bs], v_new, P_GDN_STATE)
    s_ref[...] = s

    o = jnp.stack(outs, axis=0)
    o = o * lax.rsqrt(jnp.mean(o * o, axis=-1, keepdims=True) + NORM_EPS) * og_ref[...]
    for h in range(nh):
        sl = slice(h * GDN_D, (h + 1) * GDN_D)
        z = pa_ref[0, :, nconv + h * GDN_D: nconv + (h + 1) * GDN_D]
        o_ref[0, :, sl] = (o[:, h].reshape(tc, GDN_D) * _silu(z)).astype(o_ref.dtype)

    @pl.when(t == pl.num_programs(1) - 1)
    def _():
        sfin_ref[0] = s_ref[...]


def _gdn(pa, pab, cbuf8, s0, cw, alog, dtb, og, tc):
    b, t, na = pa.shape
    nconv = cw.shape[1]
    row = lambda i, j: (i, j, 0)
    const2 = lambda i, j: (0, 0)
    return pl.pallas_call(
        functools.partial(_gdn_kernel, tc=tc),
        grid=(b, t // tc),
        in_specs=[pl.BlockSpec((1, tc, na), row),
                  pl.BlockSpec((1, tc, LANES), row),
                  pl.BlockSpec((1, SUBLANES, nconv), lambda i, j: (i, 0, 0)),
                  pl.BlockSpec((1, GDN_HEADS, GDN_D, GDN_D), lambda i, j: (i, 0, 0, 0)),
                  pl.BlockSpec((GDN_CONV, nconv), const2),
                  pl.BlockSpec((1, LANES), const2),
                  pl.BlockSpec((1, LANES), const2),
                  pl.BlockSpec((1, GDN_D), const2)],
        out_specs=[pl.BlockSpec((1, tc, GDN_HEADS * GDN_D), row),
                   pl.BlockSpec((1, GDN_HEADS, GDN_D, GDN_D), lambda i, j: (i, 0, 0, 0))],
        out_shape=[jax.ShapeDtypeStruct((b, t, GDN_HEADS * GDN_D), BF16),
                   jax.ShapeDtypeStruct((b, GDN_HEADS, GDN_D, GDN_D), F32)],
        scratch_shapes=[pltpu.VMEM((tc + SUBLANES, nconv), F32),
                        pltpu.VMEM((GDN_HEADS, GDN_D, GDN_D), F32)],
        compiler_params=_cparams(("arbitrary", "arbitrary"), 40),
        name="gdn_mixer",
    )(pa, pab, cbuf8, s0, cw, alog, dtb, og)


def _rwkv_kernel(pb_ref, sbuf_ref, s0_ref, mu_ref, w0_ref, w2_ref, a0_ref, a2_ref, g2_ref, kk_ref, ka_ref,
                 rk_ref, lnw_ref, lnb_ref, bd_ref, o_ref, sfin_ref, xc_ref, s_ref, y_ref, *, tc):
    t = pl.program_id(1)
    nw = RWKV_HEADS * RWKV_N

    @pl.when(t == 0)
    def _():
        xc_ref[0:SUBLANES, :] = sbuf_ref[0]
        s_ref[...] = s0_ref[0]

    x = pb_ref[0]
    xc_ref[SUBLANES:SUBLANES + tc, :] = x
    prev = xc_ref[SUBLANES - 1:SUBLANES - 1 + tc, :]
    xc_ref[0:SUBLANES, :] = xc_ref[tc:tc + SUBLANES, :]
    xs = x + (prev - x) * mu_ref[...]
    r = xs[:, 0:nw]
    kr = xs[:, nw:2 * nw]
    vr = xs[:, 2 * nw:3 * nw]
    wa = xs[:, 3 * nw:3 * nw + LANES]
    gd = xs[:, 3 * nw + LANES:3 * nw + 2 * LANES]

    bd = bd_ref[...]
    ps = P_RWKV_SMALL
    wlog = -_softplus(-(w0_ref[...] + _mm(jnp.tanh(wa), w2_ref[...], ps))) - 0.5
    lw = -jnp.exp(wlog)
    a = jax.nn.sigmoid(a0_ref[...] + _mm(wa, a2_ref[...], ps))
    gate = _mm(jax.nn.sigmoid(gd), g2_ref[...], ps)
    kkr = kr * kk_ref[...]
    kk = kkr * lax.rsqrt(_mm(kkr * kkr, bd, ps) + L2_EPS)
    kb = kr * (1.0 + (a - 1.0) * ka_ref[...])
    aa = -kk
    bb = kk * a
    bonus = _mm(r * kb * rk_ref[...], bd, ps) * vr

    c = CHUNK
    n = RWKV_N
    nch = tc // c
    nh = RWKV_HEADS
    nb = nch * nh
    ri = _iota2((c, c), 0)
    ci = _iota2((c, c), 1)
    incl = ri >= ci
    strict = ri > ci
    eye = ri == ci
    eye_c = eye.astype(F32)

    def heads(x):
        return jnp.stack([x[:, :, h * n:(h + 1) * n] for h in range(nh)], axis=1).reshape(nb, x.shape[1], n)

    chunks = lambda x: x.reshape(nch, c, nw)
    lw4 = chunks(lw)
    g = _mm(jnp.broadcast_to(incl.astype(F32), (nch, c, c)), lw4, "r3")
    gfin = g[:, c - 1:c, :]
    eng = jnp.exp(-g)
    tail = jnp.exp(gfin - g)
    aa4, bb4, kb4 = chunks(aa), chunks(bb), chunks(kb)
    at = heads(aa4 * jnp.exp(g - lw4))
    bt = heads(bb4 * eng)
    kt = heads(kb4 * eng)
    rt = heads(chunks(r) * jnp.exp(g))
    bh = heads(bb4 * tail)
    kh = heads(kb4 * tail)
    v = heads(chunks(vr))
    egf = heads(jnp.exp(gfin))

    pm = P_RWKV_MID
    ar = jnp.concatenate([at, rt], axis=1)
    xb_ = _mm_nt(ar, bt, P_RWKV_SCORE)
    xk_ = _mm_nt(ar, kt, P_RWKV_SCORE)
    lab = jnp.where(strict, xb_[:, :c], 0.0)
    lak = jnp.where(strict, xk_[:, :c], 0.0)
    mrb = jnp.where(incl, xb_[:, c:], 0.0)
    mrk = jnp.where(incl, xk_[:, c:], 0.0)
    tinv = _neumann_inverse(lab, eye_c, P_RWKV_INV)
    wt = _mm(tinv, at, pm)
    u0 = _mm(tinv, _mm(lak, v, pm), pm)
    qt = rt + _mm(mrb, wt, pm)
    y0 = _mm(mrb, u0, pm) + _mm(mrk, v, pm)
    p = jnp.where(eye, egf, 0.0) + _mm_tn(bh, wt, pm)
    inc = _mm_tn(u0, bh, pm) + _mm_tn(v, kh, pm)

    s = s_ref[...]
    ys = []
    for ch in range(nch):
        bs = slice(ch * nh, (ch + 1) * nh)
        ys.append(_mm_nt(qt[bs], s, P_RWKV_STATE) + y0[bs])
        s = _mm_nt(s, p[bs], P_RWKV_STATE) + inc[bs]
    s_ref[...] = s
    y4 = jnp.stack(ys, axis=0)
    for h in range(nh):
        y_ref[:, h * n:(h + 1) * n] = y4[:, h].reshape(tc, n)

    y = y_ref[...]
    mean = _mm(y, bd, ps) * (1.0 / n)
    dlt = y - mean
    var = _mm(dlt * dlt, bd, ps) * (1.0 / n)
    yn = dlt * lax.rsqrt(var + GN_EPS) * lnw_ref[...] + lnb_ref[...]
    o_ref[0] = ((yn + bonus) * gate).astype(o_ref.dtype)

    @pl.when(t == pl.num_programs(1) - 1)
    def _():
        sfin_ref[0] = s_ref[...]


def _rwkv(pb, sbuf8, s0, vecs, w2p, a2p, g2, bd, tc):
    b, t, nb = pb.shape
    nw = RWKV_HEADS * RWKV_N
    row = lambda i, j: (i, j, 0)
    const2 = lambda i, j: (0, 0)
    mu, w0, a0, kk, ka, rk, lnw, lnb = vecs
    vec_spec = pl.BlockSpec((1, nw), const2)
    return pl.pallas_call(
        functools.partial(_rwkv_kernel, tc=tc),
        grid=(b, t // tc),
        in_specs=[pl.BlockSpec((1, tc, nb), row),
                  pl.BlockSpec((1, SUBLANES, nb), lambda i, j: (i, 0, 0)),
                  pl.BlockSpec((1, RWKV_HEADS, RWKV_N, RWKV_N), lambda i, j: (i, 0, 0, 0)),
                  pl.BlockSpec((1, nb), const2),
                  vec_spec,
                  pl.BlockSpec((LANES, nw), const2),
                  vec_spec,
                  pl.BlockSpec((LANES, nw), const2),
                  pl.BlockSpec((LANES, nw), const2),
                  vec_spec, vec_spec, vec_spec, vec_spec, vec_spec,
                  pl.BlockSpec((nw, nw), const2)],
        out_specs=[pl.BlockSpec((1, tc, nw), row),
                   pl.BlockSpec((1, RWKV_HEADS, RWKV_N, RWKV_N), lambda i, j: (i, 0, 0, 0))],
        out_shape=[jax.ShapeDtypeStruct((b, t, nw), BF16),
                   jax.ShapeDtypeStruct((b, RWKV_HEADS, RWKV_N, RWKV_N), F32)],
        scratch_shapes=[pltpu.VMEM((tc + SUBLANES, nb), F32),
                        pltpu.VMEM((RWKV_HEADS, RWKV_N, RWKV_N), F32),
                        pltpu.VMEM((tc, nw), F32)],
        compiler_params=_cparams(("arbitrary", "arbitrary"), 40),
        name="rwkv_mixer",
    )(pb, sbuf8, s0, mu, w0, w2p, a0, a2p, g2, kk, ka, rk, lnw, lnb, bd)


def _out_proj_kernel(oa_ref, ob_ref, x_ref, gt_ref, sh_ref, sc_ref, g_ref, woa_ref, wob_ref, *rest, with_router, nt):
    if with_router:
        wr_ref = rest[0]
        x1_ref, h_ref, lg_ref = rest[-3:]
    else:
        x1_ref, h_ref = rest

    @pl.when(pl.program_id(1) < nt)
    def _():
        mix = _mm(oa_ref[0], woa_ref[...]) + _mm(ob_ref[0], wob_ref[...])
        x1 = x_ref[0] + _gated(mix, gt_ref)
        x1_ref[0] = x1
        rs = lax.rsqrt(jnp.mean(x1 * x1, axis=-1, keepdims=True) + NORM_EPS)
        h = _scale_shift(x1 * rs * g_ref[...], sc_ref, sh_ref)
        if with_router:
            h_ref[...] = h
            lg_ref[...] = _mm(h, wr_ref[...], P_ROUTER)
        else:
            h_ref[0] = h.astype(h_ref.dtype)

    if with_router:
        @pl.when(pl.program_id(1) >= nt)
        def _():
            h_ref[...] = jnp.zeros_like(h_ref)
            lg_ref[...] = jnp.zeros_like(lg_ref)


def _out_proj(oa, ob, x, mod, g, woa, wob, tm, router=None, flat_rows=None, row_off=0, flat_prev=None):
    b, t, d = x.shape
    nh = oa.shape[2]
    nt = t // tm
    nseg = mod.shape[1]
    tail = 0
    if router is not None and flat_prev is None:
        assert b == 1 and row_off == 0 and (flat_rows - t) % tm == 0
        tail = (flat_rows - t) // tm
    row = lambda i, j: (i, jnp.minimum(j, nt - 1), 0)
    const = lambda i, j: (0, 0)
    in_specs = [pl.BlockSpec((1, tm, nh), row), pl.BlockSpec((1, tm, nh), row), pl.BlockSpec((1, tm, d), row),
                pl.BlockSpec((1, nseg, d), lambda i, j: (i, 0, 2)),
                pl.BlockSpec((1, nseg, d), lambda i, j: (i, 0, 3)),
                pl.BlockSpec((1, nseg, d), lambda i, j: (i, 0, 4)),
                pl.BlockSpec((1, d), const), pl.BlockSpec((nh, d), const), pl.BlockSpec((nh, d), const)]
    args = [oa, ob, x, mod, mod, mod, g, woa, wob]
    aliases = {}
    if router is None:
        out_specs = [pl.BlockSpec((1, tm, d), row), pl.BlockSpec((1, tm, d), row)]
        out_shape = [jax.ShapeDtypeStruct((b, t, d), F32), jax.ShapeDtypeStruct((b, t, d), BF16)]
    else:
        blk_off = row_off // tm
        flat = lambda i, j: (blk_off + i * nt + j, 0)
        in_specs.append(pl.BlockSpec((d, LANES), const))
        args.append(router)
        if flat_prev is not None:
            aliases = {len(args): 1, len(args) + 1: 2}
            in_specs += [pl.BlockSpec(memory_space=pl.ANY)] * 2
            args += list(flat_prev)
        out_specs = [pl.BlockSpec((1, tm, d), row), pl.BlockSpec((tm, d), flat), pl.BlockSpec((tm, LANES), flat)]
        out_shape = [jax.ShapeDtypeStruct((b, t, d), F32), jax.ShapeDtypeStruct((flat_rows, d), F32),
                     jax.ShapeDtypeStruct((flat_rows, LANES), F32)]
    return pl.pallas_call(
        functools.partial(_out_proj_kernel, with_router=router is not None, nt=nt),
        grid=(b, nt + tail),
        in_specs=in_specs, out_specs=out_specs, out_shape=out_shape, input_output_aliases=aliases,
        compiler_params=_cparams(("arbitrary", "arbitrary"), 40),
        name="out_proj" if router is None else "out_proj_router",
    )(*args)


def _final_norm(x2, fg_ref):
    rs = lax.rsqrt(jnp.mean(x2 * x2, axis=-1, keepdims=True) + NORM_EPS)
    return x2 * rs * fg_ref[...]


def _ffn_kernel(h_ref, x_ref, gt_ref, wg_ref, wu_ref, wd_ref, fg_ref, o_ref, *, nf, final):
    h = h_ref[0]
    ff = wg_ref.shape[1]
    tf = ff // nf
    acc = None
    for f in range(nf):
        cols = slice(f * tf, (f + 1) * tf)
        act = (_silu(_mm(h, wg_ref[:, cols])) * _mm(h, wu_ref[:, cols])).astype(BF16)
        part = _mm(act, wd_ref[cols, :])
        acc = part if acc is None else acc + part
    x2 = x_ref[0] + _gated(acc, gt_ref)
    o_ref[0] = _final_norm(x2, fg_ref) if final else x2


def _ffn(h, x, mod, wg, wu, wd, fg, final, tm):
    b, t, d = x.shape
    ff = wg.shape[1]
    row = lambda i, j: (i, j, 0)
    const = lambda i, j: (0, 0)
    once = pl.Buffered(1)
    return pl.pallas_call(
        functools.partial(_ffn_kernel, nf=2, final=final),
        grid=(b, t // tm),
        in_specs=[pl.BlockSpec((1, tm, d), row), pl.BlockSpec((1, tm, d), row),
                  pl.BlockSpec((1, mod.shape[1], d), lambda i, j: (i, 0, 5)),
                  pl.BlockSpec((d, ff), const, pipeline_mode=once),
                  pl.BlockSpec((d, ff), const, pipeline_mode=once),
                  pl.BlockSpec((ff, d), const, pipeline_mode=once),
                  pl.BlockSpec((1, d), const)],
        out_specs=pl.BlockSpec((1, tm, d), row),
        out_shape=jax.ShapeDtypeStruct((b, t, d), F32),
        compiler_params=_cparams(("arbitrary", "arbitrary"), 52),
        name="ffn_dense",
    )(h, x, mod, wg, wu, wd, fg)


def _route_kernel(lg_ref, info_ref, p_ref, cnt_ref, carry_ref, *, tr):
    i = pl.program_id(0)

    @pl.when(i == 0)
    def _():
        carry_ref[...] = jnp.zeros_like(carry_ref)

    lane = _iota2((tr, LANES), 1)
    lg = jnp.where(lane < N_EXPERTS, lg_ref[...], NEG_BIG)
    m1 = jnp.max(lg, axis=-1, keepdims=True)
    e1 = jnp.min(jnp.where(lg == m1, lane, LANES), axis=-1, keepdims=True)
    lg2 = jnp.where(lane == e1, NEG_BIG, lg)
    m2 = jnp.max(lg2, axis=-1, keepdims=True)
    e2 = jnp.min(jnp.where(lg2 == m2, lane, LANES), axis=-1, keepdims=True)
    ex = jnp.exp(m2 - m1)
    den = 1.0 + ex
    p1 = 1.0 / den
    p2 = ex / den
    oh1 = lane == e1
    oh2 = lane == e2
    oh = jnp.where(oh1 | oh2, 1.0, 0.0)
    lstrict = (_iota2((tr, tr), 0) > _iota2((tr, tr), 1)).astype(BF16)
    cex = _mm(lstrict, oh.astype(BF16)) + carry_ref[...]
    rank1 = jnp.sum(jnp.where(oh1, cex, 0.0), axis=-1, keepdims=True).astype(jnp.int32)
    rank2 = jnp.sum(jnp.where(oh2, cex, 0.0), axis=-1, keepdims=True).astype(jnp.int32)
    carry_ref[...] = carry_ref[...] + jnp.sum(oh, axis=0, keepdims=True)
    info_ref[...] = jnp.where(lane == 0, e1, jnp.where(lane == 1, e2, jnp.where(lane == 2, rank1,
                              jnp.where(lane == 3, rank2, 0))))
    p_ref[...] = jnp.where(lane == 0, p1, jnp.where(lane == 1, p2, 0.0))

    @pl.when(i == pl.num_programs(0) - 1)
    def _():
        cnt_ref[...] = carry_ref[...].astype(jnp.int32)


def _route(logits, tr):
    n = logits.shape[0]
    blk = pl.BlockSpec((tr, LANES), lambda i: (i, 0))
    return pl.pallas_call(
        functools.partial(_route_kernel, tr=tr),
        grid=(n // tr,),
        in_specs=[blk],
        out_specs=[blk, blk, pl.BlockSpec((1, LANES), lambda i: (0, 0))],
        out_shape=[jax.ShapeDtypeStruct((n, LANES), jnp.int32), jax.ShapeDtypeStruct((n, LANES), F32),
                   jax.ShapeDtypeStruct((1, LANES), jnp.int32)],
        scratch_shapes=[pltpu.VMEM((1, LANES), F32)],
        compiler_params=_cparams(("arbitrary",), 24),
        name="moe_route",
    )(logits)


def _dest_kernel(info_ref, sp_ref, d_ref):
    info = info_ref[...]
    lane = _iota2(info.shape, 1)
    sp = sp_ref[...].astype(F32)
    start1 = jnp.sum(jnp.where(lane == info[:, 0:1], sp, 0.0), axis=-1, keepdims=True).astype(jnp.int32)
    start2 = jnp.sum(jnp.where(lane == info[:, 1:2], sp, 0.0), axis=-1, keepdims=True).astype(jnp.int32)
    d_ref[...] = jnp.where(lane == 0, start1 + info[:, 2:3], jnp.where(lane == 1, start2 + info[:, 3:4], 0))


def _dest(info, sp_row, tr):
    n = info.shape[0]
    blk = pl.BlockSpec((tr, LANES), lambda i: (i, 0))
    return pl.pallas_call(
        _dest_kernel,
        grid=(n // tr,),
        in_specs=[blk, pl.BlockSpec((1, LANES), lambda i: (0, 0))],
        out_specs=blk,
        out_shape=jax.ShapeDtypeStruct((n, LANES), jnp.int32),
        compiler_params=_cparams(("arbitrary",), 24),
        name="moe_dest",
    )(info, sp_row)


ROW_DMA_UNROLL = 8


def _dispatch_kernel(d1_ref, d2_ref, sp_ref, ep_ref, h_ref, xb_ref, zero_ref, sem, zsem, *, td):
    i = pl.program_id(0)

    @pl.when(i == 0)
    def _():
        zero_ref[...] = jnp.zeros_like(zero_ref)

        def zero_block(start):
            cp = pltpu.make_async_copy(zero_ref, xb_ref.at[pl.ds(pl.multiple_of(start, MOE_ROWS), MOE_ROWS), :], zsem)
            cp.start()
            cp.wait()

        for e in range(N_EXPERTS):
            @pl.when(ep_ref[e] > sp_ref[e])
            def _():
                zero_block(ep_ref[e] - MOE_ROWS)

        def tail(blk, carry):
            zero_block(blk * MOE_ROWS)
            return carry

        lax.fori_loop(ep_ref[N_EXPERTS - 1] // MOE_ROWS, xb_ref.shape[0] // MOE_ROWS, tail, 0)

    base = i * td

    def row_copy(j, dst):
        return pltpu.make_async_copy(h_ref.at[pl.ds(j, 1), :], xb_ref.at[pl.ds(dst, 1), :], sem)

    def issue(j8, carry):
        for u in range(ROW_DMA_UNROLL):
            j = j8 * ROW_DMA_UNROLL + u
            row_copy(j, d1_ref[base + j]).start(priority=0)
            row_copy(j, d2_ref[base + j]).start(priority=1)
        return carry

    lax.fori_loop(0, td // ROW_DMA_UNROLL, issue, 0)
    for _ in range(2):
        pltpu.make_async_copy(h_ref, xb_ref.at[pl.ds(0, td), :], sem).wait()


def _dispatch(d1, d2, sp, ep, h2, n_rows, td):
    n, d = h2.shape
    return pl.pallas_call(
        functools.partial(_dispatch_kernel, td=td),
        grid_spec=pltpu.PrefetchScalarGridSpec(
            num_scalar_prefetch=4,
            grid=(n // td,),
            in_specs=[pl.BlockSpec((td, d), lambda i, *_: (i, 0))],
            out_specs=pl.BlockSpec(memory_space=pl.ANY),
            scratch_shapes=[pltpu.VMEM((MOE_ROWS, d), F32), pltpu.SemaphoreType.DMA, pltpu.SemaphoreType.DMA]),
        out_shape=jax.ShapeDtypeStruct((n_rows, d), F32),
        compiler_params=_cparams(("arbitrary",), 24),
        name="moe_dispatch",
    )(d1, d2, sp, ep, h2)


def _expert_kernel(be_ref, nu_ref, xb_ref, wg_ref, wu_ref, wd_ref, yb_ref):
    j = pl.program_id(0)
    f = pl.program_id(1)

    @pl.when(j < nu_ref[0])
    def _():
        xg = xb_ref[...].astype(BF16)
        act = (_silu(_mm(xg, wg_ref[0])) * _mm(xg, wu_ref[0])).astype(BF16)
        part = _mm(act, wd_ref[0])

        @pl.when(f == 0)
        def _():
            yb_ref[...] = part

        @pl.when(f > 0)
        def _():
            yb_ref[...] = yb_ref[...] + part

    @pl.when((j >= nu_ref[0]) & (f == 0))
    def _():
        yb_ref[...] = jnp.zeros_like(yb_ref)


def _experts(block_e, n_used, xb, wg, wu, wd, tf):
    n_rows, d = xb.shape
    ff = wg.shape[2]
    nf = ff // tf
    n_blocks = n_rows // MOE_ROWS

    def blk(j, f, be, nu):
        return (jnp.minimum(j, nu[0] - 1), 0)

    def fcol(j, f, be, nu):
        return jnp.where(j < nu[0], f, nf - 1)

    return pl.pallas_call(
        _expert_kernel,
        grid_spec=pltpu.PrefetchScalarGridSpec(
            num_scalar_prefetch=2,
            grid=(n_blocks, nf),
            in_specs=[pl.BlockSpec((MOE_ROWS, d), blk),
                      pl.BlockSpec((1, d, tf), lambda j, f, be, nu: (be[j], 0, fcol(j, f, be, nu))),
                      pl.BlockSpec((1, d, tf), lambda j, f, be, nu: (be[j], 0, fcol(j, f, be, nu))),
                      pl.BlockSpec((1, tf, d), lambda j, f, be, nu: (be[j], fcol(j, f, be, nu), 0))],
            out_specs=pl.BlockSpec((MOE_ROWS, d), lambda j, f, be, nu: (j, 0))),
        out_shape=jax.ShapeDtypeStruct((n_rows, d), F32),
        compiler_params=_cparams(("arbitrary", "arbitrary"), 56),
        name="moe_experts",
    )(block_e, n_used, xb, wg, wu, wd)


def _combine_kernel(d1_ref, d2_ref, x_ref, gt_ref, p_ref, fg_ref, yb_ref, o_ref, y1_ref, y2_ref, sem, *,
                    tm, final, row_off):
    base = row_off + (pl.program_id(0) * pl.num_programs(1) + pl.program_id(1)) * tm

    def row_copy(src, buf_ref, j):
        return pltpu.make_async_copy(yb_ref.at[pl.ds(src, 1), :], buf_ref.at[pl.ds(j, 1), :], sem)

    def issue(j8, carry):
        for u in range(min(ROW_DMA_UNROLL, tm)):
            j = j8 * ROW_DMA_UNROLL + u
            row_copy(d1_ref[base + j], y1_ref, j).start(priority=0)
            row_copy(d2_ref[base + j], y2_ref, j).start(priority=1)
        return carry

    lax.fori_loop(0, tm // ROW_DMA_UNROLL, issue, 0)
    for buf_ref in (y1_ref, y2_ref):
        pltpu.make_async_copy(yb_ref.at[pl.ds(0, tm), :], buf_ref, sem).wait()

    p = p_ref[...]
    f = y1_ref[...] * p[:, 0:1] + y2_ref[...] * p[:, 1:2]
    x2 = x_ref[0] + _gated(f, gt_ref)
    o_ref[0] = _final_norm(x2, fg_ref) if final else x2


def _combine(d1, d2, x, mod, probs, fg, yb, final, tm, row_off):
    b, t, d = x.shape
    nt = t // tm
    blk_off = row_off // tm
    return pl.pallas_call(
        functools.partial(_combine_kernel, tm=tm, final=final, row_off=row_off),
        grid_spec=pltpu.PrefetchScalarGridSpec(
            num_scalar_prefetch=2,
            grid=(b, nt),
            in_specs=[pl.BlockSpec((1, tm, d), lambda i, j, *_: (i, j, 0)),
                      pl.BlockSpec((1, mod.shape[1], d), lambda i, j, *_: (i, 0, 5)),
                      pl.BlockSpec((tm, LANES), lambda i, j, *_: (blk_off + i * nt + j, 0)),
                      pl.BlockSpec((1, d), lambda i, j, *_: (0, 0)),
                      pl.BlockSpec(memory_space=pl.ANY)],
            out_specs=pl.BlockSpec((1, tm, d), lambda i, j, *_: (i, j, 0)),
            scratch_shapes=[pltpu.VMEM((tm, d), F32), pltpu.VMEM((tm, d), F32), pltpu.SemaphoreType.DMA]),
        out_shape=jax.ShapeDtypeStruct((b, t, d), F32),
        compiler_params=_cparams(("arbitrary", "arbitrary"), 24),
        name="moe_combine",
    )(d1, d2, x, mod, probs, fg, yb)


def _moe(h2, logits, trunks, wg, wu, wd, fg, final):
    n, d = h2.shape
    tile = lambda cap: max(r for r in (cap, cap // 2, cap // 4, cap // 8, cap // 16) if n % r == 0)
    info, probs, cnt = _route(logits, tile(256))
    counts = cnt[0, :N_EXPERTS]
    padded = (counts + MOE_ROWS - 1) // MOE_ROWS * MOE_ROWS
    ep = jnp.cumsum(padded).astype(jnp.int32)
    sp = ep - padded
    n_blocks = -(-(2 * n) // MOE_ROWS) + N_EXPERTS
    n_used = jnp.maximum(ep[-1] // MOE_ROWS, 1).astype(jnp.int32)
    blk_start = jnp.minimum(jnp.arange(n_blocks, dtype=jnp.int32), n_used - 1) * MOE_ROWS
    block_e = jnp.minimum(jnp.sum(blk_start[:, None] >= ep[None, :], axis=1), N_EXPERTS - 1).astype(jnp.int32)
    dest = _dest(info, jnp.pad(sp, (0, LANES - N_EXPERTS))[None, :], tile(1024))
    d1, d2 = dest[:, 0], dest[:, 1]
    xb = _dispatch(d1, d2, sp, ep, h2, n_blocks * MOE_ROWS, tile(256))
    yb = _experts(block_e, n_used.reshape(1), xb, wg, wu, wd, MOE_FF_TILE)
    return [_combine(d1, d2, x1, mod, probs, fg, yb, final, tm, off) for x1, mod, tm, off in trunks]


def _pad_rows_front(a, rows):
    pad = [(0, 0)] * a.ndim
    pad[-2] = (rows - a.shape[-2], 0)
    return jnp.pad(a, pad)


def _trunks(xs, mods, states, w):
    depth = w["wa"].shape[0]
    n_tr = len(xs)
    shapes = [x.shape for x in xs]
    d = shapes[0][2]
    offs = [sum(s[0] * s[1] for s in shapes[:i]) for i in range(n_tr)]
    n_all = sum(s[0] * s[1] for s in shapes)
    tms, mvs = [], []
    for (b, t, _), m in zip(shapes, mods):
        if t < TOKEN_TILE and TOKEN_TILE % t == 0 and (b * t) % TOKEN_TILE == 0:
            nseg = TOKEN_TILE // t
            tms.append(TOKEN_TILE)
            mvs.append(m.reshape(depth, b // nseg, nseg, m.shape[-1]))
        else:
            tms.append(min(t, TOKEN_TILE))
            mvs.append(m)
    tok = lambda a, i: a.reshape(-1, tms[i] if shapes[i][1] < tms[i] else shapes[i][1], a.shape[-1])
    seq = lambda a, i: a.reshape(shapes[i][0], shapes[i][1], a.shape[-1])
    xs = [tok(x, i) for i, x in enumerate(xs)]
    tcs = [min(s[1], 256) for s in shapes]
    new_states = [([], [], [], []) for _ in xs]
    for l in range(depth):
        final = l == depth - 1
        j = l // 2
        x1s, flat = [], None
        for i, x in enumerate(xs):
            t = shapes[i][1]
            ml = mvs[i][l]
            conv0, gdn0, shift0, rwkv0 = states[i]
            pa, pab, pb = (seq(a, i) for a in _norm_proj(x, ml, w["norm1_g"][l], w["wa"][l], w["wab"][l],
                                                         w["wb"][l], tms[i]))
            oa, sg = _gdn(pa, pab, _pad_rows_front(conv0[l], SUBLANES), gdn0[l], w["conv_w"][l], w["alog"][l],
                          w["dtb"][l], w["onorm_g"][l], tcs[i])
            ob, sr = _rwkv(pb, _pad_rows_front(shift0[l], SUBLANES), rwkv0[l], [v[l] for v in w["rwkv_vecs"]],
                           w["w2p"][l], w["a2p"][l], w["g2"][l], w["bd"], tcs[i])
            for lst, val in zip(new_states[i], (pa[:, t - (GDN_CONV - 1):, :w["conv_w"].shape[2]], sg,
                                                pb[:, t - 1:, :], sr)):
                lst.append(val)
            oa, ob = tok(oa, i), tok(ob, i)
            if l % 2 == 0:
                x1, h2 = _out_proj(oa, ob, x, ml, w["norm2_g"][l], w["woa"][l], w["wob"][l], tms[i])
                xs[i] = _ffn(h2, x1, ml, w["ffn_g"][j], w["ffn_u"][j], w["ffn_d"][j], w["final_g"], final, tms[i])
            else:
                x1, *flat = _out_proj(oa, ob, x, ml, w["norm2_g"][l], w["woa"][l], w["wob"][l], tms[i],
                                      router=w["router"][j], flat_rows=n_all, row_off=offs[i], flat_prev=flat)
                x1s.append(x1)
        if l % 2 == 1:
            trunks = [(x1s[i], mvs[i][l], tms[i], offs[i]) for i in range(n_tr)]
            xs = _moe(flat[0], flat[1], trunks, w["moe_g"][j], w["moe_u"][j], w["moe_d"][j], w["final_g"], final)
    return [seq(x, i) for i, x in enumerate(xs)], [tuple(jnp.stack(s) for s in st) for st in new_states]


def kernel(x_prompt, x_sample, c_prompt, c_sample, state_gdn_conv, state_gdn, state_rwkv_shift, state_rwkv, w_ada, b_ada, norm1_g, norm2_g, w_in, gdn_conv_w, gdn_a_log, gdn_dt_bias, gdn_onorm_g, rwkv_mu, rwkv_w0, rwkv_w2, rwkv_a0, rwkv_a2, rwkv_g2, rwkv_k_k, rwkv_k_a, rwkv_r_k, rwkv_ln_w, rwkv_ln_b, w_out, ffn_w_gate, ffn_w_up, ffn_w_down, moe_router, moe_w_gate, moe_w_up, moe_w_down, final_g):
    depth, d, _ = w_in.shape
    nbp = x_prompt.shape[0]
    nbs = x_sample.shape[0]
    nqkvz = 4 * GDN_HEADS * GDN_D
    nab = 2 * GDN_HEADS
    nw = RWKV_HEADS * RWKV_N
    lora_w = rwkv_w2.shape[1]

    def lane_pad(v):
        return jnp.pad(v, ((0, 0), (0, LANES - v.shape[1])))[:, None, :]

    rows = lambda v: v[:, None, :]
    hi = jnp.arange(nw) // RWKV_N
    w = dict(
        norm1_g=rows(norm1_g), norm2_g=rows(norm2_g), final_g=final_g[None, :],
        wa=w_in[:, :, :nqkvz].astype(BF16),
        wab=jnp.pad(w_in[:, :, nqkvz:nqkvz + nab], ((0, 0), (0, 0), (0, LANES - nab))).astype(BF16),
        wb=w_in[:, :, nqkvz + nab:].astype(BF16),
        conv_w=gdn_conv_w, alog=lane_pad(gdn_a_log), dtb=lane_pad(gdn_dt_bias), onorm_g=rows(gdn_onorm_g),
        rwkv_vecs=[rows(rwkv_mu), rows(rwkv_w0), rows(rwkv_a0), rows(rwkv_k_k), rows(rwkv_k_a),
                   rwkv_r_k.reshape(depth, 1, nw), rows(rwkv_ln_w), rows(rwkv_ln_b)],
        w2p=jnp.pad(rwkv_w2, ((0, 0), (0, LANES - lora_w), (0, 0))),
        a2p=jnp.pad(rwkv_a2, ((0, 0), (lora_w, LANES - lora_w - rwkv_a2.shape[1]), (0, 0))),
        g2=rwkv_g2,
        bd=(hi[:, None] == hi[None, :]).astype(F32),
        woa=w_out[:, :GDN_HEADS * GDN_D, :].astype(BF16), wob=w_out[:, GDN_HEADS * GDN_D:, :].astype(BF16),
        ffn_g=ffn_w_gate.astype(BF16), ffn_u=ffn_w_up.astype(BF16), ffn_d=ffn_w_down.astype(BF16),
        router=jnp.pad(moe_router, ((0, 0), (0, 0), (0, LANES - moe_router.shape[2]))),
        moe_g=moe_w_gate.astype(BF16), moe_u=moe_w_up.astype(BF16), moe_d=moe_w_down.astype(BF16),
    )

    nb_all = nbp + nbs
    bp = -(-nb_all // SUBLANES) * SUBLANES
    c_all = jnp.pad(jnp.concatenate([c_prompt, c_sample], axis=0), ((0, bp - nb_all), (0, 0)))
    mod = _ada_mod(c_all, w_ada, b_ada)[:, :, None, :]
    mod_p, mod_s = mod[:, :nbp], mod[:, nbp:nb_all]

    dt = x_prompt.dtype
    zc = jnp.zeros((depth, nbp) + state_gdn_conv.shape[2:], dt)
    zg = jnp.zeros((depth, nbp) + state_gdn.shape[2:], dt)
    zs = jnp.zeros((depth, nbp) + state_rwkv_shift.shape[2:], dt)
    zr = jnp.zeros((depth, nbp) + state_rwkv.shape[2:], dt)
    (y_p, y_s), (st_p, st_s) = _trunks(
        [x_prompt, x_sample], [mod_p, mod_s],
        [(zc, zg, zs, zr), (state_gdn_conv, state_gdn, state_rwkv_shift, state_rwkv)], w)
    return (y_p, y_s) + st_p + st_s
```

```python
import functools

import jax
import jax.numpy as jnp
from jax import lax
from jax.experimental import pallas as pl
from jax.experimental.pallas import tpu as pltpu

F32 = jnp.float32
BF16 = jnp.bfloat16
HI = lax.Precision.HIGHEST

LANES = 128
SUBLANES = 8
CHUNK = 64
GDN_HEADS = 4
GDN_D = 128
RWKV_HEADS = 8
RWKV_N = 64
GDN_CONV = 4
N_EXPERTS = 8
NORM_EPS = 1e-6
L2_EPS = 1e-6
GN_EPS = 64e-5
NEG_BIG = -1e30
TOKEN_TILE = 512
GDN_TILE = 512
RWKV_TILE = 256
MOE_ROWS = 512
MOE_FF_TILE = 1792
MIB = 1024 * 1024

P_GDN_SCORE = 1
P_GDN_INV = 3
P_GDN_SOLVE = 1
P_GDN_STATE = 1
P_RWKV_SMALL = 1
P_RWKV_SCORE = 1
P_RWKV_INV = 1
P_RWKV_MID = 1
P_RWKV_STATE = 1
P_ROUTER = 3


def _cparams(sem, vmem_mib):
    return pltpu.CompilerParams(dimension_semantics=sem, vmem_limit_bytes=vmem_mib * MIB)


def _split_bf16(x):
    hi = x.astype(BF16)
    return hi, (x - hi.astype(F32)).astype(BF16)


def _split3_bf16(x):
    hi = x.astype(BF16)
    r1 = x - hi.astype(F32)
    mid = r1.astype(BF16)
    return hi, mid, (r1 - mid.astype(F32)).astype(BF16)


_CONTRACT = {"nn": (1, 0), "nt": (1, 1), "tn": (0, 0)}


def _dg(a, b, kind, prec):
    off = a.ndim - 2
    ca, cb = _CONTRACT[kind]
    dn = (((ca + off,), (cb + off,)), (((0,), (0,)) if off else ((), ())))
    dot = lambda x, y: lax.dot_general(x, y, dn, preferred_element_type=F32)
    if prec is None or prec is HI:
        return lax.dot_general(a, b, dn, preferred_element_type=F32, precision=prec)
    if prec == 1:
        return dot(a.astype(BF16), b.astype(BF16))
    if prec == "l3":
        bb = b.astype(BF16)
        a0, a1, a2 = _split3_bf16(a)
        return dot(a0, bb) + dot(a1, bb) + dot(a2, bb)
    if prec == "r3":
        ab = a.astype(BF16)
        b0, b1, b2 = _split3_bf16(b)
        return dot(ab, b0) + dot(ab, b1) + dot(ab, b2)
    ah, al = _split_bf16(a)
    bh, bl = _split_bf16(b)
    return dot(ah, bh) + dot(ah, bl) + dot(al, bh)


def _mm(a, b, prec=None):
    return _dg(a, b, "nn", prec)


def _mm_nt(a, b, prec=None):
    return _dg(a, b, "nt", prec)


def _mm_tn(a, b, prec=None):
    return _dg(a, b, "tn", prec)


def _silu(x):
    return x * jax.nn.sigmoid(x)


def _softplus(x):
    return jnp.maximum(x, 0.0) + jnp.log1p(jnp.exp(-jnp.abs(x)))


def _iota2(shape, dim):
    return lax.broadcasted_iota(jnp.int32, shape, dim)


def _per_segment(x, fn, *mod_refs):
    mods = [r[0] for r in mod_refs]
    nseg = mods[0].shape[0]
    if nseg == 1:
        return fn(x, *mods)
    rows = x.shape[0] // nseg
    return jnp.concatenate([fn(x[s * rows:(s + 1) * rows], *(m[s:s + 1] for m in mods)) for s in range(nseg)], axis=0)


def _scale_shift(xn, sc_ref, sh_ref):
    return _per_segment(xn, lambda x, sc, sh: x * (1.0 + sc) + sh, sc_ref, sh_ref)


def _gated(f, gt_ref):
    return _per_segment(f, lambda x, gt: gt * x, gt_ref)


def _neumann_inverse(x, eye, prec):
    c = x.shape[-1]
    z = jnp.concatenate([x, jnp.broadcast_to(eye, x.shape)], axis=-1)
    keep_s = _iota2((c, 2 * c), 1) >= c
    for _ in range(6):
        z = _mm(z[..., :c], z, prec) + jnp.where(keep_s, z, 0.0)
    return z[..., c:]


def _ada_kernel(c_ref, w_ref, b_ref, o_ref):
    o_ref[0] = _mm(_silu(c_ref[...]), w_ref[0], HI) + b_ref[0]


def _ada_mod(c_all, w_ada, b_ada):
    depth, d, n6 = w_ada.shape
    bp = c_all.shape[0]
    tn = 1536
    return pl.pallas_call(
        _ada_kernel,
        grid=(depth, n6 // tn),
        in_specs=[pl.BlockSpec((bp, d), lambda l, j: (0, 0)),
                  pl.BlockSpec((1, d, tn), lambda l, j: (l, 0, j)),
                  pl.BlockSpec((1, 1, tn), lambda l, j: (l, 0, j))],
        out_specs=pl.BlockSpec((1, bp, tn), lambda l, j: (l, 0, j)),
        out_shape=jax.ShapeDtypeStruct((depth, bp, n6), F32),
        compiler_params=_cparams(("arbitrary", "arbitrary"), 40),
        name="ada_mod",
    )(c_all, w_ada, b_ada.reshape(depth, 1, n6))


def _norm_proj_kernel(x_ref, sh_ref, sc_ref, g_ref, wa_ref, wab_ref, wb_ref, pa_ref, pab_ref, pb_ref):
    x = x_ref[0]
    rs = lax.rsqrt(jnp.mean(x * x, axis=-1, keepdims=True) + NORM_EPS)
    hb = _scale_shift(x * rs * g_ref[...], sc_ref, sh_ref).astype(BF16)
    pa_ref[0] = _mm(hb, wa_ref[...])
    pab_ref[0] = _mm(hb, wab_ref[...])
    pb_ref[0] = _mm(hb, wb_ref[...])


def _norm_proj(x, mod, g, wa, wab, wb, tm):
    b, t, d = x.shape
    na, nab, nb = wa.shape[1], wab.shape[1], wb.shape[1]
    nseg = mod.shape[1]
    row = lambda i, j: (i, j, 0)
    const = lambda i, j: (0, 0)
    return pl.pallas_call(
        _norm_proj_kernel,
        grid=(b, t // tm),
        in_specs=[pl.BlockSpec((1, tm, d), row),
                  pl.BlockSpec((1, nseg, d), lambda i, j: (i, 0, 0)),
                  pl.BlockSpec((1, nseg, d), lambda i, j: (i, 0, 1)),
                  pl.BlockSpec((1, d), const),
                  pl.BlockSpec((d, na), const),
                  pl.BlockSpec((d, nab), const),
                  pl.BlockSpec((d, nb), const)],
        out_specs=[pl.BlockSpec((1, tm, na), row), pl.BlockSpec((1, tm, nab), row), pl.BlockSpec((1, tm, nb), row)],
        out_shape=[jax.ShapeDtypeStruct((b, t, na), F32), jax.ShapeDtypeStruct((b, t, nab), F32),
                   jax.ShapeDtypeStruct((b, t, nb), F32)],
        compiler_params=_cparams(("arbitrary", "arbitrary"), 48),
        name="norm_proj",
    )(x, mod, mod, g, wa, wab, wb)


def _gdn_kernel(pa_ref, pab_ref, cbuf_ref, s0_ref, cw_ref, alog_ref, dtb_ref, og_ref,
                o_ref, sfin_ref, xc_ref, s_ref, *, tc):
    t = pl.program_id(1)
    nqk = GDN_HEADS * GDN_D
    nconv = 3 * nqk

    @pl.when(t == 0)
    def _():
        xc_ref[0:SUBLANES, :] = cbuf_ref[0]
        s_ref[...] = s0_ref[0]

    xc_ref[SUBLANES:SUBLANES + tc, :] = pa_ref[0, :, 0:nconv]
    cw = cw_ref[...]
    y = xc_ref[5:5 + tc, :] * cw[0:1, :]
    for i in range(1, GDN_CONV):
        y = y + xc_ref[5 + i:5 + i + tc, :] * cw[i:i + 1, :]
    xc_ref[0:SUBLANES, :] = xc_ref[tc:tc + SUBLANES, :]
    qkv = _silu(y)

    ab = pab_ref[0]
    gmat = -jnp.exp(alog_ref[...]) * _softplus(ab + dtb_ref[...])
    bmat = jax.nn.sigmoid(ab)

    c = CHUNK
    nch = tc // c
    nh = GDN_HEADS
    nb = nch * nh
    ri = _iota2((c, c), 0)
    ci = _iota2((c, c), 1)
    incl = ri >= ci
    strict = ri > ci
    eye_c = (ri == ci).astype(F32)

    def heads(x, width):
        x = x.reshape(nch, c, nh * width)
        return jnp.stack([x[:, :, h * width:(h + 1) * width] for h in range(nh)], axis=1).reshape(nb, c, width)

    q = heads(qkv[:, 0:nqk], GDN_D)
    k = heads(qkv[:, nqk:2 * nqk], GDN_D)
    v = heads(qkv[:, 2 * nqk:3 * nqk], GDN_D)
    q = q * lax.rsqrt(jnp.sum(q * q, axis=-1, keepdims=True) + L2_EPS) * (GDN_D ** -0.5)
    k = k * lax.rsqrt(jnp.sum(k * k, axis=-1, keepdims=True) + L2_EPS)
    beta = heads(bmat[:, nh:2 * nh], 1)

    g4 = gmat.reshape(nch, c, LANES)
    gcol4 = _mm(jnp.broadcast_to(incl.astype(F32), (nch, c, c)), g4, "r3")
    grow4 = _mm_tn(g4, jnp.broadcast_to((ri <= ci).astype(F32), (nch, c, c)), "l3")
    gc = jnp.stack([gcol4[:, :, h:h + 1] for h in range(nh)], axis=1).reshape(nb, c, 1)
    gr = jnp.stack([grow4[:, h:h + 1, :] for h in range(nh)], axis=1).reshape(nb, 1, c)
    decay = jnp.exp(jnp.where(incl, gc - gr, NEG_BIG))

    kb = k * beta
    m = jnp.where(strict, _mm_nt(kb, k, P_GDN_SCORE) * decay, 0.0)
    t0 = _neumann_inverse(-m, eye_c, 1)
    tinv = t0 + _mm(t0, (eye_c - t0) - _mm(m, t0, P_GDN_INV), 1)
    egc = jnp.exp(gc)
    sol = _mm(tinv, jnp.concatenate([v * beta, kb * egc], axis=2), P_GDN_SOLVE)
    u, w = sol[:, :, :GDN_D], sol[:, :, GDN_D:]
    aqk = _mm_nt(q, k, P_GDN_SCORE) * decay
    gl = gc[:, c - 1:c, :]
    kd = k * jnp.exp(gl - gc)
    qg = q * egc
    egl = jnp.exp(gl)

    s = s_ref[...]
    outs = []
    for ch in range(nch):
        bs = slice(ch * nh, (ch + 1) * nh)
        v_new = u[bs] - _mm(w[bs], s, P_GDN_STATE)
        outs.append(_mm(qg[bs], s, P_GDN_STATE) + _mm(aqk[bs], v_new, P_GDN_STATE))
        s = s * egl[bs] + _mm_tn(kd[bs], v_new, P_GDN_STATE)
    s_ref[...] = s

    o = jnp.stack(outs, axis=0)
    o = o * lax.rsqrt(jnp.mean(o * o, axis=-1, keepdims=True) + NORM_EPS) * og_ref[...]
    for h in range(nh):
        sl = slice(h * GDN_D, (h + 1) * GDN_D)
        z = pa_ref[0, :, nconv + h * GDN_D: nconv + (h + 1) * GDN_D]
        o_ref[0, :, sl] = (o[:, h].reshape(tc, GDN_D) * _silu(z)).astype(o_ref.dtype)

    @pl.when(t == pl.num_programs(1) - 1)
    def _():
        sfin_ref[0] = s_ref[...]


def _gdn(pa, pab, cbuf8, s0, cw, alog, dtb, og, tc):
    b, t, na = pa.shape
    nconv = cw.shape[1]
    row = lambda i, j: (i, j, 0)
    const2 = lambda i, j: (0, 0)
    return pl.pallas_call(
        functools.partial(_gdn_kernel, tc=tc),
        grid=(b, t // tc),
        in_specs=[pl.BlockSpec((1, tc, na), row),
                  pl.BlockSpec((1, tc, LANES), row),
                  pl.BlockSpec((1, SUBLANES, nconv), lambda i, j: (i, 0, 0)),
                  pl.BlockSpec((1, GDN_HEADS, GDN_D, GDN_D), lambda i, j: (i, 0, 0, 0)),
                  pl.BlockSpec((GDN_CONV, nconv), const2),
                  pl.BlockSpec((1, LANES), const2),
                  pl.BlockSpec((1, LANES), const2),
                  pl.BlockSpec((1, GDN_D), const2)],
        out_specs=[pl.BlockSpec((1, tc, GDN_HEADS * GDN_D), row),
                   pl.BlockSpec((1, GDN_HEADS, GDN_D, GDN_D), lambda i, j: (i, 0, 0, 0))],
        out_shape=[jax.ShapeDtypeStruct((b, t, GDN_HEADS * GDN_D), BF16),
                   jax.ShapeDtypeStruct((b, GDN_HEADS, GDN_D, GDN_D), F32)],
        scratch_shapes=[pltpu.VMEM((tc + SUBLANES, nconv), F32),
                        pltpu.VMEM((GDN_HEADS, GDN_D, GDN_D), F32)],
        compiler_params=_cparams(("arbitrary", "arbitrary"), 40),
        name="gdn_mixer",
    )(pa, pab, cbuf8, s0, cw, alog, dtb, og)


def _rwkv_kernel(pb_ref, sbuf_ref, s0_ref, mu_ref, w0_ref, w2_ref, a0_ref, a2_ref, g2_ref, kk_ref, ka_ref,
                 rk_ref, lnw_ref, lnb_ref, bd_ref, o_ref, sfin_ref, xc_ref, s_ref, y_ref, *, tc):
    t = pl.program_id(1)
    nw = RWKV_HEADS * RWKV_N

    @pl.when(t == 0)
    def _():
        xc_ref[0:SUBLANES, :] = sbuf_ref[0]
        s_ref[...] = s0_ref[0]

    x = pb_ref[0]
    xc_ref[SUBLANES:SUBLANES + tc, :] = x
    prev = xc_ref[SUBLANES - 1:SUBLANES - 1 + tc, :]
    xc_ref[0:SUBLANES, :] = xc_ref[tc:tc + SUBLANES, :]
    xs = x + (prev - x) * mu_ref[...]
    r = xs[:, 0:nw]
    kr = xs[:, nw:2 * nw]
    vr = xs[:, 2 * nw:3 * nw]
    wa = xs[:, 3 * nw:3 * nw + LANES]
    gd = xs[:, 3 * nw + LANES:3 * nw + 2 * LANES]

    bd = bd_ref[...]
    ps = P_RWKV_SMALL
    wlog = -_softplus(-(w0_ref[...] + _mm(jnp.tanh(wa), w2_ref[...], ps))) - 0.5
    lw = -jnp.exp(wlog)
    a = jax.nn.sigmoid(a0_ref[...] + _mm(wa, a2_ref[...], ps))
    gate = _mm(jax.nn.sigmoid(gd), g2_ref[...], ps)
    kkr = kr * kk_ref[...]
    kk = kkr * lax.rsqrt(_mm(kkr * kkr, bd, ps) + L2_EPS)
    kb = kr * (1.0 + (a - 1.0) * ka_ref[...])
    aa = -kk
    bb = kk * a
    bonus = _mm(r * kb * rk_ref[...], bd, ps) * vr

    c = CHUNK
    n = RWKV_N
    nch = tc // c
    nh = RWKV_HEADS
    nb = nch * nh
    ri = _iota2((c, c), 0)
    ci = _iota2((c, c), 1)
    incl = ri >= ci
    strict = ri > ci
    eye = ri == ci
    eye_c = eye.astype(F32)

    def heads(x):
        return jnp.stack([x[:, :, h * n:(h + 1) * n] for h in range(nh)], axis=1).reshape(nb, x.shape[1], n)

    chunks = lambda x: x.reshape(nch, c, nw)
    lw4 = chunks(lw)
    g = _mm(jnp.broadcast_to(incl.astype(F32), (nch, c, c)), lw4, "r3")
    gfin = g[:, c - 1:c, :]
    eng = jnp.exp(-g)
    tail = jnp.exp(gfin - g)
    aa4, bb4, kb4 = chunks(aa), chunks(bb), chunks(kb)
    at = heads(aa4 * jnp.exp(g - lw4))
    bt = heads(bb4 * eng)
    kt = heads(kb4 * eng)
    rt = heads(chunks(r) * jnp.exp(g))
    bh = heads(bb4 * tail)
    kh = heads(kb4 * tail)
    egf = heads(jnp.exp(gfin))
    v4 = chunks(vr)
    upper = _iota2((c, 2 * n), 1) >= n
    vcols = [v4[:, :, (h // 2) * 2 * n:(h // 2 + 1) * 2 * n] for h in range(nh)]
    vhi = jnp.stack([jnp.where(upper, col if h % 2 else pltpu.roll(col, n, 2), 0.0)
                     for h, col in enumerate(vcols)], axis=1).reshape(nb, c, 2 * n)
    zeros_lo = jnp.zeros((nb, c, n), F32)
    pad_hi = lambda x: jnp.concatenate([x, zeros_lo], axis=-1)

    pm = P_RWKV_MID
    colj = _iota2((c, 2 * c), 1) % c
    rowi = _iota2((c, 2 * c), 0)
    ar = jnp.concatenate([at, rt], axis=1)
    xbk = _mm_nt(ar, jnp.concatenate([bt, kt], axis=1), P_RWKV_SCORE)
    l2 = jnp.where(rowi > colj, xbk[:, :c], 0.0)
    m2 = jnp.where(rowi >= colj, xbk[:, c:], 0.0)
    tinv = _neumann_inverse(l2[..., :c], eye_c, P_RWKV_INV)
    lv = _mm(l2, jnp.concatenate([jnp.zeros_like(vhi), vhi], axis=1), pm)
    wu = _mm(tinv, lv + pad_hi(at), pm)
    wv = jnp.concatenate([wu, vhi], axis=1)
    qy = _mm(m2, wv, pm) + pad_hi(rt)
    pi = _mm_tn(wv, jnp.concatenate([bh, kh], axis=1), pm)
    pt = jnp.where(eye, egf, 0.0) + pi[:, :n]
    inc = pi[:, n:]

    s = s_ref[...]
    zeros_s = jnp.zeros_like(s)
    ys = []
    for ch in range(nch):
        bs = slice(ch * nh, (ch + 1) * nh)
        ys.append(_mm_nt(qy[bs][..., :n], jnp.concatenate([zeros_s, s], axis=1), P_RWKV_STATE) + qy[bs])
        s = _mm(s, pt[bs], P_RWKV_STATE) + inc[bs]
    s_ref[...] = s
    y4 = jnp.stack(ys, axis=0)
    for h in range(nh):
        y_ref[:, h * n:(h + 1) * n] = y4[:, h, :, n:].reshape(tc, n)

    y = y_ref[...]
    mean = _mm(y, bd, ps) * (1.0 / n)
    dlt = y - mean
    var = _mm(dlt * dlt, bd, ps) * (1.0 / n)
    yn = dlt * lax.rsqrt(var + GN_EPS) * lnw_ref[...] + lnb_ref[...]
    o_ref[0] = ((yn + bonus) * gate).astype(o_ref.dtype)

    @pl.when(t == pl.num_programs(1) - 1)
    def _():
        sfin_ref[0] = s_ref[...]


def _rwkv(pb, sbuf8, s0, vecs, w2p, a2p, g2, bd, tc):
    b, t, nb = pb.shape
    nw = RWKV_HEADS * RWKV_N
    row = lambda i, j: (i, j, 0)
    const2 = lambda i, j: (0, 0)
    mu, w0, a0, kk, ka, rk, lnw, lnb = vecs
    vec_spec = pl.BlockSpec((1, nw), const2)
    return pl.pallas_call(
        functools.partial(_rwkv_kernel, tc=tc),
        grid=(b, t // tc),
        in_specs=[pl.BlockSpec((1, tc, nb), row),
                  pl.BlockSpec((1, SUBLANES, nb), lambda i, j: (i, 0, 0)),
                  pl.BlockSpec((1, RWKV_HEADS, RWKV_N, RWKV_N), lambda i, j: (i, 0, 0, 0)),
                  pl.BlockSpec((1, nb), const2),
                  vec_spec,
                  pl.BlockSpec((LANES, nw), const2),
                  vec_spec,
                  pl.BlockSpec((LANES, nw), const2),
                  pl.BlockSpec((LANES, nw), const2),
                  vec_spec, vec_spec, vec_spec, vec_spec, vec_spec,
                  pl.BlockSpec((nw, nw), const2)],
        out_specs=[pl.BlockSpec((1, tc, nw), row),
                   pl.BlockSpec((1, RWKV_HEADS, RWKV_N, RWKV_N), lambda i, j: (i, 0, 0, 0))],
        out_shape=[jax.ShapeDtypeStruct((b, t, nw), BF16),
                   jax.ShapeDtypeStruct((b, RWKV_HEADS, RWKV_N, RWKV_N), F32)],
        scratch_shapes=[pltpu.VMEM((tc + SUBLANES, nb), F32),
                        pltpu.VMEM((RWKV_HEADS, RWKV_N, RWKV_N), F32),
                        pltpu.VMEM((tc, nw), F32)],
        compiler_params=_cparams(("arbitrary", "arbitrary"), 40),
        name="rwkv_mixer",
    )(pb, sbuf8, s0, mu, w0, w2p, a0, a2p, g2, kk, ka, rk, lnw, lnb, bd)


def _out_proj_kernel(oa_ref, ob_ref, x_ref, gt_ref, sh_ref, sc_ref, g_ref, woa_ref, wob_ref, *rest, with_router, nt):
    if with_router:
        wr_ref = rest[0]
        x1_ref, h_ref, lg_ref = rest[-3:]
    else:
        x1_ref, h_ref = rest

    @pl.when(pl.program_id(1) < nt)
    def _():
        mix = _mm(oa_ref[0], woa_ref[...]) + _mm(ob_ref[0], wob_ref[...])
        x1 = x_ref[0] + _gated(mix, gt_ref)
        x1_ref[0] = x1
        rs = lax.rsqrt(jnp.mean(x1 * x1, axis=-1, keepdims=True) + NORM_EPS)
        h = _scale_shift(x1 * rs * g_ref[...], sc_ref, sh_ref)
        if with_router:
            h_ref[...] = h
            lg_ref[...] = _mm(h, wr_ref[...], P_ROUTER)
        else:
            h_ref[0] = h.astype(h_ref.dtype)

    if with_router:
        @pl.when(pl.program_id(1) >= nt)
        def _():
            h_ref[...] = jnp.zeros_like(h_ref)
            lg_ref[...] = jnp.zeros_like(lg_ref)


def _out_proj(oa, ob, x, mod, g, woa, wob, tm, router=None, flat_rows=None, row_off=0, flat_prev=None):
    b, t, d = x.shape
    nh = oa.shape[2]
    nt = t // tm
    nseg = mod.shape[1]
    tail = 0
    if router is not None and flat_prev is None:
        assert b == 1 and row_off == 0 and (flat_rows - t) % tm == 0
        tail = (flat_rows - t) // tm
    row = lambda i, j: (i, jnp.minimum(j, nt - 1), 0)
    const = lambda i, j: (0, 0)
    in_specs = [pl.BlockSpec((1, tm, nh), row), pl.BlockSpec((1, tm, nh), row), pl.BlockSpec((1, tm, d), row),
                pl.BlockSpec((1, nseg, d), lambda i, j: (i, 0, 2)),
                pl.BlockSpec((1, nseg, d), lambda i, j: (i, 0, 3)),
                pl.BlockSpec((1, nseg, d), lambda i, j: (i, 0, 4)),
                pl.BlockSpec((1, d), const), pl.BlockSpec((nh, d), const), pl.BlockSpec((nh, d), const)]
    args = [oa, ob, x, mod, mod, mod, g, woa, wob]
    aliases = {}
    if router is None:
        out_specs = [pl.BlockSpec((1, tm, d), row), pl.BlockSpec((1, tm, d), row)]
        out_shape = [jax.ShapeDtypeStruct((b, t, d), F32), jax.ShapeDtypeStruct((b, t, d), BF16)]
    else:
        blk_off = row_off // tm
        flat = lambda i, j: (blk_off + i * nt + j, 0)
        in_specs.append(pl.BlockSpec((d, LANES), const))
        args.append(router)
        if flat_prev is not None:
            aliases = {len(args): 1, len(args) + 1: 2}
            in_specs += [pl.BlockSpec(memory_space=pl.ANY)] * 2
            args += list(flat_prev)
        out_specs = [pl.BlockSpec((1, tm, d), row), pl.BlockSpec((tm, d), flat), pl.BlockSpec((tm, LANES), flat)]
        out_shape = [jax.ShapeDtypeStruct((b, t, d), F32), jax.ShapeDtypeStruct((flat_rows, d), F32),
                     jax.ShapeDtypeStruct((flat_rows, LANES), F32)]
    return pl.pallas_call(
        functools.partial(_out_proj_kernel, with_router=router is not None, nt=nt),
        grid=(b, nt + tail),
        in_specs=in_specs, out_specs=out_specs, out_shape=out_shape, input_output_aliases=aliases,
        compiler_params=_cparams(("arbitrary", "arbitrary"), 40),
        name="out_proj" if router is None else "out_proj_router",
    )(*args)


def _final_norm(x2, fg_ref):
    rs = lax.rsqrt(jnp.mean(x2 * x2, axis=-1, keepdims=True) + NORM_EPS)
    return x2 * rs * fg_ref[...]


def _ffn_kernel(h_ref, x_ref, gt_ref, wg_ref, wu_ref, wd_ref, fg_ref, o_ref, *, nf, final):
    h = h_ref[0]
    ff = wg_ref.shape[1]
    tf = ff // nf
    acc = None
    for f in range(nf):
        cols = slice(f * tf, (f + 1) * tf)
        act = (_silu(_mm(h, wg_ref[:, cols])) * _mm(h, wu_ref[:, cols])).astype(BF16)
        part = _mm(act, wd_ref[cols, :])
        acc = part if acc is None else acc + part
    x2 = x_ref[0] + _gated(acc, gt_ref)
    o_ref[0] = _final_norm(x2, fg_ref) if final else x2


def _ffn(h, x, mod, wg, wu, wd, fg, final, tm):
    b, t, d = x.shape
    ff = wg.shape[1]
    row = lambda i, j: (i, j, 0)
    const = lambda i, j: (0, 0)
    once = pl.Buffered(1)
    return pl.pallas_call(
        functools.partial(_ffn_kernel, nf=2, final=final),
        grid=(b, t // tm),
        in_specs=[pl.BlockSpec((1, tm, d), row), pl.BlockSpec((1, tm, d), row),
                  pl.BlockSpec((1, mod.shape[1], d), lambda i, j: (i, 0, 5)),
                  pl.BlockSpec((d, ff), const, pipeline_mode=once),
                  pl.BlockSpec((d, ff), const, pipeline_mode=once),
                  pl.BlockSpec((ff, d), const, pipeline_mode=once),
                  pl.BlockSpec((1, d), const)],
        out_specs=pl.BlockSpec((1, tm, d), row),
        out_shape=jax.ShapeDtypeStruct((b, t, d), F32),
        compiler_params=_cparams(("arbitrary", "arbitrary"), 52),
        name="ffn_dense",
    )(h, x, mod, wg, wu, wd, fg)


def _route_kernel(lg_ref, info_ref, p_ref, cnt_ref, carry_ref, *, tr):
    i = pl.program_id(0)

    @pl.when(i == 0)
    def _():
        carry_ref[...] = jnp.zeros_like(carry_ref)

    lane = _iota2((tr, LANES), 1)
    lg = jnp.where(lane < N_EXPERTS, lg_ref[...], NEG_BIG)
    m1 = jnp.max(lg, axis=-1, keepdims=True)
    e1 = jnp.min(jnp.where(lg == m1, lane, LANES), axis=-1, keepdims=True)
    lg2 = jnp.where(lane == e1, NEG_BIG, lg)
    m2 = jnp.max(lg2, axis=-1, keepdims=True)
    e2 = jnp.min(jnp.where(lg2 == m2, lane, LANES), axis=-1, keepdims=True)
    ex = jnp.exp(m2 - m1)
    den = 1.0 + ex
    p1 = 1.0 / den
    p2 = ex / den
    oh1 = lane == e1
    oh2 = lane == e2
    oh = jnp.where(oh1 | oh2, 1.0, 0.0)
    lstrict = (_iota2((tr, tr), 0) > _iota2((tr, tr), 1)).astype(BF16)
    cex = _mm(lstrict, oh.astype(BF16)) + carry_ref[...]
    rank1 = jnp.sum(jnp.where(oh1, cex, 0.0), axis=-1, keepdims=True).astype(jnp.int32)
    rank2 = jnp.sum(jnp.where(oh2, cex, 0.0), axis=-1, keepdims=True).astype(jnp.int32)
    carry_ref[...] = carry_ref[...] + jnp.sum(oh, axis=0, keepdims=True)
    info_ref[...] = jnp.where(lane == 0, e1, jnp.where(lane == 1, e2, jnp.where(lane == 2, rank1,
                              jnp.where(lane == 3, rank2, 0))))
    p_ref[...] = jnp.where(lane == 0, p1, jnp.where(lane == 1, p2, 0.0))

    @pl.when(i == pl.num_programs(0) - 1)
    def _():
        cnt_ref[...] = carry_ref[...].astype(jnp.int32)


def _route(logits, tr):
    n = logits.shape[0]
    blk = pl.BlockSpec((tr, LANES), lambda i: (i, 0))
    return pl.pallas_call(
        functools.partial(_route_kernel, tr=tr),
        grid=(n // tr,),
        in_specs=[blk],
        out_specs=[blk, blk, pl.BlockSpec((1, LANES), lambda i: (0, 0))],
        out_shape=[jax.ShapeDtypeStruct((n, LANES), jnp.int32), jax.ShapeDtypeStruct((n, LANES), F32),
                   jax.ShapeDtypeStruct((1, LANES), jnp.int32)],
        scratch_shapes=[pltpu.VMEM((1, LANES), F32)],
        compiler_params=_cparams(("arbitrary",), 24),
        name="moe_route",
    )(logits)


def _dest_kernel(info_ref, sp_ref, d_ref):
    info = info_ref[...]
    lane = _iota2(info.shape, 1)
    sp = sp_ref[...].astype(F32)
    start1 = jnp.sum(jnp.where(lane == info[:, 0:1], sp, 0.0), axis=-1, keepdims=True).astype(jnp.int32)
    start2 = jnp.sum(jnp.where(lane == info[:, 1:2], sp, 0.0), axis=-1, keepdims=True).astype(jnp.int32)
    d_ref[...] = jnp.where(lane == 0, start1 + info[:, 2:3], jnp.where(lane == 1, start2 + info[:, 3:4], 0))


def _dest(info, sp_row, tr):
    n = info.shape[0]
    blk = pl.BlockSpec((tr, LANES), lambda i: (i, 0))
    return pl.pallas_call(
        _dest_kernel,
        grid=(n // tr,),
        in_specs=[blk, pl.BlockSpec((1, LANES), lambda i: (0, 0))],
        out_specs=blk,
        out_shape=jax.ShapeDtypeStruct((n, LANES), jnp.int32),
        compiler_params=_cparams(("arbitrary",), 24),
        name="moe_dest",
    )(info, sp_row)


ROW_DMA_UNROLL = 8


def _dispatch_kernel(d1_ref, d2_ref, sp_ref, ep_ref, h_ref, xb_ref, zero_ref, sem, zsem, *, td):
    i = pl.program_id(0)

    @pl.when(i == 0)
    def _():
        zero_ref[...] = jnp.zeros_like(zero_ref)

        def zero_block(start):
            cp = pltpu.make_async_copy(zero_ref, xb_ref.at[pl.ds(pl.multiple_of(start, MOE_ROWS), MOE_ROWS), :], zsem)
            cp.start()
            cp.wait()

        for e in range(N_EXPERTS):
            @pl.when(ep_ref[e] > sp_ref[e])
            def _():
                zero_block(ep_ref[e] - MOE_ROWS)

        def tail(blk, carry):
            zero_block(blk * MOE_ROWS)
            return carry

        lax.fori_loop(ep_ref[N_EXPERTS - 1] // MOE_ROWS, xb_ref.shape[0] // MOE_ROWS, tail, 0)

    base = i * td

    def row_copy(j, dst):
        return pltpu.make_async_copy(h_ref.at[pl.ds(j, 1), :], xb_ref.at[pl.ds(dst, 1), :], sem)

    def issue(j8, carry):
        for u in range(ROW_DMA_UNROLL):
            j = j8 * ROW_DMA_UNROLL + u
            row_copy(j, d1_ref[base + j]).start(priority=0)
            row_copy(j, d2_ref[base + j]).start(priority=1)
        return carry

    lax.fori_loop(0, td // ROW_DMA_UNROLL, issue, 0)
    for _ in range(2):
        pltpu.make_async_copy(h_ref, xb_ref.at[pl.ds(0, td), :], sem).wait()


def _dispatch(d1, d2, sp, ep, h2, n_rows, td):
    n, d = h2.shape
    return pl.pallas_call(
        functools.partial(_dispatch_kernel, td=td),
        grid_spec=pltpu.PrefetchScalarGridSpec(
            num_scalar_prefetch=4,
            grid=(n // td,),
            in_specs=[pl.BlockSpec((td, d), lambda i, *_: (i, 0))],
            out_specs=pl.BlockSpec(memory_space=pl.ANY),
            scratch_shapes=[pltpu.VMEM((MOE_ROWS, d), F32), pltpu.SemaphoreType.DMA, pltpu.SemaphoreType.DMA]),
        out_shape=jax.ShapeDtypeStruct((n_rows, d), F32),
        compiler_params=_cparams(("arbitrary",), 24),
        name="moe_dispatch",
    )(d1, d2, sp, ep, h2)


def _expert_kernel(be_ref, nu_ref, xb_ref, wg_ref, wu_ref, wd_ref, yb_ref):
    j = pl.program_id(0)
    f = pl.program_id(1)

    @pl.when(j < nu_ref[0])
    def _():
        xg = xb_ref[...].astype(BF16)
        act = (_silu(_mm(xg, wg_ref[0])) * _mm(xg, wu_ref[0])).astype(BF16)
        part = _mm(act, wd_ref[0])

        @pl.when(f == 0)
        def _():
            yb_ref[...] = part

        @pl.when(f > 0)
        def _():
            yb_ref[...] = yb_ref[...] + part

    @pl.when((j >= nu_ref[0]) & (f == 0))
    def _():
        yb_ref[...] = jnp.zeros_like(yb_ref)


def _experts(block_e, n_used, xb, wg, wu, wd, tf):
    n_rows, d = xb.shape
    ff = wg.shape[2]
    nf = ff // tf
    n_blocks = n_rows // MOE_ROWS

    def blk(j, f, be, nu):
        return (jnp.minimum(j, nu[0] - 1), 0)

    def fcol(j, f, be, nu):
        return jnp.where(j < nu[0], f, nf - 1)

    return pl.pallas_call(
        _expert_kernel,
        grid_spec=pltpu.PrefetchScalarGridSpec(
            num_scalar_prefetch=2,
            grid=(n_blocks, nf),
            in_specs=[pl.BlockSpec((MOE_ROWS, d), blk),
                      pl.BlockSpec((1, d, tf), lambda j, f, be, nu: (be[j], 0, fcol(j, f, be, nu))),
                      pl.BlockSpec((1, d, tf), lambda j, f, be, nu: (be[j], 0, fcol(j, f, be, nu))),
                      pl.BlockSpec((1, tf, d), lambda j, f, be, nu: (be[j], fcol(j, f, be, nu), 0))],
            out_specs=pl.BlockSpec((MOE_ROWS, d), lambda j, f, be, nu: (j, 0))),
        out_shape=jax.ShapeDtypeStruct((n_rows, d), F32),
        compiler_params=_cparams(("arbitrary", "arbitrary"), 56),
        name="moe_experts",
    )(block_e, n_used, xb, wg, wu, wd)


def _combine_kernel(d1_ref, d2_ref, x_ref, gt_ref, p_ref, fg_ref, yb_ref, o_ref, y1_ref, y2_ref, sem, *,
                    tm, final, row_off):
    base = row_off + (pl.program_id(0) * pl.num_programs(1) + pl.program_id(1)) * tm

    def row_copy(src, buf_ref, j):
        return pltpu.make_async_copy(yb_ref.at[pl.ds(src, 1), :], buf_ref.at[pl.ds(j, 1), :], sem)

    def issue(j8, carry):
        for u in range(min(ROW_DMA_UNROLL, tm)):
            j = j8 * ROW_DMA_UNROLL + u
            row_copy(d1_ref[base + j], y1_ref, j).start(priority=0)
            row_copy(d2_ref[base + j], y2_ref, j).start(priority=1)
        return carry

    lax.fori_loop(0, tm // ROW_DMA_UNROLL, issue, 0)
    for buf_ref in (y1_ref, y2_ref):
        pltpu.make_async_copy(yb_ref.at[pl.ds(0, tm), :], buf_ref, sem).wait()

    p = p_ref[...]
    f = y1_ref[...] * p[:, 0:1] + y2_ref[...] * p[:, 1:2]
    x2 = x_ref[0] + _gated(f, gt_ref)
    o_ref[0] = _final_norm(x2, fg_ref) if final else x2


def _combine(d1, d2, x, mod, probs, fg, yb, final, tm, row_off):
    b, t, d = x.shape
    nt = t // tm
    blk_off = row_off // tm
    return pl.pallas_call(
        functools.partial(_combine_kernel, tm=tm, final=final, row_off=row_off),
        grid_spec=pltpu.PrefetchScalarGridSpec(
            num_scalar_prefetch=2,
            grid=(b, nt),
            in_specs=[pl.BlockSpec((1, tm, d), lambda i, j, *_: (i, j, 0)),
                      pl.BlockSpec((1, mod.shape[1], d), lambda i, j, *_: (i, 0, 5)),
                      pl.BlockSpec((tm, LANES), lambda i, j, *_: (blk_off + i * nt + j, 0)),
                      pl.BlockSpec((1, d), lambda i, j, *_: (0, 0)),
                      pl.BlockSpec(memory_space=pl.ANY)],
            out_specs=pl.BlockSpec((1, tm, d), lambda i, j, *_: (i, j, 0)),
            scratch_shapes=[pltpu.VMEM((tm, d), F32), pltpu.VMEM((tm, d), F32), pltpu.SemaphoreType.DMA]),
        out_shape=jax.ShapeDtypeStruct((b, t, d), F32),
        compiler_params=_cparams(("arbitrary", "arbitrary"), 24),
        name="moe_combine",
    )(d1, d2, x, mod, probs, fg, yb)


def _moe(h2, logits, trunks, wg, wu, wd, fg, final):
    n, d = h2.shape
    tile = lambda cap: max(r for r in (cap, cap // 2, cap // 4, cap // 8, cap // 16) if n % r == 0)
    info, probs, cnt = _route(logits, tile(256))
    counts = cnt[0, :N_EXPERTS]
    padded = (counts + MOE_ROWS - 1) // MOE_ROWS * MOE_ROWS
    ep = jnp.cumsum(padded).astype(jnp.int32)
    sp = ep - padded
    n_blocks = -(-(2 * n) // MOE_ROWS) + N_EXPERTS
    n_used = jnp.maximum(ep[-1] // MOE_ROWS, 1).astype(jnp.int32)
    blk_start = jnp.minimum(jnp.arange(n_blocks, dtype=jnp.int32), n_used - 1) * MOE_ROWS
    block_e = jnp.minimum(jnp.sum(blk_start[:, None] >= ep[None, :], axis=1), N_EXPERTS - 1).astype(jnp.int32)
    dest = _dest(info, jnp.pad(sp, (0, LANES - N_EXPERTS))[None, :], tile(1024))
    d1, d2 = dest[:, 0], dest[:, 1]
    xb = _dispatch(d1, d2, sp, ep, h2, n_blocks * MOE_ROWS, tile(256))
    yb = _experts(block_e, n_used.reshape(1), xb, wg, wu, wd, MOE_FF_TILE)
    return [_combine(d1, d2, x1, mod, probs, fg, yb, final, tm, off) for x1, mod, tm, off in trunks]


def _pad_rows_front(a, rows):
    pad = [(0, 0)] * a.ndim
    pad[-2] = (rows - a.shape[-2], 0)
    return jnp.pad(a, pad)


def _trunks(xs, mods, states, w):
    depth = w["wa"].shape[0]
    n_tr = len(xs)
    shapes = [x.shape for x in xs]
    d = shapes[0][2]
    offs = [sum(s[0] * s[1] for s in shapes[:i]) for i in range(n_tr)]
    n_all = sum(s[0] * s[1] for s in shapes)
    tms, mvs = [], []
    for (b, t, _), m in zip(shapes, mods):
        if t < TOKEN_TILE and TOKEN_TILE % t == 0 and (b * t) % TOKEN_TILE == 0:
            nseg = TOKEN_TILE // t
            tms.append(TOKEN_TILE)
            mvs.append(m.reshape(depth, b // nseg, nseg, m.shape[-1]))
        else:
            tms.append(min(t, TOKEN_TILE))
            mvs.append(m)
    tok = lambda a, i: a.reshape(-1, tms[i] if shapes[i][1] < tms[i] else shapes[i][1], a.shape[-1])
    seq = lambda a, i: a.reshape(shapes[i][0], shapes[i][1], a.shape[-1])
    xs = [tok(x, i) for i, x in enumerate(xs)]
    new_states = [([], [], [], []) for _ in xs]
    for l in range(depth):
        final = l == depth - 1
        j = l // 2
        x1s, flat = [], None
        for i, x in enumerate(xs):
            t = shapes[i][1]
            ml = mvs[i][l]
            conv0, gdn0, shift0, rwkv0 = states[i]
            pa, pab, pb = (seq(a, i) for a in _norm_proj(x, ml, w["norm1_g"][l], w["wa"][l], w["wab"][l],
                                                         w["wb"][l], tms[i]))
            oa, sg = _gdn(pa, pab, _pad_rows_front(conv0[l], SUBLANES), gdn0[l], w["conv_w"][l], w["alog"][l],
                          w["dtb"][l], w["onorm_g"][l], min(t, GDN_TILE))
            ob, sr = _rwkv(pb, _pad_rows_front(shift0[l], SUBLANES), rwkv0[l], [v[l] for v in w["rwkv_vecs"]],
                           w["w2p"][l], w["a2p"][l], w["g2"][l], w["bd"], min(t, RWKV_TILE))
            for lst, val in zip(new_states[i], (pa[:, t - (GDN_CONV - 1):, :w["conv_w"].shape[2]], sg,
                                                pb[:, t - 1:, :], sr)):
                lst.append(val)
            oa, ob = tok(oa, i), tok(ob, i)
            if l % 2 == 0:
                x1, h2 = _out_proj(oa, ob, x, ml, w["norm2_g"][l], w["woa"][l], w["wob"][l], tms[i])
                xs[i] = _ffn(h2, x1, ml, w["ffn_g"][j], w["ffn_u"][j], w["ffn_d"][j], w["final_g"], final, tms[i])
            else:
                x1, *flat = _out_proj(oa, ob, x, ml, w["norm2_g"][l], w["woa"][l], w["wob"][l], tms[i],
                                      router=w["router"][j], flat_rows=n_all, row_off=offs[i], flat_prev=flat)
                x1s.append(x1)
        if l % 2 == 1:
            trunks = [(x1s[i], mvs[i][l], tms[i], offs[i]) for i in range(n_tr)]
            xs = _moe(flat[0], flat[1], trunks, w["moe_g"][j], w["moe_u"][j], w["moe_d"][j], w["final_g"], final)
    return [seq(x, i) for i, x in enumerate(xs)], [tuple(jnp.stack(s) for s in st) for st in new_states]


def kernel(x_prompt, x_sample, c_prompt, c_sample, state_gdn_conv, state_gdn, state_rwkv_shift, state_rwkv, w_ada, b_ada, norm1_g, norm2_g, w_in, gdn_conv_w, gdn_a_log, gdn_dt_bias, gdn_onorm_g, rwkv_mu, rwkv_w0, rwkv_w2, rwkv_a0, rwkv_a2, rwkv_g2, rwkv_k_k, rwkv_k_a, rwkv_r_k, rwkv_ln_w, rwkv_ln_b, w_out, ffn_w_gate, ffn_w_up, ffn_w_down, moe_router, moe_w_gate, moe_w_up, moe_w_down, final_g):
    depth, d, _ = w_in.shape
    nbp = x_prompt.shape[0]
    nbs = x_sample.shape[0]
    nqkvz = 4 * GDN_HEADS * GDN_D
    nab = 2 * GDN_HEADS
    nw = RWKV_HEADS * RWKV_N
    lora_w = rwkv_w2.shape[1]

    def lane_pad(v):
        return jnp.pad(v, ((0, 0), (0, LANES - v.shape[1])))[:, None, :]

    rows = lambda v: v[:, None, :]
    hi = jnp.arange(nw) // RWKV_N
    w = dict(
        norm1_g=rows(norm1_g), norm2_g=rows(norm2_g), final_g=final_g[None, :],
        wa=w_in[:, :, :nqkvz].astype(BF16),
        wab=jnp.pad(w_in[:, :, nqkvz:nqkvz + nab], ((0, 0), (0, 0), (0, LANES - nab))).astype(BF16),
        wb=w_in[:, :, nqkvz + nab:].astype(BF16),
        conv_w=gdn_conv_w, alog=lane_pad(gdn_a_log), dtb=lane_pad(gdn_dt_bias), onorm_g=rows(gdn_onorm_g),
        rwkv_vecs=[rows(rwkv_mu), rows(rwkv_w0), rows(rwkv_a0), rows(rwkv_k_k), rows(rwkv_k_a),
                   rwkv_r_k.reshape(depth, 1, nw), rows(rwkv_ln_w), rows(rwkv_ln_b)],
        w2p=jnp.pad(rwkv_w2, ((0, 0), (0, LANES - lora_w), (0, 0))),
        a2p=jnp.pad(rwkv_a2, ((0, 0), (lora_w, LANES - lora_w - rwkv_a2.shape[1]), (0, 0))),
        g2=rwkv_g2,
        bd=(hi[:, None] == hi[None, :]).astype(F32),
        woa=w_out[:, :GDN_HEADS * GDN_D, :].astype(BF16), wob=w_out[:, GDN_HEADS * GDN_D:, :].astype(BF16),
        ffn_g=ffn_w_gate.astype(BF16), ffn_u=ffn_w_up.astype(BF16), ffn_d=ffn_w_down.astype(BF16),
        router=jnp.pad(moe_router, ((0, 0), (0, 0), (0, LANES - moe_router.shape[2]))),
        moe_g=moe_w_gate.astype(BF16), moe_u=moe_w_up.astype(BF16), moe_d=moe_w_down.astype(BF16),
    )

    nb_all = nbp + nbs
    bp = -(-nb_all // SUBLANES) * SUBLANES
    c_all = jnp.pad(jnp.concatenate([c_prompt, c_sample], axis=0), ((0, bp - nb_all), (0, 0)))
    mod = _ada_mod(c_all, w_ada, b_ada)[:, :, None, :]
    mod_p, mod_s = mod[:, :nbp], mod[:, nbp:nb_all]

    dt = x_prompt.dtype
    zc = jnp.zeros((depth, nbp) + state_gdn_conv.shape[2:], dt)
    zg = jnp.zeros((depth, nbp) + state_gdn.shape[2:], dt)
    zs = jnp.zeros((depth, nbp) + state_rwkv_shift.shape[2:], dt)
    zr = jnp.zeros((depth, nbp) + state_rwkv.shape[2:], dt)
    (y_p, y_s), (st_p, st_s) = _trunks(
        [x_prompt, x_sample], [mod_p, mod_s],
        [(zc, zg, zs, zr), (state_gdn_conv, state_gdn, state_rwkv_shift, state_rwkv)], w)
    return (y_p, y_s) + st_p + st_s
```

```python
import functools

import jax
import jax.numpy as jnp
from jax import lax
from jax.experimental import pallas as pl
from jax.experimental.pallas import tpu as pltpu

F32 = jnp.float32
BF16 = jnp.bfloat16
HI = lax.Precision.HIGHEST

LANES = 128
SUBLANES = 8
CHUNK = 64
GDN_HEADS = 4
GDN_D = 128
RWKV_HEADS = 8
RWKV_N = 64
GDN_CONV = 4
N_EXPERTS = 8
NORM_EPS = 1e-6
L2_EPS = 1e-6
GN_EPS = 64e-5
NEG_BIG = -1e30
RWKV_DECAY_SCALE = 0.6065306597126334
TOKEN_TILE = 512
GDN_TILE = 512
RWKV_TILE = 256
SHORT_SEQ_BATCH = 4
MOE_ROWS = 512
MOE_FF_TILE = 1792
MIB = 1024 * 1024

P_GDN_SCORE = 1
P_GDN_INV = 3
P_GDN_SOLVE = 1
P_GDN_STATE = 1
P_RWKV_SMALL = 1
P_RWKV_SCORE = 1
P_RWKV_INV = 1
P_RWKV_MID = 1
P_RWKV_STATE = 1
P_ROUTER = 3


def _cparams(sem, vmem_mib):
    return pltpu.CompilerParams(dimension_semantics=sem, vmem_limit_bytes=vmem_mib * MIB)


def _split_bf16(x):
    hi = x.astype(BF16)
    return hi, (x - hi.astype(F32)).astype(BF16)


def _split3_bf16(x):
    hi = x.astype(BF16)
    r1 = x - hi.astype(F32)
    mid = r1.astype(BF16)
    return hi, mid, (r1 - mid.astype(F32)).astype(BF16)


_CONTRACT = {"nn": (1, 0), "nt": (1, 1), "tn": (0, 0)}


def _dg(a, b, kind, prec):
    off = a.ndim - 2
    ca, cb = _CONTRACT[kind]
    dn = (((ca + off,), (cb + off,)), (((0,), (0,)) if off else ((), ())))
    dot = lambda x, y: lax.dot_general(x, y, dn, preferred_element_type=F32)
    if prec is None or prec is HI:
        return lax.dot_general(a, b, dn, preferred_element_type=F32, precision=prec)
    if prec == 1:
        return dot(a.astype(BF16), b.astype(BF16))
    if prec == "l3":
        bb = b.astype(BF16)
        a0, a1, a2 = _split3_bf16(a)
        return dot(a0, bb) + dot(a1, bb) + dot(a2, bb)
    if prec == "r3":
        ab = a.astype(BF16)
        b0, b1, b2 = _split3_bf16(b)
        return dot(ab, b0) + dot(ab, b1) + dot(ab, b2)
    ah, al = _split_bf16(a)
    bh, bl = _split_bf16(b)
    return dot(ah, bh) + dot(ah, bl) + dot(al, bh)


def _mm(a, b, prec=None):
    return _dg(a, b, "nn", prec)


def _mm_nt(a, b, prec=None):
    return _dg(a, b, "nt", prec)


def _mm_tn(a, b, prec=None):
    return _dg(a, b, "tn", prec)


def _silu(x):
    return x * jax.nn.sigmoid(x)


def _softplus(x):
    return jnp.maximum(x, 0.0) + jnp.log1p(jnp.exp(-jnp.abs(x)))


def _iota2(shape, dim):
    return lax.broadcasted_iota(jnp.int32, shape, dim)


def _per_segment(x, fn, *mod_refs):
    mods = [r[0] for r in mod_refs]
    nseg = mods[0].shape[0]
    if nseg == 1:
        return fn(x, *mods)
    rows = x.shape[0] // nseg
    return jnp.concatenate([fn(x[s * rows:(s + 1) * rows], *(m[s:s + 1] for m in mods)) for s in range(nseg)], axis=0)


def _scale_shift(xn, sc_ref, sh_ref):
    return _per_segment(xn, lambda x, sc, sh: x * (1.0 + sc) + sh, sc_ref, sh_ref)


def _gated(f, gt_ref):
    return _per_segment(f, lambda x, gt: gt * x, gt_ref)


def _shift_rows(x, prev8, k):
    head = jnp.where(_iota2(prev8.shape, 0) < k, pltpu.roll(prev8, k, 0), pltpu.roll(x[0:SUBLANES], k, 0))
    return jnp.concatenate([head, pltpu.roll(x, k, 0)[SUBLANES:]], axis=0)


def _causal_taps(x, w, prev8, ntap):
    y = x * w[ntap - 1:ntap, :]
    for k in range(1, ntap):
        y = y + _shift_rows(x, prev8, k) * w[ntap - 1 - k:ntap - k, :]
    return y


def _neumann_inverse(x, eye, prec):
    c = x.shape[-1]
    z = jnp.concatenate([x, jnp.broadcast_to(eye, x.shape)], axis=-1)
    keep_s = _iota2((c, 2 * c), 1) >= c
    for _ in range(6):
        z = _mm(z[..., :c], z, prec) + jnp.where(keep_s, z, 0.0)
    return z[..., c:]


def _ada_kernel(c_ref, w_ref, b_ref, o_ref):
    o_ref[0] = _mm(_silu(c_ref[...]), w_ref[0], HI) + b_ref[0]


def _ada_mod(c_all, w_ada, b_ada):
    depth, d, n6 = w_ada.shape
    bp = c_all.shape[0]
    tn = 1536
    return pl.pallas_call(
        _ada_kernel,
        grid=(depth, n6 // tn),
        in_specs=[pl.BlockSpec((bp, d), lambda l, j: (0, 0)),
                  pl.BlockSpec((1, d, tn), lambda l, j: (l, 0, j)),
                  pl.BlockSpec((1, 1, tn), lambda l, j: (l, 0, j))],
        out_specs=pl.BlockSpec((1, bp, tn), lambda l, j: (l, 0, j)),
        out_shape=jax.ShapeDtypeStruct((depth, bp, n6), F32),
        compiler_params=_cparams(("arbitrary", "arbitrary"), 40),
        name="ada_mod",
    )(c_all, w_ada, b_ada.reshape(depth, 1, n6))


def _norm_proj_kernel(x_ref, sh_ref, sc_ref, g_ref, wa_ref, wab_ref, wb_ref, pa_ref, pab_ref, pb_ref):
    x = x_ref[0]
    rs = lax.rsqrt(jnp.mean(x * x, axis=-1, keepdims=True) + NORM_EPS)
    hb = _scale_shift(x * rs * g_ref[...], sc_ref, sh_ref).astype(BF16)
    pa_ref[0] = _mm(hb, wa_ref[...])
    pab_ref[0] = _mm(hb, wab_ref[...])
    pb_ref[0] = _mm(hb, wb_ref[...])


def _norm_proj(x, mod, g, wa, wab, wb, tm):
    b, t, d = x.shape
    na, nab, nb = wa.shape[1], wab.shape[1], wb.shape[1]
    nseg = mod.shape[1]
    row = lambda i, j: (i, j, 0)
    const = lambda i, j: (0, 0)
    return pl.pallas_call(
        _norm_proj_kernel,
        grid=(b, t // tm),
        in_specs=[pl.BlockSpec((1, tm, d), row),
                  pl.BlockSpec((1, nseg, d), lambda i, j: (i, 0, 0)),
                  pl.BlockSpec((1, nseg, d), lambda i, j: (i, 0, 1)),
                  pl.BlockSpec((1, d), const),
                  pl.BlockSpec((d, na), const),
                  pl.BlockSpec((d, nab), const),
                  pl.BlockSpec((d, nb), const)],
        out_specs=[pl.BlockSpec((1, tm, na), row), pl.BlockSpec((1, tm, nab), row), pl.BlockSpec((1, tm, nb), row)],
        out_shape=[jax.ShapeDtypeStruct((b, t, na), F32), jax.ShapeDtypeStruct((b, t, nab), F32),
                   jax.ShapeDtypeStruct((b, t, nb), F32)],
        compiler_params=_cparams(("arbitrary", "arbitrary"), 48),
        name="norm_proj",
    )(x, mod, mod, g, wa, wab, wb)


def _gdn_kernel(pa_ref, pab_ref, cbuf_ref, s0_ref, cw_ref, alog_ref, dtb_ref, og_ref,
                o_ref, sfin_ref, xc_ref, s_ref, *, tc):
    t = pl.program_id(1)
    nqk = GDN_HEADS * GDN_D
    nconv = 3 * nqk

    nsq = pa_ref.shape[0]
    nh = GDN_HEADS

    @pl.when(t == 0)
    def _():
        xc_ref[...] = cbuf_ref[...]
        s_ref[...] = s0_ref[...].reshape(s_ref.shape)

    ys = []
    for sq in range(nsq):
        x = pa_ref[sq, :, 0:nconv]
        ys.append(_causal_taps(x, cw_ref[...], xc_ref[sq], GDN_CONV))
        xc_ref[sq] = x[tc - SUBLANES:tc, :]
    qkv = _silu(ys[0] if nsq == 1 else jnp.concatenate(ys, axis=0))

    ab = pab_ref[...].reshape(nsq * tc, LANES)
    gmat = -jnp.exp(alog_ref[...]) * _softplus(ab + dtb_ref[...])
    bmat = jax.nn.sigmoid(ab)

    c = CHUNK
    nch = nsq * tc // c
    nb = nch * nh
    ri = _iota2((c, c), 0)
    ci = _iota2((c, c), 1)
    incl = ri >= ci
    strict = ri > ci
    eye_c = (ri == ci).astype(F32)

    def heads(x, width):
        x = x.reshape(nch, c, nh * width)
        return jnp.stack([x[:, :, h * width:(h + 1) * width] for h in range(nh)], axis=1).reshape(nb, c, width)

    q = heads(qkv[:, 0:nqk], GDN_D)
    k = heads(qkv[:, nqk:2 * nqk], GDN_D)
    v = heads(qkv[:, 2 * nqk:3 * nqk], GDN_D)
    ones_d = jnp.ones((GDN_D, GDN_D), BF16)

    def l2n(x, scale):
        ss = _mm((x * x).reshape(nb * c, GDN_D), ones_d, 1).reshape(nb, c, GDN_D)
        return x * (lax.rsqrt(ss + L2_EPS) * scale)

    q = l2n(q, GDN_D ** -0.5)
    k = l2n(k, 1.0)
    beta = heads(bmat[:, nh:2 * nh], 1)

    g4 = gmat.reshape(nch, c, LANES)
    gcol4 = _mm(jnp.broadcast_to(incl.astype(F32), (nch, c, c)), g4, "r3")
    grow4 = _mm_tn(g4, jnp.broadcast_to((ri <= ci).astype(F32), (nch, c, c)), "l3")
    gc = jnp.stack([gcol4[:, :, h:h + 1] for h in range(nh)], axis=1).reshape(nb, c, 1)
    gr = jnp.stack([grow4[:, h:h + 1, :] for h in range(nh)], axis=1).reshape(nb, 1, c)
    decay = jnp.exp(jnp.where(incl, gc - gr, NEG_BIG))

    kb = k * beta
    m = jnp.where(strict, _mm_nt(kb, k, P_GDN_SCORE) * decay, 0.0)
    t0 = _neumann_inverse(-m, eye_c, 1)
    tinv = t0 + _mm(t0, (eye_c - t0) - _mm(m, t0, P_GDN_INV), 1)
    egc = jnp.exp(gc)
    sol = _mm(tinv, jnp.concatenate([v * beta, kb * egc], axis=2), P_GDN_SOLVE)
    u, w = sol[:, :, :GDN_D], sol[:, :, GDN_D:]
    aqk = _mm_nt(q, k, P_GDN_SCORE) * decay
    gl = gc[:, c - 1:c, :]
    kd = k * jnp.exp(gl - gc)
    qg = q * egc
    egl = jnp.exp(gl)

    s = s_ref[...]
    outs = []
    for ch in range(tc // c):
        bs = slice(ch * nh, (ch + 1) * nh) if nsq == 1 else slice(None)
        v_new = u[bs] - _mm(w[bs], s, P_GDN_STATE)
        outs.append(_mm(qg[bs], s, P_GDN_STATE) + _mm(aqk[bs], v_new, P_GDN_STATE))
        s = s * egl[bs] + _mm_tn(kd[bs], v_new, P_GDN_STATE)
    s_ref[...] = s

    o = jnp.stack(outs, axis=0).reshape(nch, nh, c, GDN_D)
    o = o * lax.rsqrt(jnp.mean(o * o, axis=-1, keepdims=True) + NORM_EPS) * og_ref[...]
    for h in range(nh):
        sl = slice(h * GDN_D, (h + 1) * GDN_D)
        z = pa_ref[:, :, nconv + h * GDN_D: nconv + (h + 1) * GDN_D]
        o_ref[:, :, sl] = (o[:, h].reshape(nsq, tc, GDN_D) * _silu(z)).astype(o_ref.dtype)

    @pl.when(t == pl.num_programs(1) - 1)
    def _():
        sfin_ref[...] = s_ref[...].reshape(sfin_ref.shape)


def _seqs_per_step(b, t, tc):
    return max(n for n in (SHORT_SEQ_BATCH, 2, 1) if b % n == 0) if t == tc == CHUNK else 1


def _gdn(pa, pab, cbuf8, s0, cw, alog, dtb, og, tc):
    b, t, na = pa.shape
    nconv = cw.shape[1]
    nsq = _seqs_per_step(b, t, tc)
    row = lambda i, j: (i, j, 0)
    const2 = lambda i, j: (0, 0)
    return pl.pallas_call(
        functools.partial(_gdn_kernel, tc=tc),
        grid=(b // nsq, t // tc),
        in_specs=[pl.BlockSpec((nsq, tc, na), row),
                  pl.BlockSpec((nsq, tc, LANES), row),
                  pl.BlockSpec((nsq, SUBLANES, nconv), lambda i, j: (i, 0, 0)),
                  pl.BlockSpec((nsq, GDN_HEADS, GDN_D, GDN_D), lambda i, j: (i, 0, 0, 0)),
                  pl.BlockSpec((GDN_CONV, nconv), const2),
                  pl.BlockSpec((1, LANES), const2),
                  pl.BlockSpec((1, LANES), const2),
                  pl.BlockSpec((1, GDN_D), const2)],
        out_specs=[pl.BlockSpec((nsq, tc, GDN_HEADS * GDN_D), row),
                   pl.BlockSpec((nsq, GDN_HEADS, GDN_D, GDN_D), lambda i, j: (i, 0, 0, 0))],
        out_shape=[jax.ShapeDtypeStruct((b, t, GDN_HEADS * GDN_D), BF16),
                   jax.ShapeDtypeStruct((b, GDN_HEADS, GDN_D, GDN_D), F32)],
        scratch_shapes=[pltpu.VMEM((nsq, SUBLANES, nconv), F32),
                        pltpu.VMEM((nsq * GDN_HEADS, GDN_D, GDN_D), F32)],
        compiler_params=_cparams(("arbitrary", "arbitrary"), 40),
        name="gdn_mixer",
    )(pa, pab, cbuf8, s0, cw, alog, dtb, og)


def _rwkv_kernel(pb_ref, sbuf_ref, s0_ref, mu_ref, w0_ref, w2_ref, a0_ref, a2_ref, g2_ref, kk_ref, ka_ref,
                 rk_ref, lnw_ref, lnb_ref, bd_ref, o_ref, sfin_ref, xc_ref, s_ref, y_ref, *, tc):
    t = pl.program_id(1)
    nw = RWKV_HEADS * RWKV_N

    nsq = pb_ref.shape[0]
    nh = RWKV_HEADS

    @pl.when(t == 0)
    def _():
        xc_ref[...] = sbuf_ref[...]
        s_ref[...] = s0_ref[...].reshape(s_ref.shape)

    xl, pl_ = [], []
    for sq in range(nsq):
        xq = pb_ref[sq]
        xl.append(xq)
        pl_.append(_shift_rows(xq, xc_ref[sq], 1))
        xc_ref[sq] = xq[tc - SUBLANES:tc, :]
    x = xl[0] if nsq == 1 else jnp.concatenate(xl, axis=0)
    prev = pl_[0] if nsq == 1 else jnp.concatenate(pl_, axis=0)
    xs = x + (prev - x) * mu_ref[...]
    r = xs[:, 0:nw]
    kr = xs[:, nw:2 * nw]
    vr = xs[:, 2 * nw:3 * nw]
    wa = xs[:, 3 * nw:3 * nw + LANES]
    gd = xs[:, 3 * nw + LANES:3 * nw + 2 * LANES]

    bd = bd_ref[...]
    ps = P_RWKV_SMALL
    lw = -RWKV_DECAY_SCALE * jax.nn.sigmoid(w0_ref[...] + _mm(jnp.tanh(wa), w2_ref[...], ps))
    a = jax.nn.sigmoid(a0_ref[...] + _mm(wa, a2_ref[...], ps))
    gate = _mm(jax.nn.sigmoid(gd), g2_ref[...], ps)
    kkr = kr * kk_ref[...]
    kk = kkr * lax.rsqrt(_mm(kkr * kkr, bd, ps) + L2_EPS)
    kb = kr * (1.0 + (a - 1.0) * ka_ref[...])
    aa = -kk
    bb = kk * a
    bonus = _mm(r * kb * rk_ref[...], bd, ps) * vr

    c = CHUNK
    n = RWKV_N
    nch = nsq * tc // c
    nb = nch * nh
    ri = _iota2((c, c), 0)
    ci = _iota2((c, c), 1)
    incl = ri >= ci
    eye = ri == ci
    eye_c = eye.astype(F32)

    def heads(x):
        return jnp.stack([x[:, :, h * n:(h + 1) * n] for h in range(nh)], axis=1).reshape(nb, x.shape[1], n)

    chunks = lambda x: x.reshape(nch, c, nw)
    lw4 = chunks(lw)
    g = _mm(jnp.broadcast_to(incl.astype(F32), (nch, c, c)), lw4, "r3")
    gfin = g[:, c - 1:c, :]
    eng = jnp.exp(-g)
    efin = jnp.exp(gfin)
    tail = efin * eng
    aa4, bb4, kb4 = chunks(aa), chunks(bb), chunks(kb)
    at = heads(aa4 * jnp.exp(g - lw4))
    bt = heads(bb4 * eng)
    kt = heads(kb4 * eng)
    rt = heads(chunks(r) * jnp.exp(g))
    bh = heads(bb4 * tail)
    kh = heads(kb4 * tail)
    egf = heads(efin)
    v4 = chunks(vr)
    upper = _iota2((c, 2 * n), 1) >= n
    vcols = [v4[:, :, (h // 2) * 2 * n:(h // 2 + 1) * 2 * n] for h in range(nh)]
    vhi = jnp.stack([jnp.where(upper, col if h % 2 else pltpu.roll(col, n, 2), 0.0)
                     for h, col in enumerate(vcols)], axis=1).reshape(nb, c, 2 * n)
    zeros_lo = jnp.zeros((nb, c, n), F32)
    pad_hi = lambda x: jnp.concatenate([x, zeros_lo], axis=-1)

    pm = P_RWKV_MID
    colj = _iota2((c, 2 * c), 1) % c
    rowi = _iota2((c, 2 * c), 0)
    ar = jnp.concatenate([at, rt], axis=1)
    xbk = _mm_nt(ar, jnp.concatenate([bt, kt], axis=1), P_RWKV_SCORE)
    l2 = jnp.where(rowi > colj, xbk[:, :c], 0.0)
    m2 = jnp.where(rowi >= colj, xbk[:, c:], 0.0)
    tinv = _neumann_inverse(l2[..., :c], eye_c, P_RWKV_INV)
    lv = _mm(l2, jnp.concatenate([jnp.zeros_like(vhi), vhi], axis=1), pm)
    wu = _mm(tinv, lv + pad_hi(at), pm)
    wv = jnp.concatenate([wu, vhi], axis=1)
    qy = _mm(m2, wv, pm) + pad_hi(rt)
    pi = _mm_tn(wv, jnp.concatenate([bh, kh], axis=1), pm)
    pt = jnp.where(eye, egf, 0.0) + pi[:, :n]
    inc = pi[:, n:]

    s = s_ref[...]
    zeros_s = jnp.zeros_like(s)
    ys = []
    for ch in range(tc // c):
        bs = slice(ch * nh, (ch + 1) * nh) if nsq == 1 else slice(None)
        ys.append(_mm_nt(qy[bs][..., :n], jnp.concatenate([zeros_s, s], axis=1), P_RWKV_STATE) + qy[bs])
        s = _mm(s, pt[bs], P_RWKV_STATE) + inc[bs]
    s_ref[...] = s
    y4 = jnp.stack(ys, axis=0).reshape(nch, nh, c, 2 * n)
    for h in range(nh):
        y_ref[:, h * n:(h + 1) * n] = y4[:, h, :, n:].reshape(nsq * tc, n)

    y = y_ref[...]
    mean = _mm(y, bd, ps) * (1.0 / n)
    dlt = y - mean
    var = _mm(dlt * dlt, bd, ps) * (1.0 / n)
    yn = dlt * lax.rsqrt(var + GN_EPS) * lnw_ref[...] + lnb_ref[...]
    o_ref[...] = ((yn + bonus) * gate).reshape(o_ref.shape).astype(o_ref.dtype)

    @pl.when(t == pl.num_programs(1) - 1)
    def _():
        sfin_ref[...] = s_ref[...].reshape(sfin_ref.shape)


def _rwkv(pb, sbuf8, s0, vecs, w2p, a2p, g2, bd, tc):
    b, t, nb = pb.shape
    nw = RWKV_HEADS * RWKV_N
    nsq = _seqs_per_step(b, t, tc)
    row = lambda i, j: (i, j, 0)
    const2 = lambda i, j: (0, 0)
    mu, w0, a0, kk, ka, rk, lnw, lnb = vecs
    vec_spec = pl.BlockSpec((1, nw), const2)
    return pl.pallas_call(
        functools.partial(_rwkv_kernel, tc=tc),
        grid=(b // nsq, t // tc),
        in_specs=[pl.BlockSpec((nsq, tc, nb), row),
                  pl.BlockSpec((nsq, SUBLANES, nb), lambda i, j: (i, 0, 0)),
                  pl.BlockSpec((nsq, RWKV_HEADS, RWKV_N, RWKV_N), lambda i, j: (i, 0, 0, 0)),
                  pl.BlockSpec((1, nb), const2),
                  vec_spec,
                  pl.BlockSpec((LANES, nw), const2),
                  vec_spec,
                  pl.BlockSpec((LANES, nw), const2),
                  pl.BlockSpec((LANES, nw), const2),
                  vec_spec, vec_spec, vec_spec, vec_spec, vec_spec,
                  pl.BlockSpec((nw, nw), const2)],
        out_specs=[pl.BlockSpec((nsq, tc, nw), row),
                   pl.BlockSpec((nsq, RWKV_HEADS, RWKV_N, RWKV_N), lambda i, j: (i, 0, 0, 0))],
        out_shape=[jax.ShapeDtypeStruct((b, t, nw), BF16),
                   jax.ShapeDtypeStruct((b, RWKV_HEADS, RWKV_N, RWKV_N), F32)],
        scratch_shapes=[pltpu.VMEM((nsq, SUBLANES, nb), F32),
                        pltpu.VMEM((nsq * RWKV_HEADS, RWKV_N, RWKV_N), F32),
                        pltpu.VMEM((nsq * tc, nw), F32)],
        compiler_params=_cparams(("arbitrary", "arbitrary"), 40),
        name="rwkv_mixer",
    )(pb, sbuf8, s0, mu, w0, w2p, a0, a2p, g2, kk, ka, rk, lnw, lnb, bd)


def _out_proj_kernel(oa_ref, ob_ref, x_ref, gt_ref, sh_ref, sc_ref, g_ref, woa_ref, wob_ref, *rest, with_router, nt):
    if with_router:
        wr_ref = rest[0]
        x1_ref, h_ref, lg_ref = rest[-3:]
    else:
        x1_ref, h_ref = rest

    @pl.when(pl.program_id(1) < nt)
    def _():
        mix = _mm(oa_ref[0], woa_ref[...]) + _mm(ob_ref[0], wob_ref[...])
        x1 = x_ref[0] + _gated(mix, gt_ref)
        x1_ref[0] = x1
        rs = lax.rsqrt(jnp.mean(x1 * x1, axis=-1, keepdims=True) + NORM_EPS)
        h = _scale_shift(x1 * rs * g_ref[...], sc_ref, sh_ref)
        if with_router:
            h_ref[...] = h
            lg_ref[...] = _mm(h, wr_ref[...], P_ROUTER)
        else:
            h_ref[0] = h.astype(h_ref.dtype)

    if with_router:
        @pl.when(pl.program_id(1) >= nt)
        def _():
            h_ref[...] = jnp.zeros_like(h_ref)
            lg_ref[...] = jnp.zeros_like(lg_ref)


def _out_proj(oa, ob, x, mod, g, woa, wob, tm, router=None, flat_rows=None, row_off=0, flat_prev=None):
    b, t, d = x.shape
    nh = oa.shape[2]
    nt = t // tm
    nseg = mod.shape[1]
    tail = 0
    if router is not None and flat_prev is None:
        assert b == 1 and row_off == 0 and (flat_rows - t) % tm == 0
        tail = (flat_rows - t) // tm
    row = lambda i, j: (i, jnp.minimum(j, nt - 1), 0)
    const = lambda i, j: (0, 0)
    in_specs = [pl.BlockSpec((1, tm, nh), row), pl.BlockSpec((1, tm, nh), row), pl.BlockSpec((1, tm, d), row),
                pl.BlockSpec((1, nseg, d), lambda i, j: (i, 0, 2)),
                pl.BlockSpec((1, nseg, d), lambda i, j: (i, 0, 3)),
                pl.BlockSpec((1, nseg, d), lambda i, j: (i, 0, 4)),
                pl.BlockSpec((1, d), const), pl.BlockSpec((nh, d), const), pl.BlockSpec((nh, d), const)]
    args = [oa, ob, x, mod, mod, mod, g, woa, wob]
    aliases = {}
    if router is None:
        out_specs = [pl.BlockSpec((1, tm, d), row), pl.BlockSpec((1, tm, d), row)]
        out_shape = [jax.ShapeDtypeStruct((b, t, d), F32), jax.ShapeDtypeStruct((b, t, d), BF16)]
    else:
        blk_off = row_off // tm
        flat = lambda i, j: (blk_off + i * nt + j, 0)
        in_specs.append(pl.BlockSpec((d, LANES), const))
        args.append(router)
        if flat_prev is not None:
            aliases = {len(args): 1, len(args) + 1: 2}
            in_specs += [pl.BlockSpec(memory_space=pl.ANY)] * 2
            args += list(flat_prev)
        out_specs = [pl.BlockSpec((1, tm, d), row), pl.BlockSpec((tm, d), flat), pl.BlockSpec((tm, LANES), flat)]
        out_shape = [jax.ShapeDtypeStruct((b, t, d), F32), jax.ShapeDtypeStruct((flat_rows, d), F32),
                     jax.ShapeDtypeStruct((flat_rows, LANES), F32)]
    return pl.pallas_call(
        functools.partial(_out_proj_kernel, with_router=router is not None, nt=nt),
        grid=(b, nt + tail),
        in_specs=in_specs, out_specs=out_specs, out_shape=out_shape, input_output_aliases=aliases,
        compiler_params=_cparams(("arbitrary", "arbitrary"), 40),
        name="out_proj" if router is None else "out_proj_router",
    )(*args)


def _final_norm(x2, fg_ref):
    rs = lax.rsqrt(jnp.mean(x2 * x2, axis=-1, keepdims=True) + NORM_EPS)
    return x2 * rs * fg_ref[...]


def _ffn_kernel(h_ref, x_ref, gt_ref, wg_ref, wu_ref, wd_ref, fg_ref, o_ref, *, nf, final):
    h = h_ref[0]
    ff = wg_ref.shape[1]
    tf = ff // nf
    acc = None
    for f in range(nf):
        cols = slice(f * tf, (f + 1) * tf)
        act = (_silu(_mm(h, wg_ref[:, cols])) * _mm(h, wu_ref[:, cols])).astype(BF16)
        part = _mm(act, wd_ref[cols, :])
        acc = part if acc is None else acc + part
    x2 = x_ref[0] + _gated(acc, gt_ref)
    o_ref[0] = _final_norm(x2, fg_ref) if final else x2


def _ffn(h, x, mod, wg, wu, wd, fg, final, tm):
    b, t, d = x.shape
    ff = wg.shape[1]
    row = lambda i, j: (i, j, 0)
    const = lambda i, j: (0, 0)
    once = pl.Buffered(1)
    return pl.pallas_call(
        functools.partial(_ffn_kernel, nf=2, final=final),
        grid=(b, t // tm),
        in_specs=[pl.BlockSpec((1, tm, d), row), pl.BlockSpec((1, tm, d), row),
                  pl.BlockSpec((1, mod.shape[1], d), lambda i, j: (i, 0, 5)),
                  pl.BlockSpec((d, ff), const, pipeline_mode=once),
                  pl.BlockSpec((d, ff), const, pipeline_mode=once),
                  pl.BlockSpec((ff, d), const, pipeline_mode=once),
                  pl.BlockSpec((1, d), const)],
        out_specs=pl.BlockSpec((1, tm, d), row),
        out_shape=jax.ShapeDtypeStruct((b, t, d), F32),
        compiler_params=_cparams(("arbitrary", "arbitrary"), 52),
        name="ffn_dense",
    )(h, x, mod, wg, wu, wd, fg)


def _route_kernel(lg_ref, info_ref, p_ref, cnt_ref, carry_ref, *, tr):
    i = pl.program_id(0)

    @pl.when(i == 0)
    def _():
        carry_ref[...] = jnp.zeros_like(carry_ref)

    lane = _iota2((tr, LANES), 1)
    lg = jnp.where(lane < N_EXPERTS, lg_ref[...], NEG_BIG)
    m1 = jnp.max(lg, axis=-1, keepdims=True)
    e1 = jnp.min(jnp.where(lg == m1, lane, LANES), axis=-1, keepdims=True)
    lg2 = jnp.where(lane == e1, NEG_BIG, lg)
    m2 = jnp.max(lg2, axis=-1, keepdims=True)
    e2 = jnp.min(jnp.where(lg2 == m2, lane, LANES), axis=-1, keepdims=True)
    ex = jnp.exp(m2 - m1)
    den = 1.0 + ex
    p1 = 1.0 / den
    p2 = ex / den
    oh1 = lane == e1
    oh2 = lane == e2
    oh = jnp.where(oh1 | oh2, 1.0, 0.0)
    lstrict = (_iota2((tr, tr), 0) > _iota2((tr, tr), 1)).astype(BF16)
    cex = _mm(lstrict, oh.astype(BF16)) + carry_ref[...]
    rank1 = jnp.sum(jnp.where(oh1, cex, 0.0), axis=-1, keepdims=True).astype(jnp.int32)
    rank2 = jnp.sum(jnp.where(oh2, cex, 0.0), axis=-1, keepdims=True).astype(jnp.int32)
    carry_ref[...] = carry_ref[...] + jnp.sum(oh, axis=0, keepdims=True)
    info_ref[...] = jnp.where(lane == 0, e1, jnp.where(lane == 1, e2, jnp.where(lane == 2, rank1,
                              jnp.where(lane == 3, rank2, 0))))
    p_ref[...] = jnp.where(lane == 0, p1, jnp.where(lane == 1, p2, 0.0))

    @pl.when(i == pl.num_programs(0) - 1)
    def _():
        cnt_ref[...] = carry_ref[...].astype(jnp.int32)


def _route(logits, tr):
    n = logits.shape[0]
    blk = pl.BlockSpec((tr, LANES), lambda i: (i, 0))
    return pl.pallas_call(
        functools.partial(_route_kernel, tr=tr),
        grid=(n // tr,),
        in_specs=[blk],
        out_specs=[blk, blk, pl.BlockSpec((1, LANES), lambda i: (0, 0))],
        out_shape=[jax.ShapeDtypeStruct((n, LANES), jnp.int32), jax.ShapeDtypeStruct((n, LANES), F32),
                   jax.ShapeDtypeStruct((1, LANES), jnp.int32)],
        scratch_shapes=[pltpu.VMEM((1, LANES), F32)],
        compiler_params=_cparams(("arbitrary",), 24),
        name="moe_route",
    )(logits)


def _dest_kernel(info_ref, sp_ref, d_ref):
    info = info_ref[...]
    lane = _iota2(info.shape, 1)
    sp = sp_ref[...].astype(F32)
    start1 = jnp.sum(jnp.where(lane == info[:, 0:1], sp, 0.0), axis=-1, keepdims=True).astype(jnp.int32)
    start2 = jnp.sum(jnp.where(lane == info[:, 1:2], sp, 0.0), axis=-1, keepdims=True).astype(jnp.int32)
    d_ref[...] = jnp.where(lane == 0, start1 + info[:, 2:3], jnp.where(lane == 1, start2 + info[:, 3:4], 0))


def _dest(info, sp_row, tr):
    n = info.shape[0]
    blk = pl.BlockSpec((tr, LANES), lambda i: (i, 0))
    return pl.pallas_call(
        _dest_kernel,
        grid=(n // tr,),
        in_specs=[blk, pl.BlockSpec((1, LANES), lambda i: (0, 0))],
        out_specs=blk,
        out_shape=jax.ShapeDtypeStruct((n, LANES), jnp.int32),
        compiler_params=_cparams(("arbitrary",), 24),
        name="moe_dest",
    )(info, sp_row)


ROW_DMA_UNROLL = 8


def _dispatch_kernel(d1_ref, d2_ref, sp_ref, ep_ref, h_ref, xb_ref, zero_ref, sem, zsem, *, td):
    i = pl.program_id(0)

    @pl.when(i == 0)
    def _():
        zero_ref[...] = jnp.zeros_like(zero_ref)

        def zero_block(start):
            cp = pltpu.make_async_copy(zero_ref, xb_ref.at[pl.ds(pl.multiple_of(start, MOE_ROWS), MOE_ROWS), :], zsem)
            cp.start()
            cp.wait()

        for e in range(N_EXPERTS):
            @pl.when(ep_ref[e] > sp_ref[e])
            def _():
                zero_block(ep_ref[e] - MOE_ROWS)

        def tail(blk, carry):
            zero_block(blk * MOE_ROWS)
            return carry

        lax.fori_loop(ep_ref[N_EXPERTS - 1] // MOE_ROWS, xb_ref.shape[0] // MOE_ROWS, tail, 0)

    base = i * td

    def row_copy(j, dst):
        return pltpu.make_async_copy(h_ref.at[pl.ds(j, 1), :], xb_ref.at[pl.ds(dst, 1), :], sem)

    def issue(j8, carry):
        for u in range(ROW_DMA_UNROLL):
            j = j8 * ROW_DMA_UNROLL + u
            row_copy(j, d1_ref[base + j]).start(priority=0)
            row_copy(j, d2_ref[base + j]).start(priority=1)
        return carry

    lax.fori_loop(0, td // ROW_DMA_UNROLL, issue, 0)
    for _ in range(2):
        pltpu.make_async_copy(h_ref, xb_ref.at[pl.ds(0, td), :], sem).wait()


def _dispatch(d1, d2, sp, ep, h2, n_rows, td):
    n, d = h2.shape
    return pl.pallas_call(
        functools.partial(_dispatch_kernel, td=td),
        grid_spec=pltpu.PrefetchScalarGridSpec(
            num_scalar_prefetch=4,
            grid=(n // td,),
            in_specs=[pl.BlockSpec((td, d), lambda i, *_: (i, 0))],
            out_specs=pl.BlockSpec(memory_space=pl.ANY),
            scratch_shapes=[pltpu.VMEM((MOE_ROWS, d), F32), pltpu.SemaphoreType.DMA, pltpu.SemaphoreType.DMA]),
        out_shape=jax.ShapeDtypeStruct((n_rows, d), F32),
        compiler_params=_cparams(("arbitrary",), 24),
        name="moe_dispatch",
    )(d1, d2, sp, ep, h2)


def _expert_kernel(be_ref, nu_ref, xb_ref, wg_ref, wu_ref, wd_ref, yb_ref):
    j = pl.program_id(0)
    f = pl.program_id(1)

    @pl.when(j < nu_ref[0])
    def _():
        xg = xb_ref[...].astype(BF16)
        act = (_silu(_mm(xg, wg_ref[0])) * _mm(xg, wu_ref[0])).astype(BF16)
        part = _mm(act, wd_ref[0])

        @pl.when(f == 0)
        def _():
            yb_ref[...] = part

        @pl.when(f > 0)
        def _():
            yb_ref[...] = yb_ref[...] + part

    @pl.when((j >= nu_ref[0]) & (f == 0))
    def _():
        yb_ref[...] = jnp.zeros_like(yb_ref)


def _experts(block_e, n_used, xb, wg, wu, wd, tf):
    n_rows, d = xb.shape
    ff = wg.shape[2]
    nf = ff // tf
    n_blocks = n_rows // MOE_ROWS

    def blk(j, f, be, nu):
        return (jnp.minimum(j, nu[0] - 1), 0)

    def fcol(j, f, be, nu):
        return jnp.where(j < nu[0], f, nf - 1)

    return pl.pallas_call(
        _expert_kernel,
        grid_spec=pltpu.PrefetchScalarGridSpec(
            num_scalar_prefetch=2,
            grid=(n_blocks, nf),
            in_specs=[pl.BlockSpec((MOE_ROWS, d), blk),
                      pl.BlockSpec((1, d, tf), lambda j, f, be, nu: (be[j], 0, fcol(j, f, be, nu))),
                      pl.BlockSpec((1, d, tf), lambda j, f, be, nu: (be[j], 0, fcol(j, f, be, nu))),
                      pl.BlockSpec((1, tf, d), lambda j, f, be, nu: (be[j], fcol(j, f, be, nu), 0))],
            out_specs=pl.BlockSpec((MOE_ROWS, d), lambda j, f, be, nu: (j, 0))),
        out_shape=jax.ShapeDtypeStruct((n_rows, d), F32),
        compiler_params=_cparams(("arbitrary", "arbitrary"), 56),
        name="moe_experts",
    )(block_e, n_used, xb, wg, wu, wd)


def _combine_kernel(d1_ref, d2_ref, x_ref, gt_ref, p_ref, fg_ref, yb_ref, o_ref, y1_ref, y2_ref, sem, *,
                    tm, final, row_off):
    base = row_off + (pl.program_id(0) * pl.num_programs(1) + pl.program_id(1)) * tm

    def row_copy(src, buf_ref, j):
        return pltpu.make_async_copy(yb_ref.at[pl.ds(src, 1), :], buf_ref.at[pl.ds(j, 1), :], sem)

    def issue(j8, carry):
        for u in range(min(ROW_DMA_UNROLL, tm)):
            j = j8 * ROW_DMA_UNROLL + u
            row_copy(d1_ref[base + j], y1_ref, j).start(priority=0)
            row_copy(d2_ref[base + j], y2_ref, j).start(priority=1)
        return carry

    lax.fori_loop(0, tm // ROW_DMA_UNROLL, issue, 0)
    for buf_ref in (y1_ref, y2_ref):
        pltpu.make_async_copy(yb_ref.at[pl.ds(0, tm), :], buf_ref, sem).wait()

    p = p_ref[...]
    f = y1_ref[...] * p[:, 0:1] + y2_ref[...] * p[:, 1:2]
    x2 = x_ref[0] + _gated(f, gt_ref)
    o_ref[0] = _final_norm(x2, fg_ref) if final else x2


def _combine(d1, d2, x, mod, probs, fg, yb, final, tm, row_off):
    b, t, d = x.shape
    nt = t // tm
    blk_off = row_off // tm
    return pl.pallas_call(
        functools.partial(_combine_kernel, tm=tm, final=final, row_off=row_off),
        grid_spec=pltpu.PrefetchScalarGridSpec(
            num_scalar_prefetch=2,
            grid=(b, nt),
            in_specs=[pl.BlockSpec((1, tm, d), lambda i, j, *_: (i, j, 0)),
                      pl.BlockSpec((1, mod.shape[1], d), lambda i, j, *_: (i, 0, 5)),
                      pl.BlockSpec((tm, LANES), lambda i, j, *_: (blk_off + i * nt + j, 0)),
                      pl.BlockSpec((1, d), lambda i, j, *_: (0, 0)),
                      pl.BlockSpec(memory_space=pl.ANY)],
            out_specs=pl.BlockSpec((1, tm, d), lambda i, j, *_: (i, j, 0)),
            scratch_shapes=[pltpu.VMEM((tm, d), F32), pltpu.VMEM((tm, d), F32), pltpu.SemaphoreType.DMA]),
        out_shape=jax.ShapeDtypeStruct((b, t, d), F32),
        compiler_params=_cparams(("arbitrary", "arbitrary"), 24),
        name="moe_combine",
    )(d1, d2, x, mod, probs, fg, yb)


def _moe(h2, logits, trunks, wg, wu, wd, fg, final):
    n, d = h2.shape
    tile = lambda cap: max(r for r in (cap, cap // 2, cap // 4, cap // 8, cap // 16) if n % r == 0)
    info, probs, cnt = _route(logits, tile(256))
    counts = cnt[0, :N_EXPERTS]
    padded = (counts + MOE_ROWS - 1) // MOE_ROWS * MOE_ROWS
    ep = jnp.cumsum(padded).astype(jnp.int32)
    sp = ep - padded
    n_blocks = -(-(2 * n) // MOE_ROWS) + N_EXPERTS
    n_used = jnp.maximum(ep[-1] // MOE_ROWS, 1).astype(jnp.int32)
    blk_start = jnp.minimum(jnp.arange(n_blocks, dtype=jnp.int32), n_used - 1) * MOE_ROWS
    block_e = jnp.minimum(jnp.sum(blk_start[:, None] >= ep[None, :], axis=1), N_EXPERTS - 1).astype(jnp.int32)
    dest = _dest(info, jnp.pad(sp, (0, LANES - N_EXPERTS))[None, :], tile(1024))
    d1, d2 = dest[:, 0], dest[:, 1]
    xb = _dispatch(d1, d2, sp, ep, h2, n_blocks * MOE_ROWS, tile(256))
    yb = _experts(block_e, n_used.reshape(1), xb, wg, wu, wd, MOE_FF_TILE)
    return [_combine(d1, d2, x1, mod, probs, fg, yb, final, tm, off) for x1, mod, tm, off in trunks]


def _pad_rows_front(a, rows):
    pad = [(0, 0)] * a.ndim
    pad[-2] = (rows - a.shape[-2], 0)
    return jnp.pad(a, pad)


def _trunks(xs, mods, states, w):
    depth = w["wa"].shape[0]
    n_tr = len(xs)
    shapes = [x.shape for x in xs]
    d = shapes[0][2]
    offs = [sum(s[0] * s[1] for s in shapes[:i]) for i in range(n_tr)]
    n_all = sum(s[0] * s[1] for s in shapes)
    tms, mvs = [], []
    for (b, t, _), m in zip(shapes, mods):
        if t < TOKEN_TILE and TOKEN_TILE % t == 0 and (b * t) % TOKEN_TILE == 0:
            nseg = TOKEN_TILE // t
            tms.append(TOKEN_TILE)
            mvs.append(m.reshape(depth, b // nseg, nseg, m.shape[-1]))
        else:
            tms.append(min(t, TOKEN_TILE))
            mvs.append(m)
    tok = lambda a, i: a.reshape(-1, tms[i] if shapes[i][1] < tms[i] else shapes[i][1], a.shape[-1])
    seq = lambda a, i: a.reshape(shapes[i][0], shapes[i][1], a.shape[-1])
    xs = [tok(x, i) for i, x in enumerate(xs)]
    new_states = [([], [], [], []) for _ in xs]
    for l in range(depth):
        final = l == depth - 1
        j = l // 2
        x1s, flat = [], None
        for i, x in enumerate(xs):
            t = shapes[i][1]
            ml = mvs[i][l]
            conv0, gdn0, shift0, rwkv0 = states[i]
            pa, pab, pb = (seq(a, i) for a in _norm_proj(x, ml, w["norm1_g"][l], w["wa"][l], w["wab"][l],
                                                         w["wb"][l], tms[i]))
            oa, sg = _gdn(pa, pab, _pad_rows_front(conv0[l], SUBLANES), gdn0[l], w["conv_w"][l], w["alog"][l],
                          w["dtb"][l], w["onorm_g"][l], min(t, GDN_TILE))
            ob, sr = _rwkv(pb, _pad_rows_front(shift0[l], SUBLANES), rwkv0[l], [v[l] for v in w["rwkv_vecs"]],
                           w["w2p"][l], w["a2p"][l], w["g2"][l], w["bd"], min(t, RWKV_TILE))
            for lst, val in zip(new_states[i], (pa[:, t - (GDN_CONV - 1):, :w["conv_w"].shape[2]], sg,
                                                pb[:, t - 1:, :], sr)):
                lst.append(val)
            oa, ob = tok(oa, i), tok(ob, i)
            if l % 2 == 0:
                x1, h2 = _out_proj(oa, ob, x, ml, w["norm2_g"][l], w["woa"][l], w["wob"][l], tms[i])
                xs[i] = _ffn(h2, x1, ml, w["ffn_g"][j], w["ffn_u"][j], w["ffn_d"][j], w["final_g"], final, tms[i])
            else:
                x1, *flat = _out_proj(oa, ob, x, ml, w["norm2_g"][l], w["woa"][l], w["wob"][l], tms[i],
                                      router=w["router"][j], flat_rows=n_all, row_off=offs[i], flat_prev=flat)
                x1s.append(x1)
        if l % 2 == 1:
            trunks = [(x1s[i], mvs[i][l], tms[i], offs[i]) for i in range(n_tr)]
            xs = _moe(flat[0], flat[1], trunks, w["moe_g"][j], w["moe_u"][j], w["moe_d"][j], w["final_g"], final)
    return [seq(x, i) for i, x in enumerate(xs)], [tuple(jnp.stack(s) for s in st) for st in new_states]


def kernel(x_prompt, x_sample, c_prompt, c_sample, state_gdn_conv, state_gdn, state_rwkv_shift, state_rwkv, w_ada, b_ada, norm1_g, norm2_g, w_in, gdn_conv_w, gdn_a_log, gdn_dt_bias, gdn_onorm_g, rwkv_mu, rwkv_w0, rwkv_w2, rwkv_a0, rwkv_a2, rwkv_g2, rwkv_k_k, rwkv_k_a, rwkv_r_k, rwkv_ln_w, rwkv_ln_b, w_out, ffn_w_gate, ffn_w_up, ffn_w_down, moe_router, moe_w_gate, moe_w_up, moe_w_down, final_g):
    depth, d, _ = w_in.shape
    nbp = x_prompt.shape[0]
    nbs = x_sample.shape[0]
    nqkvz = 4 * GDN_HEADS * GDN_D
    nab = 2 * GDN_HEADS
    nw = RWKV_HEADS * RWKV_N
    lora_w = rwkv_w2.shape[1]

    def lane_pad(v):
        return jnp.pad(v, ((0, 0), (0, LANES - v.shape[1])))[:, None, :]

    rows = lambda v: v[:, None, :]
    hi = jnp.arange(nw) // RWKV_N
    w = dict(
        norm1_g=rows(norm1_g), norm2_g=rows(norm2_g), final_g=final_g[None, :],
        wa=w_in[:, :, :nqkvz].astype(BF16),
        wab=jnp.pad(w_in[:, :, nqkvz:nqkvz + nab], ((0, 0), (0, 0), (0, LANES - nab))).astype(BF16),
        wb=w_in[:, :, nqkvz + nab:].astype(BF16),
        conv_w=gdn_conv_w, alog=lane_pad(gdn_a_log), dtb=lane_pad(gdn_dt_bias), onorm_g=rows(gdn_onorm_g),
        rwkv_vecs=[rows(rwkv_mu), rows(rwkv_w0), rows(rwkv_a0), rows(rwkv_k_k), rows(rwkv_k_a),
                   rwkv_r_k.reshape(depth, 1, nw), rows(rwkv_ln_w), rows(rwkv_ln_b)],
        w2p=jnp.pad(rwkv_w2, ((0, 0), (0, LANES - lora_w), (0, 0))),
        a2p=jnp.pad(rwkv_a2, ((0, 0), (lora_w, LANES - lora_w - rwkv_a2.shape[1]), (0, 0))),
        g2=rwkv_g2,
        bd=(hi[:, None] == hi[None, :]).astype(F32),
        woa=w_out[:, :GDN_HEADS * GDN_D, :].astype(BF16), wob=w_out[:, GDN_HEADS * GDN_D:, :].astype(BF16),
        ffn_g=ffn_w_gate.astype(BF16), ffn_u=ffn_w_up.astype(BF16), ffn_d=ffn_w_down.astype(BF16),
        router=jnp.pad(moe_router, ((0, 0), (0, 0), (0, LANES - moe_router.shape[2]))),
        moe_g=moe_w_gate.astype(BF16), moe_u=moe_w_up.astype(BF16), moe_d=moe_w_down.astype(BF16),
    )

    nb_all = nbp + nbs
    bp = -(-nb_all // SUBLANES) * SUBLANES
    c_all = jnp.pad(jnp.concatenate([c_prompt, c_sample], axis=0), ((0, bp - nb_all), (0, 0)))
    mod = _ada_mod(c_all, w_ada, b_ada)[:, :, None, :]
    mod_p, mod_s = mod[:, :nbp], mod[:, nbp:nb_all]

    dt = x_prompt.dtype
    zc = jnp.zeros((depth, nbp) + state_gdn_conv.shape[2:], dt)
    zg = jnp.zeros((depth, nbp) + state_gdn.shape[2:], dt)
    zs = jnp.zeros((depth, nbp) + state_rwkv_shift.shape[2:], dt)
    zr = jnp.zeros((depth, nbp) + state_rwkv.shape[2:], dt)
    (y_p, y_s), (st_p, st_s) = _trunks(
        [x_prompt, x_sample], [mod_p, mod_s],
        [(zc, zg, zs, zr), (state_gdn_conv, state_gdn, state_rwkv_shift, state_rwkv)], w)
    return (y_p, y_s) + st_p + st_s
```

```python
import functools

import jax
import jax.numpy as jnp
from jax import lax
from jax.experimental import pallas as pl
from jax.experimental.pallas import tpu as pltpu

F32 = jnp.float32
BF16 = jnp.bfloat16
HI = lax.Precision.HIGHEST

LANES = 128
SUBLANES = 8
CHUNK = 64
GDN_HEADS = 4
GDN_D = 128
RWKV_HEADS = 8
RWKV_N = 64
GDN_CONV = 4
N_EXPERTS = 8
NORM_EPS = 1e-6
L2_EPS = 1e-6
GN_EPS = 64e-5
NEG_BIG = -1e30
RWKV_DECAY_SCALE = 0.6065306597126334
TOKEN_TILE = 512
GDN_TILE = 512
RWKV_TILE = 256
SHORT_SEQ_BATCH = 4
MOE_ROWS = 512
MOE_FF_TILE = 1792
MIB = 1024 * 1024

P_GDN_SCORE = 1
P_GDN_INV = 3
P_GDN_SOLVE = 1
P_GDN_STATE = 1
P_RWKV_SMALL = 1
P_RWKV_INV = 1
P_RWKV_STATE = 1
P_ROUTER = 3


def _cparams(sem, vmem_mib):
    return pltpu.CompilerParams(dimension_semantics=sem, vmem_limit_bytes=vmem_mib * MIB)


def _split_bf16(x):
    hi = x.astype(BF16)
    return hi, (x - hi.astype(F32)).astype(BF16)


def _split3_bf16(x):
    hi = x.astype(BF16)
    r1 = x - hi.astype(F32)
    mid = r1.astype(BF16)
    return hi, mid, (r1 - mid.astype(F32)).astype(BF16)


_CONTRACT = {"nn": (1, 0), "nt": (1, 1), "tn": (0, 0)}


def _dg(a, b, kind, prec):
    off = a.ndim - 2
    ca, cb = _CONTRACT[kind]
    dn = (((ca + off,), (cb + off,)), (((0,), (0,)) if off else ((), ())))
    dot = lambda x, y: lax.dot_general(x, y, dn, preferred_element_type=F32)
    if prec is None or prec is HI:
        return lax.dot_general(a, b, dn, preferred_element_type=F32, precision=prec)
    if prec == 1:
        return dot(a.astype(BF16), b.astype(BF16))
    if prec == "l3":
        bb = b.astype(BF16)
        a0, a1, a2 = _split3_bf16(a)
        return dot(a0, bb) + dot(a1, bb) + dot(a2, bb)
    if prec == "r3":
        ab = a.astype(BF16)
        b0, b1, b2 = _split3_bf16(b)
        return dot(ab, b0) + dot(ab, b1) + dot(ab, b2)
    ah, al = _split_bf16(a)
    bh, bl = _split_bf16(b)
    return dot(ah, bh) + dot(ah, bl) + dot(al, bh)


def _mm(a, b, prec=None):
    return _dg(a, b, "nn", prec)


def _mm_nt(a, b, prec=None):
    return _dg(a, b, "nt", prec)


def _mm_tn(a, b, prec=None):
    return _dg(a, b, "tn", prec)


def _silu(x):
    return x * jax.nn.sigmoid(x)


def _softplus(x):
    return jnp.maximum(x, 0.0) + jnp.log1p(jnp.exp(-jnp.abs(x)))


def _iota2(shape, dim):
    return lax.broadcasted_iota(jnp.int32, shape, dim)


def _per_segment(x, fn, *mod_refs):
    mods = [r[0] for r in mod_refs]
    nseg = mods[0].shape[0]
    if nseg == 1:
        return fn(x, *mods)
    rows = x.shape[0] // nseg
    return jnp.concatenate([fn(x[s * rows:(s + 1) * rows], *(m[s:s + 1] for m in mods)) for s in range(nseg)], axis=0)


def _scale_shift(xn, sc_ref, sh_ref):
    return _per_segment(xn, lambda x, sc, sh: x * (1.0 + sc) + sh, sc_ref, sh_ref)


def _gated(f, gt_ref):
    return _per_segment(f, lambda x, gt: gt * x, gt_ref)


def _shift_rows(x, prev8, k):
    head = jnp.where(_iota2(prev8.shape, 0) < k, pltpu.roll(prev8, k, 0), pltpu.roll(x[0:SUBLANES], k, 0))
    return jnp.concatenate([head, pltpu.roll(x, k, 0)[SUBLANES:]], axis=0)


def _causal_taps(x, w, prev8, ntap):
    y = x * w[ntap - 1:ntap, :]
    for k in range(1, ntap):
        y = y + _shift_rows(x, prev8, k) * w[ntap - 1 - k:ntap - k, :]
    return y


def _neumann_inverse(x, eye, prec):
    c = x.shape[-1]
    z = jnp.concatenate([x, jnp.broadcast_to(eye, x.shape)], axis=-1)
    keep_s = _iota2((c, 2 * c), 1) >= c
    for _ in range(6):
        z = _mm(z[..., :c], z, prec) + jnp.where(keep_s, z, 0.0)
    return z[..., c:]


def _ada_kernel(c_ref, w_ref, b_ref, o_ref):
    o_ref[0] = _mm(_silu(c_ref[...]), w_ref[0], HI) + b_ref[0]


def _ada_mod(c_all, w_ada, b_ada):
    depth, d, n6 = w_ada.shape
    bp = c_all.shape[0]
    tn = 1536
    return pl.pallas_call(
        _ada_kernel,
        grid=(depth, n6 // tn),
        in_specs=[pl.BlockSpec((bp, d), lambda l, j: (0, 0)),
                  pl.BlockSpec((1, d, tn), lambda l, j: (l, 0, j)),
                  pl.BlockSpec((1, 1, tn), lambda l, j: (l, 0, j))],
        out_specs=pl.BlockSpec((1, bp, tn), lambda l, j: (l, 0, j)),
        out_shape=jax.ShapeDtypeStruct((depth, bp, n6), F32),
        compiler_params=_cparams(("arbitrary", "arbitrary"), 40),
        name="ada_mod",
    )(c_all, w_ada, b_ada.reshape(depth, 1, n6))


def _norm_proj_kernel(x_ref, sh_ref, sc_ref, g_ref, wa_ref, wab_ref, wb_ref, pa_ref, pab_ref, pb_ref):
    x = x_ref[0]
    rs = lax.rsqrt(jnp.mean(x * x, axis=-1, keepdims=True) + NORM_EPS)
    hb = _scale_shift(x * rs * g_ref[...], sc_ref, sh_ref).astype(BF16)
    pa_ref[0] = _mm(hb, wa_ref[...])
    pab_ref[0] = _mm(hb, wab_ref[...])
    pb_ref[0] = _mm(hb, wb_ref[...])


def _norm_proj(x, mod, g, wa, wab, wb, tm):
    b, t, d = x.shape
    na, nab, nb = wa.shape[1], wab.shape[1], wb.shape[1]
    nseg = mod.shape[1]
    row = lambda i, j: (i, j, 0)
    const = lambda i, j: (0, 0)
    return pl.pallas_call(
        _norm_proj_kernel,
        grid=(b, t // tm),
        in_specs=[pl.BlockSpec((1, tm, d), row),
                  pl.BlockSpec((1, nseg, d), lambda i, j: (i, 0, 0)),
                  pl.BlockSpec((1, nseg, d), lambda i, j: (i, 0, 1)),
                  pl.BlockSpec((1, d), const),
                  pl.BlockSpec((d, na), const),
                  pl.BlockSpec((d, nab), const),
                  pl.BlockSpec((d, nb), const)],
        out_specs=[pl.BlockSpec((1, tm, na), row), pl.BlockSpec((1, tm, nab), row), pl.BlockSpec((1, tm, nb), row)],
        out_shape=[jax.ShapeDtypeStruct((b, t, na), F32), jax.ShapeDtypeStruct((b, t, nab), F32),
                   jax.ShapeDtypeStruct((b, t, nb), F32)],
        compiler_params=_cparams(("arbitrary", "arbitrary"), 48),
        name="norm_proj",
    )(x, mod, mod, g, wa, wab, wb)


def _gdn_kernel(pa_ref, pab_ref, cbuf_ref, s0_ref, cw_ref, alog_ref, dtb_ref, og_ref,
                o_ref, sfin_ref, xc_ref, s_ref, *, tc):
    t = pl.program_id(1)
    nqk = GDN_HEADS * GDN_D
    nconv = 3 * nqk

    nsq = pa_ref.shape[0]
    nh = GDN_HEADS

    @pl.when(t == 0)
    def _():
        xc_ref[...] = cbuf_ref[...]
        s_ref[...] = s0_ref[...].reshape(s_ref.shape)

    ys = []
    for sq in range(nsq):
        x = pa_ref[sq, :, 0:nconv]
        ys.append(_causal_taps(x, cw_ref[...], xc_ref[sq], GDN_CONV))
        xc_ref[sq] = x[tc - SUBLANES:tc, :]
    qkv = _silu(ys[0] if nsq == 1 else jnp.concatenate(ys, axis=0))

    ab = pab_ref[...].reshape(nsq * tc, LANES)
    gmat = -jnp.exp(alog_ref[...]) * _softplus(ab + dtb_ref[...])
    bmat = jax.nn.sigmoid(ab)

    c = CHUNK
    nch = nsq * tc // c
    nb = nch * nh
    ri = _iota2((c, c), 0)
    ci = _iota2((c, c), 1)
    incl = ri >= ci
    strict = ri > ci
    eye_c = (ri == ci).astype(F32)

    def heads(x, width):
        x = x.reshape(nch, c, nh * width)
        return jnp.stack([x[:, :, h * width:(h + 1) * width] for h in range(nh)], axis=1).reshape(nb, c, width)

    q = heads(qkv[:, 0:nqk], GDN_D)
    k = heads(qkv[:, nqk:2 * nqk], GDN_D)
    v = heads(qkv[:, 2 * nqk:3 * nqk], GDN_D)
    ones_d = jnp.ones((GDN_D, GDN_D), BF16)

    def l2n(x, scale):
        ss = _mm((x * x).reshape(nb * c, GDN_D), ones_d, 1).reshape(nb, c, GDN_D)
        return x * (lax.rsqrt(ss + L2_EPS) * scale)

    q = l2n(q, GDN_D ** -0.5)
    k = l2n(k, 1.0)
    beta = heads(bmat[:, nh:2 * nh], 1)

    g4 = gmat.reshape(nch, c, LANES)
    gcol4 = _mm(jnp.broadcast_to(incl.astype(F32), (nch, c, c)), g4, "r3")
    grow4 = _mm_tn(g4, jnp.broadcast_to((ri <= ci).astype(F32), (nch, c, c)), "l3")
    gc = jnp.stack([gcol4[:, :, h:h + 1] for h in range(nh)], axis=1).reshape(nb, c, 1)
    gr = jnp.stack([grow4[:, h:h + 1, :] for h in range(nh)], axis=1).reshape(nb, 1, c)
    decay = jnp.exp(jnp.where(incl, gc - gr, NEG_BIG))

    kb = k * beta
    m = jnp.where(strict, _mm_nt(kb, k, P_GDN_SCORE) * decay, 0.0)
    t0 = _neumann_inverse(-m, eye_c, 1)
    tinv = t0 + _mm(t0, (eye_c - t0) - _mm(m, t0, P_GDN_INV), 1)
    egc = jnp.exp(gc)
    sol = _mm(tinv, jnp.concatenate([v * beta, kb * egc], axis=2), P_GDN_SOLVE)
    u, w = sol[:, :, :GDN_D], sol[:, :, GDN_D:]
    aqk = _mm_nt(q, k, P_GDN_SCORE) * decay
    gl = gc[:, c - 1:c, :]
    kd = k * jnp.exp(gl - gc)
    qg = q * egc
    egl = jnp.exp(gl)

    s = s_ref[...]
    outs = []
    for ch in range(tc // c):
        bs = slice(ch * nh, (ch + 1) * nh) if nsq == 1 else slice(None)
        v_new = u[bs] - _mm(w[bs], s, P_GDN_STATE)
        outs.append(_mm(qg[bs], s, P_GDN_STATE) + _mm(aqk[bs], v_new, P_GDN_STATE))
        s = s * egl[bs] + _mm_tn(kd[bs], v_new, P_GDN_STATE)
    s_ref[...] = s

    o = jnp.stack(outs, axis=0).reshape(nch, nh, c, GDN_D)
    o = o * lax.rsqrt(jnp.mean(o * o, axis=-1, keepdims=True) + NORM_EPS) * og_ref[...]
    for h in range(nh):
        sl = slice(h * GDN_D, (h + 1) * GDN_D)
        z = pa_ref[:, :, nconv + h * GDN_D: nconv + (h + 1) * GDN_D]
        o_ref[:, :, sl] = (o[:, h].reshape(nsq, tc, GDN_D) * _silu(z)).astype(o_ref.dtype)

    @pl.when(t == pl.num_programs(1) - 1)
    def _():
        sfin_ref[...] = s_ref[...].reshape(sfin_ref.shape)


def _seqs_per_step(b, t, tc):
    return max(n for n in (SHORT_SEQ_BATCH, 2, 1) if b % n == 0) if t == tc == CHUNK else 1


def _gdn(pa, pab, cbuf8, s0, cw, alog, dtb, og, tc):
    b, t, na = pa.shape
    nconv = cw.shape[1]
    nsq = _seqs_per_step(b, t, tc)
    row = lambda i, j: (i, j, 0)
    const2 = lambda i, j: (0, 0)
    return pl.pallas_call(
        functools.partial(_gdn_kernel, tc=tc),
        grid=(b // nsq, t // tc),
        in_specs=[pl.BlockSpec((nsq, tc, na), row),
                  pl.BlockSpec((nsq, tc, LANES), row),
                  pl.BlockSpec((nsq, SUBLANES, nconv), lambda i, j: (i, 0, 0)),
                  pl.BlockSpec((nsq, GDN_HEADS, GDN_D, GDN_D), lambda i, j: (i, 0, 0, 0)),
                  pl.BlockSpec((GDN_CONV, nconv), const2),
                  pl.BlockSpec((1, LANES), const2),
                  pl.BlockSpec((1, LANES), const2),
                  pl.BlockSpec((1, GDN_D), const2)],
        out_specs=[pl.BlockSpec((nsq, tc, GDN_HEADS * GDN_D), row),
                   pl.BlockSpec((nsq, GDN_HEADS, GDN_D, GDN_D), lambda i, j: (i, 0, 0, 0))],
        out_shape=[jax.ShapeDtypeStruct((b, t, GDN_HEADS * GDN_D), BF16),
                   jax.ShapeDtypeStruct((b, GDN_HEADS, GDN_D, GDN_D), F32)],
        scratch_shapes=[pltpu.VMEM((nsq, SUBLANES, nconv), F32),
                        pltpu.VMEM((nsq * GDN_HEADS, GDN_D, GDN_D), F32)],
        compiler_params=_cparams(("arbitrary", "arbitrary"), 40),
        name="gdn_mixer",
    )(pa, pab, cbuf8, s0, cw, alog, dtb, og)


def _rwkv_kernel(pb_ref, sbuf_ref, s0_ref, mu_ref, w0_ref, w2_ref, a0_ref, a2_ref, g2_ref, kk_ref, ka_ref,
                 rk_ref, lnw_ref, lnb_ref, bd_ref, o_ref, sfin_ref, xc_ref, s_ref, y_ref, *, tc):
    t = pl.program_id(1)
    nw = RWKV_HEADS * RWKV_N

    nsq = pb_ref.shape[0]
    nh = RWKV_HEADS

    @pl.when(t == 0)
    def _():
        xc_ref[...] = sbuf_ref[...]
        s_ref[...] = s0_ref[...].reshape(s_ref.shape)

    xl, pl_ = [], []
    for sq in range(nsq):
        xq = pb_ref[sq]
        xl.append(xq)
        pl_.append(_shift_rows(xq, xc_ref[sq], 1))
        xc_ref[sq] = xq[tc - SUBLANES:tc, :]
    x = xl[0] if nsq == 1 else jnp.concatenate(xl, axis=0)
    prev = pl_[0] if nsq == 1 else jnp.concatenate(pl_, axis=0)
    xs = x + (prev - x) * mu_ref[...]
    r = xs[:, 0:nw]
    kr = xs[:, nw:2 * nw]
    vr = xs[:, 2 * nw:3 * nw]
    wa = xs[:, 3 * nw:3 * nw + LANES]
    gd = xs[:, 3 * nw + LANES:3 * nw + 2 * LANES]

    bd = bd_ref[...]
    ps = P_RWKV_SMALL
    lw = -RWKV_DECAY_SCALE * jax.nn.sigmoid(w0_ref[...] + _mm(jnp.tanh(wa), w2_ref[...], ps))
    a = jax.nn.sigmoid(a0_ref[...] + _mm(wa, a2_ref[...], ps))
    gate = _mm(jax.nn.sigmoid(gd), g2_ref[...], ps)
    kkr = kr * kk_ref[...]
    kk = kkr * lax.rsqrt(_mm(kkr * kkr, bd, ps) + L2_EPS)
    kb = kr * (1.0 + (a - 1.0) * ka_ref[...])
    aa = -kk
    bb = kk * a
    bonus = _mm(r * kb * rk_ref[...], bd, ps) * vr

    c = CHUNK
    n = RWKV_N
    nch = nsq * tc // c
    nb = nch * nh
    ri = _iota2((c, c), 0)
    ci = _iota2((c, c), 1)
    incl = ri >= ci
    eye = ri == ci
    eye_c = eye.astype(F32)

    def heads(x):
        return jnp.stack([x[:, :, h * n:(h + 1) * n] for h in range(nh)], axis=1).reshape(nb, x.shape[1], n)

    chunks = lambda x: x.reshape(nch, c, nw)
    lw4 = chunks(lw)
    g = _mm(jnp.broadcast_to(incl.astype(F32), (nch, c, c)), lw4, "r3")
    gfin = g[:, c - 1:c, :]
    eng = jnp.exp(-g)
    efin = jnp.exp(gfin)
    tail = efin * eng
    aa4, bb4, kb4 = chunks(aa), chunks(bb), chunks(kb)
    at = heads(aa4 * jnp.exp(g - lw4))
    bt = heads(bb4 * eng)
    kt = heads(kb4 * eng)
    rt = heads(chunks(r) * jnp.exp(g))
    bh = heads(bb4 * tail)
    kh = heads(kb4 * tail)
    egf = heads(efin)
    v4 = chunks(vr)
    upper = _iota2((c, 2 * n), 1) >= n
    vcols = [v4[:, :, (h // 2) * 2 * n:(h // 2 + 1) * 2 * n] for h in range(nh)]
    vhi = jnp.stack([jnp.where(upper, col if h % 2 else pltpu.roll(col, n, 2), 0.0)
                     for h, col in enumerate(vcols)], axis=1).reshape(nb, c, 2 * n)
    zeros_lo = jnp.zeros((nb, c, n), F32)
    pad_hi = lambda x: jnp.concatenate([x, zeros_lo], axis=-1)

    h16 = lambda x: x.astype(BF16)
    colj = _iota2((c, 2 * c), 1) % c
    rowi = _iota2((c, 2 * c), 0)
    vhi16 = h16(vhi)
    ar = jnp.concatenate([h16(at), h16(rt)], axis=1)
    xbk = _mm_nt(ar, jnp.concatenate([h16(bt), h16(kt)], axis=1))
    l2 = jnp.where(rowi > colj, xbk[:, :c], 0.0)
    m2 = jnp.where(rowi >= colj, xbk[:, c:], 0.0)
    tinv = _neumann_inverse(l2[..., :c], eye_c, P_RWKV_INV)
    lv = _mm(h16(l2), jnp.concatenate([jnp.zeros_like(vhi16), vhi16], axis=1))
    wu = _mm(h16(tinv), h16(lv + pad_hi(at)))
    wv = jnp.concatenate([h16(wu), vhi16], axis=1)
    qy = _mm(h16(m2), wv) + pad_hi(rt)
    pi = _mm_tn(wv, jnp.concatenate([h16(bh), h16(kh)], axis=1))
    pt = jnp.where(eye, egf, 0.0) + pi[:, :n]
    inc = pi[:, n:]

    s = s_ref[...]
    zeros_s = jnp.zeros_like(s)
    ys = []
    for ch in range(tc // c):
        bs = slice(ch * nh, (ch + 1) * nh) if nsq == 1 else slice(None)
        ys.append(_mm_nt(qy[bs][..., :n], jnp.concatenate([zeros_s, s], axis=1), P_RWKV_STATE) + qy[bs])
        s = _mm(s, pt[bs], P_RWKV_STATE) + inc[bs]
    s_ref[...] = s
    y4 = jnp.stack(ys, axis=0).reshape(nch, nh, c, 2 * n)
    for h in range(nh):
        y_ref[:, h * n:(h + 1) * n] = y4[:, h, :, n:].reshape(nsq * tc, n)

    y = y_ref[...]
    mean = _mm(y, bd, ps) * (1.0 / n)
    dlt = y - mean
    var = _mm(dlt * dlt, bd, ps) * (1.0 / n)
    yn = dlt * lax.rsqrt(var + GN_EPS) * lnw_ref[...] + lnb_ref[...]
    o_ref[...] = ((yn + bonus) * gate).reshape(o_ref.shape).astype(o_ref.dtype)

    @pl.when(t == pl.num_programs(1) - 1)
    def _():
        sfin_ref[...] = s_ref[...].reshape(sfin_ref.shape)


def _rwkv(pb, sbuf8, s0, vecs, w2p, a2p, g2, bd, tc):
    b, t, nb = pb.shape
    nw = RWKV_HEADS * RWKV_N
    nsq = _seqs_per_step(b, t, tc)
    row = lambda i, j: (i, j, 0)
    const2 = lambda i, j: (0, 0)
    mu, w0, a0, kk, ka, rk, lnw, lnb = vecs
    vec_spec = pl.BlockSpec((1, nw), const2)
    return pl.pallas_call(
        functools.partial(_rwkv_kernel, tc=tc),
        grid=(b // nsq, t // tc),
        in_specs=[pl.BlockSpec((nsq, tc, nb), row),
                  pl.BlockSpec((nsq, SUBLANES, nb), lambda i, j: (i, 0, 0)),
                  pl.BlockSpec((nsq, RWKV_HEADS, RWKV_N, RWKV_N), lambda i, j: (i, 0, 0, 0)),
                  pl.BlockSpec((1, nb), const2),
                  vec_spec,
                  pl.BlockSpec((LANES, nw), const2),
                  vec_spec,
                  pl.BlockSpec((LANES, nw), const2),
                  pl.BlockSpec((LANES, nw), const2),
                  vec_spec, vec_spec, vec_spec, vec_spec, vec_spec,
                  pl.BlockSpec((nw, nw), const2)],
        out_specs=[pl.BlockSpec((nsq, tc, nw), row),
                   pl.BlockSpec((nsq, RWKV_HEADS, RWKV_N, RWKV_N), lambda i, j: (i, 0, 0, 0))],
        out_shape=[jax.ShapeDtypeStruct((b, t, nw), BF16),
                   jax.ShapeDtypeStruct((b, RWKV_HEADS, RWKV_N, RWKV_N), F32)],
        scratch_shapes=[pltpu.VMEM((nsq, SUBLANES, nb), F32),
                        pltpu.VMEM((nsq * RWKV_HEADS, RWKV_N, RWKV_N), F32),
                        pltpu.VMEM((nsq * tc, nw), F32)],
        compiler_params=_cparams(("arbitrary", "arbitrary"), 40),
        name="rwkv_mixer",
    )(pb, sbuf8, s0, mu, w0, w2p, a0, a2p, g2, kk, ka, rk, lnw, lnb, bd)


def _out_proj_kernel(oa_ref, ob_ref, x_ref, gt_ref, sh_ref, sc_ref, g_ref, woa_ref, wob_ref, *rest, with_router, nt):
    if with_router:
        wr_ref = rest[0]
        x1_ref, h_ref, lg_ref = rest[-3:]
    else:
        x1_ref, h_ref = rest

    @pl.when(pl.program_id(1) < nt)
    def _():
        mix = _mm(oa_ref[0], woa_ref[...]) + _mm(ob_ref[0], wob_ref[...])
        x1 = x_ref[0] + _gated(mix, gt_ref)
        x1_ref[0] = x1
        rs = lax.rsqrt(jnp.mean(x1 * x1, axis=-1, keepdims=True) + NORM_EPS)
        h = _scale_shift(x1 * rs * g_ref[...], sc_ref, sh_ref)
        if with_router:
            h_ref[...] = h
            lg_ref[...] = _mm(h, wr_ref[...], P_ROUTER)
        else:
            h_ref[0] = h.astype(h_ref.dtype)

    if with_router:
        @pl.when(pl.program_id(1) >= nt)
        def _():
            h_ref[...] = jnp.zeros_like(h_ref)
            lg_ref[...] = jnp.zeros_like(lg_ref)


def _out_proj(oa, ob, x, mod, g, woa, wob, tm, router=None, flat_rows=None, row_off=0, flat_prev=None):
    b, t, d = x.shape
    nh = oa.shape[2]
    nt = t // tm
    nseg = mod.shape[1]
    tail = 0
    if router is not None and flat_prev is None:
        assert b == 1 and row_off == 0 and (flat_rows - t) % tm == 0
        tail = (flat_rows - t) // tm
    row = lambda i, j: (i, jnp.minimum(j, nt - 1), 0)
    const = lambda i, j: (0, 0)
    in_specs = [pl.BlockSpec((1, tm, nh), row), pl.BlockSpec((1, tm, nh), row), pl.BlockSpec((1, tm, d), row),
                pl.BlockSpec((1, nseg, d), lambda i, j: (i, 0, 2)),
                pl.BlockSpec((1, nseg, d), lambda i, j: (i, 0, 3)),
                pl.BlockSpec((1, nseg, d), lambda i, j: (i, 0, 4)),
                pl.BlockSpec((1, d), const), pl.BlockSpec((nh, d), const), pl.BlockSpec((nh, d), const)]
    args = [oa, ob, x, mod, mod, mod, g, woa, wob]
    aliases = {}
    if router is None:
        out_specs = [pl.BlockSpec((1, tm, d), row), pl.BlockSpec((1, tm, d), row)]
        out_shape = [jax.ShapeDtypeStruct((b, t, d), F32), jax.ShapeDtypeStruct((b, t, d), BF16)]
    else:
        blk_off = row_off // tm
        flat = lambda i, j: (blk_off + i * nt + j, 0)
        in_specs.append(pl.BlockSpec((d, LANES), const))
        args.append(router)
        if flat_prev is not None:
            aliases = {len(args): 1, len(args) + 1: 2}
            in_specs += [pl.BlockSpec(memory_space=pl.ANY)] * 2
            args += list(flat_prev)
        out_specs = [pl.BlockSpec((1, tm, d), row), pl.BlockSpec((tm, d), flat), pl.BlockSpec((tm, LANES), flat)]
        out_shape = [jax.ShapeDtypeStruct((b, t, d), F32), jax.ShapeDtypeStruct((flat_rows, d), F32),
                     jax.ShapeDtypeStruct((flat_rows, LANES), F32)]
    return pl.pallas_call(
        functools.partial(_out_proj_kernel, with_router=router is not None, nt=nt),
        grid=(b, nt + tail),
        in_specs=in_specs, out_specs=out_specs, out_shape=out_shape, input_output_aliases=aliases,
        compiler_params=_cparams(("arbitrary", "arbitrary"), 40),
        name="out_proj" if router is None else "out_proj_router",
    )(*args)


def _final_norm(x2, fg_ref):
    rs = lax.rsqrt(jnp.mean(x2 * x2, axis=-1, keepdims=True) + NORM_EPS)
    return x2 * rs * fg_ref[...]


def _ffn_kernel(h_ref, x_ref, gt_ref, wg_ref, wu_ref, wd_ref, fg_ref, o_ref, *, nf, final):
    h = h_ref[0]
    ff = wg_ref.shape[1]
    tf = ff // nf
    acc = None
    for f in range(nf):
        cols = slice(f * tf, (f + 1) * tf)
        act = (_silu(_mm(h, wg_ref[:, cols])) * _mm(h, wu_ref[:, cols])).astype(BF16)
        part = _mm(act, wd_ref[cols, :])
        acc = part if acc is None else acc + part
    x2 = x_ref[0] + _gated(acc, gt_ref)
    o_ref[0] = _final_norm(x2, fg_ref) if final else x2


def _ffn(h, x, mod, wg, wu, wd, fg, final, tm):
    b, t, d = x.shape
    ff = wg.shape[1]
    row = lambda i, j: (i, j, 0)
    const = lambda i, j: (0, 0)
    once = pl.Buffered(1)
    return pl.pallas_call(
        functools.partial(_ffn_kernel, nf=1, final=final),
        grid=(b, t // tm),
        in_specs=[pl.BlockSpec((1, tm, d), row), pl.BlockSpec((1, tm, d), row),
                  pl.BlockSpec((1, mod.shape[1], d), lambda i, j: (i, 0, 5)),
                  pl.BlockSpec((d, ff), const, pipeline_mode=once),
                  pl.BlockSpec((d, ff), const, pipeline_mode=once),
                  pl.BlockSpec((ff, d), const, pipeline_mode=once),
                  pl.BlockSpec((1, d), const)],
        out_specs=pl.BlockSpec((1, tm, d), row),
        out_shape=jax.ShapeDtypeStruct((b, t, d), F32),
        compiler_params=_cparams(("arbitrary", "arbitrary"), 56),
        name="ffn_dense",
    )(h, x, mod, wg, wu, wd, fg)


def _route_kernel(lg_ref, info_ref, p_ref, cnt_ref, carry_ref, *, tr):
    i = pl.program_id(0)

    @pl.when(i == 0)
    def _():
        carry_ref[...] = jnp.zeros_like(carry_ref)

    lane = _iota2((tr, LANES), 1)
    lg = jnp.where(lane < N_EXPERTS, lg_ref[...], NEG_BIG)
    m1 = jnp.max(lg, axis=-1, keepdims=True)
    e1 = jnp.min(jnp.where(lg == m1, lane, LANES), axis=-1, keepdims=True)
    lg2 = jnp.where(lane == e1, NEG_BIG, lg)
    m2 = jnp.max(lg2, axis=-1, keepdims=True)
    e2 = jnp.min(jnp.where(lg2 == m2, lane, LANES), axis=-1, keepdims=True)
    ex = jnp.exp(m2 - m1)
    den = 1.0 + ex
    p1 = 1.0 / den
    p2 = ex / den
    oh1 = lane == e1
    oh2 = lane == e2
    oh = jnp.where(oh1 | oh2, 1.0, 0.0)
    lstrict = (_iota2((tr, tr), 0) > _iota2((tr, tr), 1)).astype(BF16)
    cex = _mm(lstrict, oh.astype(BF16)) + carry_ref[...]
    rank1 = jnp.sum(jnp.where(oh1, cex, 0.0), axis=-1, keepdims=True).astype(jnp.int32)
    rank2 = jnp.sum(jnp.where(oh2, cex, 0.0), axis=-1, keepdims=True).astype(jnp.int32)
    carry_ref[...] = carry_ref[...] + jnp.sum(oh, axis=0, keepdims=True)
    info_ref[...] = jnp.where(lane == 0, e1, jnp.where(lane == 1, e2, jnp.where(lane == 2, rank1,
                              jnp.where(lane == 3, rank2, 0))))
    p_ref[...] = jnp.where(lane == 0, p1, jnp.where(lane == 1, p2, 0.0))

    @pl.when(i == pl.num_programs(0) - 1)
    def _():
        cnt_ref[...] = carry_ref[...].astype(jnp.int32)


def _route(logits, tr):
    n = logits.shape[0]
    blk = pl.BlockSpec((tr, LANES), lambda i: (i, 0))
    return pl.pallas_call(
        functools.partial(_route_kernel, tr=tr),
        grid=(n // tr,),
        in_specs=[blk],
        out_specs=[blk, blk, pl.BlockSpec((1, LANES), lambda i: (0, 0))],
        out_shape=[jax.ShapeDtypeStruct((n, LANES), jnp.int32), jax.ShapeDtypeStruct((n, LANES), F32),
                   jax.ShapeDtypeStruct((1, LANES), jnp.int32)],
        scratch_shapes=[pltpu.VMEM((1, LANES), F32)],
        compiler_params=_cparams(("arbitrary",), 24),
        name="moe_route",
    )(logits)


def _dest_kernel(info_ref, sp_ref, d_ref):
    info = info_ref[...]
    lane = _iota2(info.shape, 1)
    sp = sp_ref[...].astype(F32)
    start1 = jnp.sum(jnp.where(lane == info[:, 0:1], sp, 0.0), axis=-1, keepdims=True).astype(jnp.int32)
    start2 = jnp.sum(jnp.where(lane == info[:, 1:2], sp, 0.0), axis=-1, keepdims=True).astype(jnp.int32)
    d_ref[...] = jnp.where(lane == 0, start1 + info[:, 2:3], jnp.where(lane == 1, start2 + info[:, 3:4], 0))


def _dest(info, sp_row, tr):
    n = info.shape[0]
    blk = pl.BlockSpec((tr, LANES), lambda i: (i, 0))
    return pl.pallas_call(
        _dest_kernel,
        grid=(n // tr,),
        in_specs=[blk, pl.BlockSpec((1, LANES), lambda i: (0, 0))],
        out_specs=blk,
        out_shape=jax.ShapeDtypeStruct((n, LANES), jnp.int32),
        compiler_params=_cparams(("arbitrary",), 24),
        name="moe_dest",
    )(info, sp_row)


ROW_DMA_UNROLL = 8


def _dispatch_kernel(d1_ref, d2_ref, sp_ref, ep_ref, h_ref, xb_ref, zero_ref, sem, zsem, *, td):
    i = pl.program_id(0)

    @pl.when(i == 0)
    def _():
        zero_ref[...] = jnp.zeros_like(zero_ref)

        def zero_block(start):
            cp = pltpu.make_async_copy(zero_ref, xb_ref.at[pl.ds(pl.multiple_of(start, MOE_ROWS), MOE_ROWS), :], zsem)
            cp.start()
            cp.wait()

        for e in range(N_EXPERTS):
            @pl.when(ep_ref[e] > sp_ref[e])
            def _():
                zero_block(ep_ref[e] - MOE_ROWS)

        def tail(blk, carry):
            zero_block(blk * MOE_ROWS)
            return carry

        lax.fori_loop(ep_ref[N_EXPERTS - 1] // MOE_ROWS, xb_ref.shape[0] // MOE_ROWS, tail, 0)

    base = i * td

    def row_copy(j, dst):
        return pltpu.make_async_copy(h_ref.at[pl.ds(j, 1), :], xb_ref.at[pl.ds(dst, 1), :], sem)

    def issue(j8, carry):
        for u in range(ROW_DMA_UNROLL):
            j = j8 * ROW_DMA_UNROLL + u
            row_copy(j, d1_ref[base + j]).start(priority=0)
            row_copy(j, d2_ref[base + j]).start(priority=1)
        return carry

    lax.fori_loop(0, td // ROW_DMA_UNROLL, issue, 0)
    for _ in range(2):
        pltpu.make_async_copy(h_ref, xb_ref.at[pl.ds(0, td), :], sem).wait()


def _dispatch(d1, d2, sp, ep, h2, n_rows, td):
    n, d = h2.shape
    return pl.pallas_call(
        functools.partial(_dispatch_kernel, td=td),
        grid_spec=pltpu.PrefetchScalarGridSpec(
            num_scalar_prefetch=4,
            grid=(n // td,),
            in_specs=[pl.BlockSpec((td, d), lambda i, *_: (i, 0))],
            out_specs=pl.BlockSpec(memory_space=pl.ANY),
            scratch_shapes=[pltpu.VMEM((MOE_ROWS, d), F32), pltpu.SemaphoreType.DMA, pltpu.SemaphoreType.DMA]),
        out_shape=jax.ShapeDtypeStruct((n_rows, d), F32),
        compiler_params=_cparams(("arbitrary",), 24),
        name="moe_dispatch",
    )(d1, d2, sp, ep, h2)


def _expert_kernel(be_ref, nu_ref, xb_ref, wg_ref, wu_ref, wd_ref, yb_ref):
    j = pl.program_id(0)
    f = pl.program_id(1)

    @pl.when(j < nu_ref[0])
    def _():
        xg = xb_ref[...].astype(BF16)
        act = (_silu(_mm(xg, wg_ref[0])) * _mm(xg, wu_ref[0])).astype(BF16)
        part = _mm(act, wd_ref[0])

        @pl.when(f == 0)
        def _():
            yb_ref[...] = part

        @pl.when(f > 0)
        def _():
            yb_ref[...] = yb_ref[...] + part

    @pl.when((j >= nu_ref[0]) & (f == 0))
    def _():
        yb_ref[...] = jnp.zeros_like(yb_ref)


def _experts(block_e, n_used, xb, wg, wu, wd, tf):
    n_rows, d = xb.shape
    ff = wg.shape[2]
    nf = ff // tf
    n_blocks = n_rows // MOE_ROWS

    def blk(j, f, be, nu):
        return (jnp.minimum(j, nu[0] - 1), 0)

    def fcol(j, f, be, nu):
        return jnp.where(j < nu[0], f, nf - 1)

    return pl.pallas_call(
        _expert_kernel,
        grid_spec=pltpu.PrefetchScalarGridSpec(
            num_scalar_prefetch=2,
            grid=(n_blocks, nf),
            in_specs=[pl.BlockSpec((MOE_ROWS, d), blk),
                      pl.BlockSpec((1, d, tf), lambda j, f, be, nu: (be[j], 0, fcol(j, f, be, nu))),
                      pl.BlockSpec((1, d, tf), lambda j, f, be, nu: (be[j], 0, fcol(j, f, be, nu))),
                      pl.BlockSpec((1, tf, d), lambda j, f, be, nu: (be[j], fcol(j, f, be, nu), 0))],
            out_specs=pl.BlockSpec((MOE_ROWS, d), lambda j, f, be, nu: (j, 0))),
        out_shape=jax.ShapeDtypeStruct((n_rows, d), F32),
        compiler_params=_cparams(("arbitrary", "arbitrary"), 56),
        name="moe_experts",
    )(block_e, n_used, xb, wg, wu, wd)


def _combine_kernel(d1_ref, d2_ref, x_ref, gt_ref, p_ref, fg_ref, yb_ref, o_ref, y1_ref, y2_ref, sem, *,
                    tm, final, row_off):
    base = row_off + (pl.program_id(0) * pl.num_programs(1) + pl.program_id(1)) * tm

    def row_copy(src, buf_ref, j):
        return pltpu.make_async_copy(yb_ref.at[pl.ds(src, 1), :], buf_ref.at[pl.ds(j, 1), :], sem)

    def issue(j8, carry):
        for u in range(min(ROW_DMA_UNROLL, tm)):
            j = j8 * ROW_DMA_UNROLL + u
            row_copy(d1_ref[base + j], y1_ref, j).start(priority=0)
            row_copy(d2_ref[base + j], y2_ref, j).start(priority=1)
        return carry

    lax.fori_loop(0, tm // ROW_DMA_UNROLL, issue, 0)
    for buf_ref in (y1_ref, y2_ref):
        pltpu.make_async_copy(yb_ref.at[pl.ds(0, tm), :], buf_ref, sem).wait()

    p = p_ref[...]
    f = y1_ref[...] * p[:, 0:1] + y2_ref[...] * p[:, 1:2]
    x2 = x_ref[0] + _gated(f, gt_ref)
    o_ref[0] = _final_norm(x2, fg_ref) if final else x2


def _combine(d1, d2, x, mod, probs, fg, yb, final, tm, row_off):
    b, t, d = x.shape
    nt = t // tm
    blk_off = row_off // tm
    return pl.pallas_call(
        functools.partial(_combine_kernel, tm=tm, final=final, row_off=row_off),
        grid_spec=pltpu.PrefetchScalarGridSpec(
            num_scalar_prefetch=2,
            grid=(b, nt),
            in_specs=[pl.BlockSpec((1, tm, d), lambda i, j, *_: (i, j, 0)),
                      pl.BlockSpec((1, mod.shape[1], d), lambda i, j, *_: (i, 0, 5)),
                      pl.BlockSpec((tm, LANES), lambda i, j, *_: (blk_off + i * nt + j, 0)),
                      pl.BlockSpec((1, d), lambda i, j, *_: (0, 0)),
                      pl.BlockSpec(memory_space=pl.ANY)],
            out_specs=pl.BlockSpec((1, tm, d), lambda i, j, *_: (i, j, 0)),
            scratch_shapes=[pltpu.VMEM((tm, d), F32), pltpu.VMEM((tm, d), F32), pltpu.SemaphoreType.DMA]),
        out_shape=jax.ShapeDtypeStruct((b, t, d), F32),
        compiler_params=_cparams(("arbitrary", "arbitrary"), 24),
        name="moe_combine",
    )(d1, d2, x, mod, probs, fg, yb)


def _moe(h2, logits, trunks, wg, wu, wd, fg, final):
    n, d = h2.shape
    tile = lambda cap: max(r for r in (cap, cap // 2, cap // 4, cap // 8, cap // 16) if n % r == 0)
    info, probs, cnt = _route(logits, tile(256))
    counts = cnt[0, :N_EXPERTS]
    padded = (counts + MOE_ROWS - 1) // MOE_ROWS * MOE_ROWS
    ep = jnp.cumsum(padded).astype(jnp.int32)
    sp = ep - padded
    n_blocks = -(-(2 * n) // MOE_ROWS) + N_EXPERTS
    n_used = jnp.maximum(ep[-1] // MOE_ROWS, 1).astype(jnp.int32)
    blk_start = jnp.minimum(jnp.arange(n_blocks, dtype=jnp.int32), n_used - 1) * MOE_ROWS
    block_e = jnp.minimum(jnp.sum(blk_start[:, None] >= ep[None, :], axis=1), N_EXPERTS - 1).astype(jnp.int32)
    dest = _dest(info, jnp.pad(sp, (0, LANES - N_EXPERTS))[None, :], tile(1024))
    d1, d2 = dest[:, 0], dest[:, 1]
    xb = _dispatch(d1, d2, sp, ep, h2, n_blocks * MOE_ROWS, tile(512))
    yb = _experts(block_e, n_used.reshape(1), xb, wg, wu, wd, MOE_FF_TILE)
    return [_combine(d1, d2, x1, mod, probs, fg, yb, final, tm, off) for x1, mod, tm, off in trunks]


def _pad_rows_front(a, rows):
    pad = [(0, 0)] * a.ndim
    pad[-2] = (rows - a.shape[-2], 0)
    return jnp.pad(a, pad)


def _trunks(xs, mods, states, w):
    depth = w["wa"].shape[0]
    n_tr = len(xs)
    shapes = [x.shape for x in xs]
    d = shapes[0][2]
    offs = [sum(s[0] * s[1] for s in shapes[:i]) for i in range(n_tr)]
    n_all = sum(s[0] * s[1] for s in shapes)
    tms, mvs = [], []
    for (b, t, _), m in zip(shapes, mods):
        if t < TOKEN_TILE and TOKEN_TILE % t == 0 and (b * t) % TOKEN_TILE == 0:
            nseg = TOKEN_TILE // t
            tms.append(TOKEN_TILE)
            mvs.append(m.reshape(depth, b // nseg, nseg, m.shape[-1]))
        else:
            tms.append(min(t, TOKEN_TILE))
            mvs.append(m)
    tok = lambda a, i: a.reshape(-1, tms[i] if shapes[i][1] < tms[i] else shapes[i][1], a.shape[-1])
    seq = lambda a, i: a.reshape(shapes[i][0], shapes[i][1], a.shape[-1])
    xs = [tok(x, i) for i, x in enumerate(xs)]
    new_states = [([], [], [], []) for _ in xs]
    for l in range(depth):
        final = l == depth - 1
        j = l // 2
        x1s, flat = [], None
        for i, x in enumerate(xs):
            t = shapes[i][1]
            ml = mvs[i][l]
            conv0, gdn0, shift0, rwkv0 = states[i]
            pa, pab, pb = (seq(a, i) for a in _norm_proj(x, ml, w["norm1_g"][l], w["wa"][l], w["wab"][l],
                                                         w["wb"][l], tms[i]))
            oa, sg = _gdn(pa, pab, _pad_rows_front(conv0[l], SUBLANES), gdn0[l], w["conv_w"][l], w["alog"][l],
                          w["dtb"][l], w["onorm_g"][l], min(t, GDN_TILE))
            ob, sr = _rwkv(pb, _pad_rows_front(shift0[l], SUBLANES), rwkv0[l], [v[l] for v in w["rwkv_vecs"]],
                           w["w2p"][l], w["a2p"][l], w["g2"][l], w["bd"], min(t, RWKV_TILE))
            for lst, val in zip(new_states[i], (pa[:, t - (GDN_CONV - 1):, :w["conv_w"].shape[2]], sg,
                                                pb[:, t - 1:, :], sr)):
                lst.append(val)
            oa, ob = tok(oa, i), tok(ob, i)
            if l % 2 == 0:
                x1, h2 = _out_proj(oa, ob, x, ml, w["norm2_g"][l], w["woa"][l], w["wob"][l], tms[i])
                xs[i] = _ffn(h2, x1, ml, w["ffn_g"][j], w["ffn_u"][j], w["ffn_d"][j], w["final_g"], final, tms[i])
            else:
                x1, *flat = _out_proj(oa, ob, x, ml, w["norm2_g"][l], w["woa"][l], w["wob"][l], tms[i],
                                      router=w["router"][j], flat_rows=n_all, row_off=offs[i], flat_prev=flat)
                x1s.append(x1)
        if l % 2 == 1:
            trunks = [(x1s[i], mvs[i][l], tms[i], offs[i]) for i in range(n_tr)]
            xs = _moe(flat[0], flat[1], trunks, w["moe_g"][j], w["moe_u"][j], w["moe_d"][j], w["final_g"], final)
    return [seq(x, i) for i, x in enumerate(xs)], [tuple(jnp.stack(s) for s in st) for st in new_states]


def kernel(x_prompt, x_sample, c_prompt, c_sample, state_gdn_conv, state_gdn, state_rwkv_shift, state_rwkv, w_ada, b_ada, norm1_g, norm2_g, w_in, gdn_conv_w, gdn_a_log, gdn_dt_bias, gdn_onorm_g, rwkv_mu, rwkv_w0, rwkv_w2, rwkv_a0, rwkv_a2, rwkv_g2, rwkv_k_k, rwkv_k_a, rwkv_r_k, rwkv_ln_w, rwkv_ln_b, w_out, ffn_w_gate, ffn_w_up, ffn_w_down, moe_router, moe_w_gate, moe_w_up, moe_w_down, final_g):
    depth, d, _ = w_in.shape
    nbp = x_prompt.shape[0]
    nbs = x_sample.shape[0]
    nqkvz = 4 * GDN_HEADS * GDN_D
    nab = 2 * GDN_HEADS
    nw = RWKV_HEADS * RWKV_N
    lora_w = rwkv_w2.shape[1]

    def lane_pad(v):
        return jnp.pad(v, ((0, 0), (0, LANES - v.shape[1])))[:, None, :]

    rows = lambda v: v[:, None, :]
    hi = jnp.arange(nw) // RWKV_N
    w = dict(
        norm1_g=rows(norm1_g), norm2_g=rows(norm2_g), final_g=final_g[None, :],
        wa=w_in[:, :, :nqkvz].astype(BF16),
        wab=jnp.pad(w_in[:, :, nqkvz:nqkvz + nab], ((0, 0), (0, 0), (0, LANES - nab))).astype(BF16),
        wb=w_in[:, :, nqkvz + nab:].astype(BF16),
        conv_w=gdn_conv_w, alog=lane_pad(gdn_a_log), dtb=lane_pad(gdn_dt_bias), onorm_g=rows(gdn_onorm_g),
        rwkv_vecs=[rows(rwkv_mu), rows(rwkv_w0), rows(rwkv_a0), rows(rwkv_k_k), rows(rwkv_k_a),
                   rwkv_r_k.reshape(depth, 1, nw), rows(rwkv_ln_w), rows(rwkv_ln_b)],
        w2p=jnp.pad(rwkv_w2, ((0, 0), (0, LANES - lora_w), (0, 0))),
        a2p=jnp.pad(rwkv_a2, ((0, 0), (lora_w, LANES - lora_w - rwkv_a2.shape[1]), (0, 0))),
        g2=rwkv_g2,
        bd=(hi[:, None] == hi[None, :]).astype(F32),
        woa=w_out[:, :GDN_HEADS * GDN_D, :].astype(BF16), wob=w_out[:, GDN_HEADS * GDN_D:, :].astype(BF16),
        ffn_g=ffn_w_gate.astype(BF16), ffn_u=ffn_w_up.astype(BF16), ffn_d=ffn_w_down.astype(BF16),
        router=jnp.pad(moe_router, ((0, 0), (0, 0), (0, LANES - moe_router.shape[2]))),
        moe_g=moe_w_gate.astype(BF16), moe_u=moe_w_up.astype(BF16), moe_d=moe_w_down.astype(BF16),
    )

    nb_all = nbp + nbs
    bp = -(-nb_all // SUBLANES) * SUBLANES
    c_all = jnp.pad(jnp.concatenate([c_prompt, c_sample], axis=0), ((0, bp - nb_all), (0, 0)))
    mod = _ada_mod(c_all, w_ada, b_ada)[:, :, None, :]
    mod_p, mod_s = mod[:, :nbp], mod[:, nbp:nb_all]

    dt = x_prompt.dtype
    zc = jnp.zeros((depth, nbp) + state_gdn_conv.shape[2:], dt)
    zg = jnp.zeros((depth, nbp) + state_gdn.shape[2:], dt)
    zs = jnp.zeros((depth, nbp) + state_rwkv_shift.shape[2:], dt)
    zr = jnp.zeros((depth, nbp) + state_rwkv.shape[2:], dt)
    (y_p, y_s), (st_p, st_s) = _trunks(
        [x_prompt, x_sample], [mod_p, mod_s],
        [(zc, zg, zs, zr), (state_gdn_conv, state_gdn, state_rwkv_shift, state_rwkv)], w)
    return (y_p, y_s) + st_p + st_s
```

```python
import functools

import jax
import jax.numpy as jnp
from jax import lax
from jax.experimental import pallas as pl
from jax.experimental.pallas import tpu as pltpu

F32 = jnp.float32
BF16 = jnp.bfloat16
HI = lax.Precision.HIGHEST

LANES = 128
SUBLANES = 8
CHUNK = 64
GDN_HEADS = 4
GDN_D = 128
RWKV_HEADS = 8
RWKV_N = 64
GDN_CONV = 4
N_EXPERTS = 8
NORM_EPS = 1e-6
L2_EPS = 1e-6
GN_EPS = 64e-5
NEG_BIG = -1e30
RWKV_DECAY_SCALE = 0.6065306597126334
TOKEN_TILE = 512
GDN_TILE = 512
RWKV_TILE = 256
SHORT_SEQ_BATCH = 4
MOE_ROWS = 512
MOE_FF_TILE = 1792
MIB = 1024 * 1024

P_GDN_SCORE = 1
P_GDN_INV = 3
P_GDN_SOLVE = 1
P_GDN_STATE = 1
P_RWKV_SMALL = 1
P_RWKV_INV = 1
P_RWKV_STATE = 1
P_ROUTER = 3


def _cparams(sem, vmem_mib):
    return pltpu.CompilerParams(dimension_semantics=sem, vmem_limit_bytes=vmem_mib * MIB)


def _split_bf16(x):
    hi = x.astype(BF16)
    return hi, (x - hi.astype(F32)).astype(BF16)


def _split3_bf16(x):
    hi = x.astype(BF16)
    r1 = x - hi.astype(F32)
    mid = r1.astype(BF16)
    return hi, mid, (r1 - mid.astype(F32)).astype(BF16)


_CONTRACT = {"nn": (1, 0), "nt": (1, 1), "tn": (0, 0)}


def _dg(a, b, kind, prec):
    off = a.ndim - 2
    ca, cb = _CONTRACT[kind]
    dn = (((ca + off,), (cb + off,)), (((0,), (0,)) if off else ((), ())))
    dot = lambda x, y: lax.dot_general(x, y, dn, preferred_element_type=F32)
    if prec is None or prec is HI:
        return lax.dot_general(a, b, dn, preferred_element_type=F32, precision=prec)
    if prec == 1:
        return dot(a.astype(BF16), b.astype(BF16))
    if prec == "l3":
        bb = b.astype(BF16)
        a0, a1, a2 = _split3_bf16(a)
        return dot(a0, bb) + dot(a1, bb) + dot(a2, bb)
    if prec == "r3":
        ab = a.astype(BF16)
        b0, b1, b2 = _split3_bf16(b)
        return dot(ab, b0) + dot(ab, b1) + dot(ab, b2)
    ah, al = _split_bf16(a)
    bh, bl = _split_bf16(b)
    return dot(ah, bh) + dot(ah, bl) + dot(al, bh)


def _mm(a, b, prec=None):
    return _dg(a, b, "nn", prec)


def _mm_nt(a, b, prec=None):
    return _dg(a, b, "nt", prec)


def _mm_tn(a, b, prec=None):
    return _dg(a, b, "tn", prec)


def _silu(x):
    return x * jax.nn.sigmoid(x)


def _softplus(x):
    return jnp.maximum(x, 0.0) + jnp.log1p(jnp.exp(-jnp.abs(x)))


def _iota2(shape, dim):
    return lax.broadcasted_iota(jnp.int32, shape, dim)


def _per_segment(x, fn, *mod_refs):
    mods = [r[0] for r in mod_refs]
    nseg = mods[0].shape[0]
    if nseg == 1:
        return fn(x, *mods)
    rows = x.shape[0] // nseg
    return jnp.concatenate([fn(x[s * rows:(s + 1) * rows], *(m[s:s + 1] for m in mods)) for s in range(nseg)], axis=0)


def _scale_shift(xn, sc_ref, sh_ref):
    return _per_segment(xn, lambda x, sc, sh: x * (1.0 + sc) + sh, sc_ref, sh_ref)


def _gated(f, gt_ref):
    return _per_segment(f, lambda x, gt: gt * x, gt_ref)


def _shift_rows(x, prev8, k):
    head = jnp.where(_iota2(prev8.shape, 0) < k, pltpu.roll(prev8, k, 0), pltpu.roll(x[0:SUBLANES], k, 0))
    return jnp.concatenate([head, pltpu.roll(x, k, 0)[SUBLANES:]], axis=0)


def _causal_taps(x, w, prev8, ntap):
    y = x * w[ntap - 1:ntap, :]
    for k in range(1, ntap):
        y = y + _shift_rows(x, prev8, k) * w[ntap - 1 - k:ntap - k, :]
    return y


def _neumann_inverse(x, eye, prec):
    c = x.shape[-1]
    z = jnp.concatenate([x, jnp.broadcast_to(eye, x.shape)], axis=-1)
    keep_s = _iota2((c, 2 * c), 1) >= c
    for _ in range(6):
        z = _mm(z[..., :c], z, prec) + jnp.where(keep_s, z, 0.0)
    return z[..., c:]


def _ada_kernel(c_ref, w_ref, b_ref, o_ref):
    o_ref[0] = _mm(_silu(c_ref[...]), w_ref[0], HI) + b_ref[0]


def _ada_mod(c_all, w_ada, b_ada):
    depth, d, n6 = w_ada.shape
    bp = c_all.shape[0]
    tn = 1536
    return pl.pallas_call(
        _ada_kernel,
        grid=(depth, n6 // tn),
        in_specs=[pl.BlockSpec((bp, d), lambda l, j: (0, 0)),
                  pl.BlockSpec((1, d, tn), lambda l, j: (l, 0, j)),
                  pl.BlockSpec((1, 1, tn), lambda l, j: (l, 0, j))],
        out_specs=pl.BlockSpec((1, bp, tn), lambda l, j: (l, 0, j)),
        out_shape=jax.ShapeDtypeStruct((depth, bp, n6), F32),
        compiler_params=_cparams(("arbitrary", "arbitrary"), 40),
        name="ada_mod",
    )(c_all, w_ada, b_ada.reshape(depth, 1, n6))


def _norm_proj_kernel(x_ref, sh_ref, sc_ref, g_ref, wa_ref, wab_ref, wb_ref, pa_ref, pab_ref, pb_ref):
    x = x_ref[0]
    rs = lax.rsqrt(jnp.mean(x * x, axis=-1, keepdims=True) + NORM_EPS)
    hb = _scale_shift(x * rs * g_ref[...], sc_ref, sh_ref).astype(BF16)
    pa_ref[0] = _mm(hb, wa_ref[...])
    pab_ref[0] = _mm(hb, wab_ref[...])
    pb_ref[0] = _mm(hb, wb_ref[...])


def _norm_proj(x, mod, g, wa, wab, wb, tm):
    b, t, d = x.shape
    na, nab, nb = wa.shape[1], wab.shape[1], wb.shape[1]
    nseg = mod.shape[1]
    row = lambda i, j: (i, j, 0)
    const = lambda i, j: (0, 0)
    return pl.pallas_call(
        _norm_proj_kernel,
        grid=(b, t // tm),
        in_specs=[pl.BlockSpec((1, tm, d), row),
                  pl.BlockSpec((1, nseg, d), lambda i, j: (i, 0, 0)),
                  pl.BlockSpec((1, nseg, d), lambda i, j: (i, 0, 1)),
                  pl.BlockSpec((1, d), const),
                  pl.BlockSpec((d, na), const),
                  pl.BlockSpec((d, nab), const),
                  pl.BlockSpec((d, nb), const)],
        out_specs=[pl.BlockSpec((1, tm, na), row), pl.BlockSpec((1, tm, nab), row), pl.BlockSpec((1, tm, nb), row)],
        out_shape=[jax.ShapeDtypeStruct((b, t, na), F32), jax.ShapeDtypeStruct((b, t, nab), F32),
                   jax.ShapeDtypeStruct((b, t, nb), F32)],
        compiler_params=_cparams(("arbitrary", "arbitrary"), 48),
        name="norm_proj",
    )(x, mod, mod, g, wa, wab, wb)


def _gdn_kernel(pa_ref, pab_ref, cbuf_ref, s0_ref, cw_ref, alog_ref, dtb_ref, og_ref,
                o_ref, sfin_ref, xc_ref, s_ref, *, tc):
    t = pl.program_id(1)
    nqk = GDN_HEADS * GDN_D
    nconv = 3 * nqk

    nsq = pa_ref.shape[0]
    nh = GDN_HEADS

    @pl.when(t == 0)
    def _():
        xc_ref[...] = cbuf_ref[...]
        s_ref[...] = s0_ref[...].reshape(s_ref.shape)

    ys = []
    for sq in range(nsq):
        x = pa_ref[sq, :, 0:nconv]
        ys.append(_causal_taps(x, cw_ref[...], xc_ref[sq], GDN_CONV))
        xc_ref[sq] = x[tc - SUBLANES:tc, :]
    qkv = _silu(ys[0] if nsq == 1 else jnp.concatenate(ys, axis=0))

    ab = pab_ref[...].reshape(nsq * tc, LANES)
    gmat = -jnp.exp(alog_ref[...]) * _softplus(ab + dtb_ref[...])
    bmat = jax.nn.sigmoid(ab)

    c = CHUNK
    nch = nsq * tc // c
    nb = nch * nh
    ri = _iota2((c, c), 0)
    ci = _iota2((c, c), 1)
    incl = ri >= ci
    strict = ri > ci
    eye_c = (ri == ci).astype(F32)

    def heads(x, width):
        x = x.reshape(nch, c, nh * width)
        return jnp.stack([x[:, :, h * width:(h + 1) * width] for h in range(nh)], axis=1).reshape(nb, c, width)

    q = heads(qkv[:, 0:nqk], GDN_D)
    k = heads(qkv[:, nqk:2 * nqk], GDN_D)
    v = heads(qkv[:, 2 * nqk:3 * nqk], GDN_D)
    ones_d = jnp.ones((GDN_D, GDN_D), BF16)

    def l2n(x, scale):
        ss = _mm((x * x).reshape(nb * c, GDN_D), ones_d, 1).reshape(nb, c, GDN_D)
        return x * (lax.rsqrt(ss + L2_EPS) * scale)

    q = l2n(q, GDN_D ** -0.5)
    k = l2n(k, 1.0)
    beta = heads(bmat[:, nh:2 * nh], 1)

    g4 = gmat.reshape(nch, c, LANES)
    gcol4 = _mm(jnp.broadcast_to(incl.astype(F32), (nch, c, c)), g4, "r3")
    grow4 = _mm_tn(g4, jnp.broadcast_to((ri <= ci).astype(F32), (nch, c, c)), "l3")
    gc = jnp.stack([gcol4[:, :, h:h + 1] for h in range(nh)], axis=1).reshape(nb, c, 1)
    gr = jnp.stack([grow4[:, h:h + 1, :] for h in range(nh)], axis=1).reshape(nb, 1, c)
    decay = jnp.exp(jnp.where(incl, gc - gr, NEG_BIG))

    kb = k * beta
    m = jnp.where(strict, _mm_nt(kb, k, P_GDN_SCORE) * decay, 0.0)
    t0 = _neumann_inverse(-m, eye_c, 1)
    tinv = t0 + _mm(t0, (eye_c - t0) - _mm(m, t0, P_GDN_INV), 1)
    egc = jnp.exp(gc)
    sol = _mm(tinv, jnp.concatenate([v * beta, kb * egc], axis=2), P_GDN_SOLVE)
    u, w = sol[:, :, :GDN_D], sol[:, :, GDN_D:]
    aqk = _mm_nt(q, k, P_GDN_SCORE) * decay
    gl = gc[:, c - 1:c, :]
    kd = k * jnp.exp(gl - gc)
    qg = q * egc
    egl = jnp.exp(gl)

    s = s_ref[...]
    outs = []
    for ch in range(tc // c):
        bs = slice(ch * nh, (ch + 1) * nh) if nsq == 1 else slice(None)
        v_new = u[bs] - _mm(w[bs], s, P_GDN_STATE)
        outs.append(_mm(qg[bs], s, P_GDN_STATE) + _mm(aqk[bs], v_new, P_GDN_STATE))
        s = s * egl[bs] + _mm_tn(kd[bs], v_new, P_GDN_STATE)
    s_ref[...] = s

    o = jnp.stack(outs, axis=0).reshape(nch, nh, c, GDN_D)
    o = o * lax.rsqrt(jnp.mean(o * o, axis=-1, keepdims=True) + NORM_EPS) * og_ref[...]
    for h in range(nh):
        sl = slice(h * GDN_D, (h + 1) * GDN_D)
        z = pa_ref[:, :, nconv + h * GDN_D: nconv + (h + 1) * GDN_D]
        o_ref[:, :, sl] = (o[:, h].reshape(nsq, tc, GDN_D) * _silu(z)).astype(o_ref.dtype)

    @pl.when(t == pl.num_programs(1) - 1)
    def _():
        sfin_ref[...] = s_ref[...].reshape(sfin_ref.shape)


def _seqs_per_step(b, t, tc):
    return max(n for n in (SHORT_SEQ_BATCH, 2, 1) if b % n == 0) if t == tc == CHUNK else 1


def _gdn(pa, pab, cbuf8, s0, cw, alog, dtb, og, tc):
    b, t, na = pa.shape
    nconv = cw.shape[1]
    nsq = _seqs_per_step(b, t, tc)
    row = lambda i, j: (i, j, 0)
    const2 = lambda i, j: (0, 0)
    return pl.pallas_call(
        functools.partial(_gdn_kernel, tc=tc),
        grid=(b // nsq, t // tc),
        in_specs=[pl.BlockSpec((nsq, tc, na), row),
                  pl.BlockSpec((nsq, tc, LANES), row),
                  pl.BlockSpec((nsq, SUBLANES, nconv), lambda i, j: (i, 0, 0)),
                  pl.BlockSpec((nsq, GDN_HEADS, GDN_D, GDN_D), lambda i, j: (i, 0, 0, 0)),
                  pl.BlockSpec((GDN_CONV, nconv), const2),
                  pl.BlockSpec((1, LANES), const2),
                  pl.BlockSpec((1, LANES), const2),
                  pl.BlockSpec((1, GDN_D), const2)],
        out_specs=[pl.BlockSpec((nsq, tc, GDN_HEADS * GDN_D), row),
                   pl.BlockSpec((nsq, GDN_HEADS, GDN_D, GDN_D), lambda i, j: (i, 0, 0, 0))],
        out_shape=[jax.ShapeDtypeStruct((b, t, GDN_HEADS * GDN_D), BF16),
                   jax.ShapeDtypeStruct((b, GDN_HEADS, GDN_D, GDN_D), F32)],
        scratch_shapes=[pltpu.VMEM((nsq, SUBLANES, nconv), F32),
                        pltpu.VMEM((nsq * GDN_HEADS, GDN_D, GDN_D), F32)],
        compiler_params=_cparams(("arbitrary", "arbitrary"), 40),
        name="gdn_mixer",
    )(pa, pab, cbuf8, s0, cw, alog, dtb, og)


def _rwkv_kernel(pb_ref, sbuf_ref, s0_ref, mu_ref, w0_ref, w2_ref, a0_ref, a2_ref, g2_ref, kk_ref, ka_ref,
                 rk_ref, lnw_ref, lnb_ref, bd_ref, o_ref, sfin_ref, xc_ref, s_ref, y_ref, *, tc):
    t = pl.program_id(1)
    nw = RWKV_HEADS * RWKV_N

    nsq = pb_ref.shape[0]
    nh = RWKV_HEADS

    @pl.when(t == 0)
    def _():
        xc_ref[...] = sbuf_ref[...]
        s_ref[...] = s0_ref[...].reshape(s_ref.shape)

    xl, pl_ = [], []
    for sq in range(nsq):
        xq = pb_ref[sq]
        xl.append(xq)
        pl_.append(_shift_rows(xq, xc_ref[sq], 1))
        xc_ref[sq] = xq[tc - SUBLANES:tc, :]
    x = xl[0] if nsq == 1 else jnp.concatenate(xl, axis=0)
    prev = pl_[0] if nsq == 1 else jnp.concatenate(pl_, axis=0)
    xs = x + (prev - x) * mu_ref[...]
    r = xs[:, 0:nw]
    kr = xs[:, nw:2 * nw]
    vr = xs[:, 2 * nw:3 * nw]
    wa = xs[:, 3 * nw:3 * nw + LANES]
    gd = xs[:, 3 * nw + LANES:3 * nw + 2 * LANES]

    bd = bd_ref[...]
    ps = P_RWKV_SMALL
    lw = -RWKV_DECAY_SCALE * jax.nn.sigmoid(w0_ref[...] + _mm(jnp.tanh(wa), w2_ref[...], ps))
    a = jax.nn.sigmoid(a0_ref[...] + _mm(wa, a2_ref[...], ps))
    gate = _mm(jax.nn.sigmoid(gd), g2_ref[...], ps)
    kkr = kr * kk_ref[...]
    kk = kkr * lax.rsqrt(_mm(kkr * kkr, bd, ps) + L2_EPS)
    kb = kr * (1.0 + (a - 1.0) * ka_ref[...])
    aa = -kk
    bb = kk * a
    bonus = _mm(r * kb * rk_ref[...], bd, ps) * vr

    c = CHUNK
    n = RWKV_N
    nch = nsq * tc // c
    nb = nch * nh
    ri = _iota2((c, c), 0)
    ci = _iota2((c, c), 1)
    incl = ri >= ci
    eye = ri == ci
    eye_c = eye.astype(F32)

    def heads(x):
        return jnp.stack([x[:, :, h * n:(h + 1) * n] for h in range(nh)], axis=1).reshape(nb, x.shape[1], n)

    chunks = lambda x: x.reshape(nch, c, nw)
    lw4 = chunks(lw)
    g = _mm(jnp.broadcast_to(incl.astype(F32), (nch, c, c)), lw4, "r3")
    gfin = g[:, c - 1:c, :]
    eng = jnp.exp(-g)
    efin = jnp.exp(gfin)
    tail = efin * eng
    aa4, bb4, kb4 = chunks(aa), chunks(bb), chunks(kb)
    at = heads(aa4 * jnp.exp(g - lw4))
    bt = heads(bb4 * eng)
    kt = heads(kb4 * eng)
    rt = heads(chunks(r) * jnp.exp(g))
    bh = heads(bb4 * tail)
    kh = heads(kb4 * tail)
    egf = heads(efin)
    v4 = chunks(vr)
    upper = _iota2((c, 2 * n), 1) >= n
    vcols = [v4[:, :, (h // 2) * 2 * n:(h // 2 + 1) * 2 * n] for h in range(nh)]
    vhi = jnp.stack([jnp.where(upper, col if h % 2 else pltpu.roll(col, n, 2), 0.0)
                     for h, col in enumerate(vcols)], axis=1).reshape(nb, c, 2 * n)
    zeros_lo = jnp.zeros((nb, c, n), F32)
    pad_hi = lambda x: jnp.concatenate([x, zeros_lo], axis=-1)

    h16 = lambda x: x.astype(BF16)
    colj = _iota2((c, 2 * c), 1) % c
    rowi = _iota2((c, 2 * c), 0)
    vhi16 = h16(vhi)
    ar = jnp.concatenate([h16(at), h16(rt)], axis=1)
    xbk = _mm_nt(ar, jnp.concatenate([h16(bt), h16(kt)], axis=1))
    l2 = jnp.where(rowi > colj, xbk[:, :c], 0.0)
    m2 = jnp.where(rowi >= colj, xbk[:, c:], 0.0)
    tinv = _neumann_inverse(l2[..., :c], eye_c, P_RWKV_INV)
    lv = _mm(h16(l2), jnp.concatenate([jnp.zeros_like(vhi16), vhi16], axis=1))
    wu = _mm(h16(tinv), h16(lv + pad_hi(at)))
    wv = jnp.concatenate([h16(wu), vhi16], axis=1)
    qy = _mm(h16(m2), wv) + pad_hi(rt)
    pi = _mm_tn(wv, jnp.concatenate([h16(bh), h16(kh)], axis=1))
    pt = jnp.where(eye, egf, 0.0) + pi[:, :n]
    inc = pi[:, n:]

    s = s_ref[...]
    zeros_s = jnp.zeros_like(s)
    ys = []
    for ch in range(tc // c):
        bs = slice(ch * nh, (ch + 1) * nh) if nsq == 1 else slice(None)
        ys.append(_mm_nt(qy[bs][..., :n], jnp.concatenate([zeros_s, s], axis=1), P_RWKV_STATE) + qy[bs])
        s = _mm(s, pt[bs], P_RWKV_STATE) + inc[bs]
    s_ref[...] = s
    y4 = jnp.stack(ys, axis=0).reshape(nch, nh, c, 2 * n)
    for h in range(nh):
        y_ref[:, h * n:(h + 1) * n] = y4[:, h, :, n:].reshape(nsq * tc, n)

    y = y_ref[...]
    mean = _mm(y, bd, ps) * (1.0 / n)
    dlt = y - mean
    var = _mm(dlt * dlt, bd, ps) * (1.0 / n)
    yn = dlt * lax.rsqrt(var + GN_EPS) * lnw_ref[...] + lnb_ref[...]
    o_ref[...] = ((yn + bonus) * gate).reshape(o_ref.shape).astype(o_ref.dtype)

    @pl.when(t == pl.num_programs(1) - 1)
    def _():
        sfin_ref[...] = s_ref[...].reshape(sfin_ref.shape)


def _rwkv(pb, sbuf8, s0, vecs, w2p, a2p, g2, bd, tc):
    b, t, nb = pb.shape
    nw = RWKV_HEADS * RWKV_N
    nsq = _seqs_per_step(b, t, tc)
    row = lambda i, j: (i, j, 0)
    const2 = lambda i, j: (0, 0)
    mu, w0, a0, kk, ka, rk, lnw, lnb = vecs
    vec_spec = pl.BlockSpec((1, nw), const2)
    return pl.pallas_call(
        functools.partial(_rwkv_kernel, tc=tc),
        grid=(b // nsq, t // tc),
        in_specs=[pl.BlockSpec((nsq, tc, nb), row),
                  pl.BlockSpec((nsq, SUBLANES, nb), lambda i, j: (i, 0, 0)),
                  pl.BlockSpec((nsq, RWKV_HEADS, RWKV_N, RWKV_N), lambda i, j: (i, 0, 0, 0)),
                  pl.BlockSpec((1, nb), const2),
                  vec_spec,
                  pl.BlockSpec((LANES, nw), const2),
                  vec_spec,
                  pl.BlockSpec((LANES, nw), const2),
                  pl.BlockSpec((LANES, nw), const2),
                  vec_spec, vec_spec, vec_spec, vec_spec, vec_spec,
                  pl.BlockSpec((nw, nw), const2)],
        out_specs=[pl.BlockSpec((nsq, tc, nw), row),
                   pl.BlockSpec((nsq, RWKV_HEADS, RWKV_N, RWKV_N), lambda i, j: (i, 0, 0, 0))],
        out_shape=[jax.ShapeDtypeStruct((b, t, nw), BF16),
                   jax.ShapeDtypeStruct((b, RWKV_HEADS, RWKV_N, RWKV_N), F32)],
        scratch_shapes=[pltpu.VMEM((nsq, SUBLANES, nb), F32),
                        pltpu.VMEM((nsq * RWKV_HEADS, RWKV_N, RWKV_N), F32),
                        pltpu.VMEM((nsq * tc, nw), F32)],
        compiler_params=_cparams(("arbitrary", "arbitrary"), 40),
        name="rwkv_mixer",
    )(pb, sbuf8, s0, mu, w0, w2p, a0, a2p, g2, kk, ka, rk, lnw, lnb, bd)


def _out_proj_kernel(oa_ref, ob_ref, x_ref, gt_ref, sh_ref, sc_ref, g_ref, woa_ref, wob_ref, *rest, with_router, nt):
    if with_router:
        wr_ref = rest[0]
        x1_ref, h_ref, lg_ref = rest[-3:]
    else:
        x1_ref, h_ref = rest

    @pl.when(pl.program_id(1) < nt)
    def _():
        mix = _mm(oa_ref[0], woa_ref[...]) + _mm(ob_ref[0], wob_ref[...])
        x1 = x_ref[0] + _gated(mix, gt_ref)
        x1_ref[0] = x1
        rs = lax.rsqrt(jnp.mean(x1 * x1, axis=-1, keepdims=True) + NORM_EPS)
        h = _scale_shift(x1 * rs * g_ref[...], sc_ref, sh_ref)
        if with_router:
            h_ref[...] = h
            lg_ref[...] = _mm(h, wr_ref[...], P_ROUTER)
        else:
            h_ref[0] = h.astype(h_ref.dtype)

    if with_router:
        @pl.when(pl.program_id(1) >= nt)
        def _():
            h_ref[...] = jnp.zeros_like(h_ref)
            lg_ref[...] = jnp.zeros_like(lg_ref)


def _out_proj(oa, ob, x, mod, g, woa, wob, tm, router=None, flat_rows=None, row_off=0, flat_prev=None):
    b, t, d = x.shape
    nh = oa.shape[2]
    nt = t // tm
    nseg = mod.shape[1]
    tail = 0
    if router is not None and flat_prev is None:
        assert b == 1 and row_off == 0 and (flat_rows - t) % tm == 0
        tail = (flat_rows - t) // tm
    row = lambda i, j: (i, jnp.minimum(j, nt - 1), 0)
    const = lambda i, j: (0, 0)
    in_specs = [pl.BlockSpec((1, tm, nh), row), pl.BlockSpec((1, tm, nh), row), pl.BlockSpec((1, tm, d), row),
                pl.BlockSpec((1, nseg, d), lambda i, j: (i, 0, 2)),
                pl.BlockSpec((1, nseg, d), lambda i, j: (i, 0, 3)),
                pl.BlockSpec((1, nseg, d), lambda i, j: (i, 0, 4)),
                pl.BlockSpec((1, d), const), pl.BlockSpec((nh, d), const), pl.BlockSpec((nh, d), const)]
    args = [oa, ob, x, mod, mod, mod, g, woa, wob]
    aliases = {}
    if router is None:
        out_specs = [pl.BlockSpec((1, tm, d), row), pl.BlockSpec((1, tm, d), row)]
        out_shape = [jax.ShapeDtypeStruct((b, t, d), F32), jax.ShapeDtypeStruct((b, t, d), BF16)]
    else:
        blk_off = row_off // tm
        flat = lambda i, j: (blk_off + i * nt + j, 0)
        in_specs.append(pl.BlockSpec((d, LANES), const))
        args.append(router)
        if flat_prev is not None:
            aliases = {len(args): 1, len(args) + 1: 2}
            in_specs += [pl.BlockSpec(memory_space=pl.ANY)] * 2
            args += list(flat_prev)
        out_specs = [pl.BlockSpec((1, tm, d), row), pl.BlockSpec((tm, d), flat), pl.BlockSpec((tm, LANES), flat)]
        out_shape = [jax.ShapeDtypeStruct((b, t, d), F32), jax.ShapeDtypeStruct((flat_rows, d), F32),
                     jax.ShapeDtypeStruct((flat_rows, LANES), F32)]
    return pl.pallas_call(
        functools.partial(_out_proj_kernel, with_router=router is not None, nt=nt),
        grid=(b, nt + tail),
        in_specs=in_specs, out_specs=out_specs, out_shape=out_shape, input_output_aliases=aliases,
        compiler_params=_cparams(("arbitrary", "arbitrary"), 56),
        name="out_proj" if router is None else "out_proj_router",
    )(*args)


def _final_norm(x2, fg_ref):
    rs = lax.rsqrt(jnp.mean(x2 * x2, axis=-1, keepdims=True) + NORM_EPS)
    return x2 * rs * fg_ref[...]


def _ffn_kernel(h_ref, x_ref, gt_ref, wg_ref, wu_ref, wd_ref, fg_ref, o_ref, *, nf, final):
    h = h_ref[0]
    ff = wg_ref.shape[1]
    tf = ff // nf
    acc = None
    for f in range(nf):
        cols = slice(f * tf, (f + 1) * tf)
        act = (_silu(_mm(h, wg_ref[:, cols])) * _mm(h, wu_ref[:, cols])).astype(BF16)
        part = _mm(act, wd_ref[cols, :])
        acc = part if acc is None else acc + part
    x2 = x_ref[0] + _gated(acc, gt_ref)
    o_ref[0] = _final_norm(x2, fg_ref) if final else x2


def _ffn(h, x, mod, wg, wu, wd, fg, final, tm):
    b, t, d = x.shape
    ff = wg.shape[1]
    row = lambda i, j: (i, j, 0)
    const = lambda i, j: (0, 0)
    once = pl.Buffered(1)
    return pl.pallas_call(
        functools.partial(_ffn_kernel, nf=1, final=final),
        grid=(b, t // tm),
        in_specs=[pl.BlockSpec((1, tm, d), row), pl.BlockSpec((1, tm, d), row),
                  pl.BlockSpec((1, mod.shape[1], d), lambda i, j: (i, 0, 5)),
                  pl.BlockSpec((d, ff), const, pipeline_mode=once),
                  pl.BlockSpec((d, ff), const, pipeline_mode=once),
                  pl.BlockSpec((ff, d), const, pipeline_mode=once),
                  pl.BlockSpec((1, d), const)],
        out_specs=pl.BlockSpec((1, tm, d), row),
        out_shape=jax.ShapeDtypeStruct((b, t, d), F32),
        compiler_params=_cparams(("arbitrary", "arbitrary"), 56),
        name="ffn_dense",
    )(h, x, mod, wg, wu, wd, fg)


def _route_kernel(lg_ref, info_ref, p_ref, cnt_ref, carry_ref, *, tr):
    i = pl.program_id(0)

    @pl.when(i == 0)
    def _():
        carry_ref[...] = jnp.zeros_like(carry_ref)

    lane = _iota2((tr, LANES), 1)
    lg = jnp.where(lane < N_EXPERTS, lg_ref[...], NEG_BIG)
    m1 = jnp.max(lg, axis=-1, keepdims=True)
    e1 = jnp.min(jnp.where(lg == m1, lane, LANES), axis=-1, keepdims=True)
    lg2 = jnp.where(lane == e1, NEG_BIG, lg)
    m2 = jnp.max(lg2, axis=-1, keepdims=True)
    e2 = jnp.min(jnp.where(lg2 == m2, lane, LANES), axis=-1, keepdims=True)
    ex = jnp.exp(m2 - m1)
    den = 1.0 + ex
    p1 = 1.0 / den
    p2 = ex / den
    oh1 = lane == e1
    oh2 = lane == e2
    oh = jnp.where(oh1 | oh2, 1.0, 0.0)
    lstrict = (_iota2((tr, tr), 0) > _iota2((tr, tr), 1)).astype(BF16)
    cex = _mm(lstrict, oh.astype(BF16)) + carry_ref[...]
    rank1 = jnp.sum(jnp.where(oh1, cex, 0.0), axis=-1, keepdims=True).astype(jnp.int32)
    rank2 = jnp.sum(jnp.where(oh2, cex, 0.0), axis=-1, keepdims=True).astype(jnp.int32)
    carry_ref[...] = carry_ref[...] + jnp.sum(oh, axis=0, keepdims=True)
    info_ref[...] = jnp.where(lane == 0, e1, jnp.where(lane == 1, e2, jnp.where(lane == 2, rank1,
                              jnp.where(lane == 3, rank2, 0))))
    p_ref[...] = jnp.where(lane == 0, p1, jnp.where(lane == 1, p2, 0.0))

    @pl.when(i == pl.num_programs(0) - 1)
    def _():
        cnt_ref[...] = carry_ref[...].astype(jnp.int32)


def _route(logits, tr):
    n = logits.shape[0]
    blk = pl.BlockSpec((tr, LANES), lambda i: (i, 0))
    return pl.pallas_call(
        functools.partial(_route_kernel, tr=tr),
        grid=(n // tr,),
        in_specs=[blk],
        out_specs=[blk, blk, pl.BlockSpec((1, LANES), lambda i: (0, 0))],
        out_shape=[jax.ShapeDtypeStruct((n, LANES), jnp.int32), jax.ShapeDtypeStruct((n, LANES), F32),
                   jax.ShapeDtypeStruct((1, LANES), jnp.int32)],
        scratch_shapes=[pltpu.VMEM((1, LANES), F32)],
        compiler_params=_cparams(("arbitrary",), 24),
        name="moe_route",
    )(logits)


def _dest_kernel(info_ref, sp_ref, d_ref):
    info = info_ref[...]
    lane = _iota2(info.shape, 1)
    sp = sp_ref[...].astype(F32)
    start1 = jnp.sum(jnp.where(lane == info[:, 0:1], sp, 0.0), axis=-1, keepdims=True).astype(jnp.int32)
    start2 = jnp.sum(jnp.where(lane == info[:, 1:2], sp, 0.0), axis=-1, keepdims=True).astype(jnp.int32)
    d_ref[...] = jnp.where(lane == 0, start1 + info[:, 2:3], jnp.where(lane == 1, start2 + info[:, 3:4], 0))


def _dest(info, sp_row, tr):
    n = info.shape[0]
    blk = pl.BlockSpec((tr, LANES), lambda i: (i, 0))
    return pl.pallas_call(
        _dest_kernel,
        grid=(n // tr,),
        in_specs=[blk, pl.BlockSpec((1, LANES), lambda i: (0, 0))],
        out_specs=blk,
        out_shape=jax.ShapeDtypeStruct((n, LANES), jnp.int32),
        compiler_params=_cparams(("arbitrary",), 24),
        name="moe_dest",
    )(info, sp_row)


ROW_DMA_UNROLL = 8


def _dispatch_kernel(d1_ref, d2_ref, sp_ref, ep_ref, h_ref, xb_ref, zero_ref, sem, zsem, *, td):
    i = pl.program_id(0)

    @pl.when(i == 0)
    def _():
        zero_ref[...] = jnp.zeros_like(zero_ref)

        def zero_block(start):
            cp = pltpu.make_async_copy(zero_ref, xb_ref.at[pl.ds(pl.multiple_of(start, MOE_ROWS), MOE_ROWS), :], zsem)
            cp.start()
            cp.wait()

        for e in range(N_EXPERTS):
            @pl.when(ep_ref[e] > sp_ref[e])
            def _():
                zero_block(ep_ref[e] - MOE_ROWS)

        def tail(blk, carry):
            zero_block(blk * MOE_ROWS)
            return carry

        lax.fori_loop(ep_ref[N_EXPERTS - 1] // MOE_ROWS, xb_ref.shape[0] // MOE_ROWS, tail, 0)

    base = i * td

    def row_copy(j, dst):
        return pltpu.make_async_copy(h_ref.at[pl.ds(j, 1), :], xb_ref.at[pl.ds(dst, 1), :], sem)

    def issue(j8, carry):
        for u in range(ROW_DMA_UNROLL):
            j = j8 * ROW_DMA_UNROLL + u
            row_copy(j, d1_ref[base + j]).start(priority=0)
            row_copy(j, d2_ref[base + j]).start(priority=1)
        return carry

    lax.fori_loop(0, td // ROW_DMA_UNROLL, issue, 0)
    for _ in range(2):
        pltpu.make_async_copy(h_ref, xb_ref.at[pl.ds(0, td), :], sem).wait()


def _dispatch(d1, d2, sp, ep, h2, n_rows, td):
    n, d = h2.shape
    return pl.pallas_call(
        functools.partial(_dispatch_kernel, td=td),
        grid_spec=pltpu.PrefetchScalarGridSpec(
            num_scalar_prefetch=4,
            grid=(n // td,),
            in_specs=[pl.BlockSpec((td, d), lambda i, *_: (i, 0))],
            out_specs=pl.BlockSpec(memory_space=pl.ANY),
            scratch_shapes=[pltpu.VMEM((MOE_ROWS, d), F32), pltpu.SemaphoreType.DMA, pltpu.SemaphoreType.DMA]),
        out_shape=jax.ShapeDtypeStruct((n_rows, d), F32),
        compiler_params=_cparams(("arbitrary",), 24),
        name="moe_dispatch",
    )(d1, d2, sp, ep, h2)


def _expert_kernel(be_ref, nu_ref, xb_ref, wg_ref, wu_ref, wd_ref, yb_ref):
    j = pl.program_id(0)
    f = pl.program_id(1)

    @pl.when(j < nu_ref[0])
    def _():
        xg = xb_ref[...].astype(BF16)
        act = (_silu(_mm(xg, wg_ref[0])) * _mm(xg, wu_ref[0])).astype(BF16)
        part = _mm(act, wd_ref[0])

        @pl.when(f == 0)
        def _():
            yb_ref[...] = part

        @pl.when(f > 0)
        def _():
            yb_ref[...] = yb_ref[...] + part

    @pl.when((j >= nu_ref[0]) & (f == 0))
    def _():
        yb_ref[...] = jnp.zeros_like(yb_ref)


def _experts(block_e, n_used, xb, wg, wu, wd, tf):
    n_rows, d = xb.shape
    ff = wg.shape[2]
    nf = ff // tf
    n_blocks = n_rows // MOE_ROWS

    def blk(j, f, be, nu):
        return (jnp.minimum(j, nu[0] - 1), 0)

    def fcol(j, f, be, nu):
        return jnp.where(j < nu[0], f, nf - 1)

    return pl.pallas_call(
        _expert_kernel,
        grid_spec=pltpu.PrefetchScalarGridSpec(
            num_scalar_prefetch=2,
            grid=(n_blocks, nf),
            in_specs=[pl.BlockSpec((MOE_ROWS, d), blk),
                      pl.BlockSpec((1, d, tf), lambda j, f, be, nu: (be[j], 0, fcol(j, f, be, nu))),
                      pl.BlockSpec((1, d, tf), lambda j, f, be, nu: (be[j], 0, fcol(j, f, be, nu))),
                      pl.BlockSpec((1, tf, d), lambda j, f, be, nu: (be[j], fcol(j, f, be, nu), 0))],
            out_specs=pl.BlockSpec((MOE_ROWS, d), lambda j, f, be, nu: (j, 0))),
        out_shape=jax.ShapeDtypeStruct((n_rows, d), F32),
        compiler_params=_cparams(("arbitrary", "arbitrary"), 56),
        name="moe_experts",
    )(block_e, n_used, xb, wg, wu, wd)


def _combine_kernel(d1_ref, d2_ref, x_ref, gt_ref, p_ref, fg_ref, yb_ref, o_ref, y1_ref, y2_ref, sem, *,
                    tm, final, row_off):
    base = row_off + (pl.program_id(0) * pl.num_programs(1) + pl.program_id(1)) * tm

    def row_copy(src, buf_ref, j):
        return pltpu.make_async_copy(yb_ref.at[pl.ds(src, 1), :], buf_ref.at[pl.ds(j, 1), :], sem)

    def issue(j8, carry):
        for u in range(min(ROW_DMA_UNROLL, tm)):
            j = j8 * ROW_DMA_UNROLL + u
            row_copy(d1_ref[base + j], y1_ref, j).start(priority=0)
            row_copy(d2_ref[base + j], y2_ref, j).start(priority=1)
        return carry

    lax.fori_loop(0, tm // ROW_DMA_UNROLL, issue, 0)
    for buf_ref in (y1_ref, y2_ref):
        pltpu.make_async_copy(yb_ref.at[pl.ds(0, tm), :], buf_ref, sem).wait()

    p = p_ref[...]
    f = y1_ref[...] * p[:, 0:1] + y2_ref[...] * p[:, 1:2]
    x2 = x_ref[0] + _gated(f, gt_ref)
    o_ref[0] = _final_norm(x2, fg_ref) if final else x2


def _combine(d1, d2, x, mod, probs, fg, yb, final, tm, row_off):
    b, t, d = x.shape
    nt = t // tm
    blk_off = row_off // tm
    return pl.pallas_call(
        functools.partial(_combine_kernel, tm=tm, final=final, row_off=row_off),
        grid_spec=pltpu.PrefetchScalarGridSpec(
            num_scalar_prefetch=2,
            grid=(b, nt),
            in_specs=[pl.BlockSpec((1, tm, d), lambda i, j, *_: (i, j, 0)),
                      pl.BlockSpec((1, mod.shape[1], d), lambda i, j, *_: (i, 0, 5)),
                      pl.BlockSpec((tm, LANES), lambda i, j, *_: (blk_off + i * nt + j, 0)),
                      pl.BlockSpec((1, d), lambda i, j, *_: (0, 0)),
                      pl.BlockSpec(memory_space=pl.ANY)],
            out_specs=pl.BlockSpec((1, tm, d), lambda i, j, *_: (i, j, 0)),
            scratch_shapes=[pltpu.VMEM((tm, d), F32), pltpu.VMEM((tm, d), F32), pltpu.SemaphoreType.DMA]),
        out_shape=jax.ShapeDtypeStruct((b, t, d), F32),
        compiler_params=_cparams(("arbitrary", "arbitrary"), 40),
        name="moe_combine",
    )(d1, d2, x, mod, probs, fg, yb)


def _moe(h2, logits, trunks, wg, wu, wd, fg, final):
    n, d = h2.shape
    tile = lambda cap: max(r for r in (cap, cap // 2, cap // 4, cap // 8, cap // 16) if n % r == 0)
    info, probs, cnt = _route(logits, tile(512))
    counts = cnt[0, :N_EXPERTS]
    padded = (counts + MOE_ROWS - 1) // MOE_ROWS * MOE_ROWS
    ep = jnp.cumsum(padded).astype(jnp.int32)
    sp = ep - padded
    n_blocks = -(-(2 * n) // MOE_ROWS) + N_EXPERTS
    n_used = jnp.maximum(ep[-1] // MOE_ROWS, 1).astype(jnp.int32)
    blk_start = jnp.minimum(jnp.arange(n_blocks, dtype=jnp.int32), n_used - 1) * MOE_ROWS
    block_e = jnp.minimum(jnp.sum(blk_start[:, None] >= ep[None, :], axis=1), N_EXPERTS - 1).astype(jnp.int32)
    dest = _dest(info, jnp.pad(sp, (0, LANES - N_EXPERTS))[None, :], tile(1024))
    d1, d2 = dest[:, 0], dest[:, 1]
    xb = _dispatch(d1, d2, sp, ep, h2, n_blocks * MOE_ROWS, tile(1024))
    yb = _experts(block_e, n_used.reshape(1), xb, wg, wu, wd, MOE_FF_TILE)
    return [_combine(d1, d2, x1, mod, probs, fg, yb, final, tm, off) for x1, mod, tm, off in trunks]


def _pad_rows_front(a, rows):
    pad = [(0, 0)] * a.ndim
    pad[-2] = (rows - a.shape[-2], 0)
    return jnp.pad(a, pad)


def _trunks(xs, mods, states, w):
    depth = w["wa"].shape[0]
    n_tr = len(xs)
    shapes = [x.shape for x in xs]
    d = shapes[0][2]
    offs = [sum(s[0] * s[1] for s in shapes[:i]) for i in range(n_tr)]
    n_all = sum(s[0] * s[1] for s in shapes)
    tms, mvs = [], []
    for (b, t, _), m in zip(shapes, mods):
        if t < TOKEN_TILE and TOKEN_TILE % t == 0 and (b * t) % TOKEN_TILE == 0:
            nseg = TOKEN_TILE // t
            tms.append(TOKEN_TILE)
            mvs.append(m.reshape(depth, b // nseg, nseg, m.shape[-1]))
        else:
            tms.append(min(t, TOKEN_TILE))
            mvs.append(m)
    wide = [2 * tm if m.shape[2] == 1 and s[1] % (2 * tm) == 0 and (n_all - s[0] * s[1]) % (2 * tm) == 0 else tm
            for tm, m, s in zip(tms, mvs, shapes)]
    tok = lambda a, i: a.reshape(-1, tms[i] if shapes[i][1] < tms[i] else shapes[i][1], a.shape[-1])
    seq = lambda a, i: a.reshape(shapes[i][0], shapes[i][1], a.shape[-1])
    xs = [tok(x, i) for i, x in enumerate(xs)]
    new_states = [([], [], [], []) for _ in xs]
    for l in range(depth):
        final = l == depth - 1
        j = l // 2
        x1s, flat = [], None
        for i, x in enumerate(xs):
            t = shapes[i][1]
            ml = mvs[i][l]
            conv0, gdn0, shift0, rwkv0 = states[i]
            pa, pab, pb = (seq(a, i) for a in _norm_proj(x, ml, w["norm1_g"][l], w["wa"][l], w["wab"][l],
                                                         w["wb"][l], tms[i]))
            oa, sg = _gdn(pa, pab, _pad_rows_front(conv0[l], SUBLANES), gdn0[l], w["conv_w"][l], w["alog"][l],
                          w["dtb"][l], w["onorm_g"][l], min(t, GDN_TILE))
            ob, sr = _rwkv(pb, _pad_rows_front(shift0[l], SUBLANES), rwkv0[l], [v[l] for v in w["rwkv_vecs"]],
                           w["w2p"][l], w["a2p"][l], w["g2"][l], w["bd"], min(t, RWKV_TILE))
            for lst, val in zip(new_states[i], (pa[:, t - (GDN_CONV - 1):, :w["conv_w"].shape[2]], sg,
                                                pb[:, t - 1:, :], sr)):
                lst.append(val)
            oa, ob = tok(oa, i), tok(ob, i)
            if l % 2 == 0:
                x1, h2 = _out_proj(oa, ob, x, ml, w["norm2_g"][l], w["woa"][l], w["wob"][l], wide[i])
                xs[i] = _ffn(h2, x1, ml, w["ffn_g"][j], w["ffn_u"][j], w["ffn_d"][j], w["final_g"], final, tms[i])
            else:
                x1, *flat = _out_proj(oa, ob, x, ml, w["norm2_g"][l], w["woa"][l], w["wob"][l], wide[i],
                                      router=w["router"][j], flat_rows=n_all, row_off=offs[i], flat_prev=flat)
                x1s.append(x1)
        if l % 2 == 1:
            trunks = [(x1s[i], mvs[i][l], wide[i], offs[i]) for i in range(n_tr)]
            xs = _moe(flat[0], flat[1], trunks, w["moe_g"][j], w["moe_u"][j], w["moe_d"][j], w["final_g"], final)
    return [seq(x, i) for i, x in enumerate(xs)], [tuple(jnp.stack(s) for s in st) for st in new_states]


def kernel(x_prompt, x_sample, c_prompt, c_sample, state_gdn_conv, state_gdn, state_rwkv_shift, state_rwkv, w_ada, b_ada, norm1_g, norm2_g, w_in, gdn_conv_w, gdn_a_log, gdn_dt_bias, gdn_onorm_g, rwkv_mu, rwkv_w0, rwkv_w2, rwkv_a0, rwkv_a2, rwkv_g2, rwkv_k_k, rwkv_k_a, rwkv_r_k, rwkv_ln_w, rwkv_ln_b, w_out, ffn_w_gate, ffn_w_up, ffn_w_down, moe_router, moe_w_gate, moe_w_up, moe_w_down, final_g):
    depth, d, _ = w_in.shape
    nbp = x_prompt.shape[0]
    nbs = x_sample.shape[0]
    nqkvz = 4 * GDN_HEADS * GDN_D
    nab = 2 * GDN_HEADS
    nw = RWKV_HEADS * RWKV_N
    lora_w = rwkv_w2.shape[1]

    def lane_pad(v):
        return jnp.pad(v, ((0, 0), (0, LANES - v.shape[1])))[:, None, :]

    rows = lambda v: v[:, None, :]
    hi = jnp.arange(nw) // RWKV_N
    w = dict(
        norm1_g=rows(norm1_g), norm2_g=rows(norm2_g), final_g=final_g[None, :],
        wa=w_in[:, :, :nqkvz].astype(BF16),
        wab=jnp.pad(w_in[:, :, nqkvz:nqkvz + nab], ((0, 0), (0, 0), (0, LANES - nab))).astype(BF16),
        wb=w_in[:, :, nqkvz + nab:].astype(BF16),
        conv_w=gdn_conv_w, alog=lane_pad(gdn_a_log), dtb=lane_pad(gdn_dt_bias), onorm_g=rows(gdn_onorm_g),
        rwkv_vecs=[rows(rwkv_mu), rows(rwkv_w0), rows(rwkv_a0), rows(rwkv_k_k), rows(rwkv_k_a),
                   rwkv_r_k.reshape(depth, 1, nw), rows(rwkv_ln_w), rows(rwkv_ln_b)],
        w2p=jnp.pad(rwkv_w2, ((0, 0), (0, LANES - lora_w), (0, 0))),
        a2p=jnp.pad(rwkv_a2, ((0, 0), (lora_w, LANES - lora_w - rwkv_a2.shape[1]), (0, 0))),
        g2=rwkv_g2,
        bd=(hi[:, None] == hi[None, :]).astype(F32),
        woa=w_out[:, :GDN_HEADS * GDN_D, :].astype(BF16), wob=w_out[:, GDN_HEADS * GDN_D:, :].astype(BF16),
        ffn_g=ffn_w_gate.astype(BF16), ffn_u=ffn_w_up.astype(BF16), ffn_d=ffn_w_down.astype(BF16),
        router=jnp.pad(moe_router, ((0, 0), (0, 0), (0, LANES - moe_router.shape[2]))),
        moe_g=moe_w_gate.astype(BF16), moe_u=moe_w_up.astype(BF16), moe_d=moe_w_down.astype(BF16),
    )

    nb_all = nbp + nbs
    bp = -(-nb_all // SUBLANES) * SUBLANES
    c_all = jnp.pad(jnp.concatenate([c_prompt, c_sample], axis=0), ((0, bp - nb_all), (0, 0)))
    mod = _ada_mod(c_all, w_ada, b_ada)[:, :, None, :]
    mod_p, mod_s = mod[:, :nbp], mod[:, nbp:nb_all]

    dt = x_prompt.dtype
    zc = jnp.zeros((depth, nbp) + state_gdn_conv.shape[2:], dt)
    zg = jnp.zeros((depth, nbp) + state_gdn.shape[2:], dt)
    zs = jnp.zeros((depth, nbp) + state_rwkv_shift.shape[2:], dt)
    zr = jnp.zeros((depth, nbp) + state_rwkv.shape[2:], dt)
    (y_p, y_s), (st_p, st_s) = _trunks(
        [x_prompt, x_sample], [mod_p, mod_s],
        [(zc, zg, zs, zr), (state_gdn_conv, state_gdn, state_rwkv_shift, state_rwkv)], w)
    return (y_p, y_s) + st_p + st_s
```

```python
import functools

import jax
import jax.numpy as jnp
from jax import lax
from jax.experimental import pallas as pl
from jax.experimental.pallas import tpu as pltpu

F32 = jnp.float32
BF16 = jnp.bfloat16
HI = lax.Precision.HIGHEST

LANES = 128
SUBLANES = 8
CHUNK = 64
GDN_HEADS = 4
GDN_D = 128
RWKV_HEADS = 8
RWKV_N = 64
GDN_CONV = 4
N_EXPERTS = 8
NORM_EPS = 1e-6
L2_EPS = 1e-6
GN_EPS = 64e-5
NEG_BIG = -1e30
RWKV_DECAY_SCALE = 0.6065306597126334
TOKEN_TILE = 512
GDN_TILE = 512
RWKV_TILE = 256
SHORT_SEQ_BATCH = 4
MOE_ROWS = 512
MOE_FF_TILE = 1792
MIB = 1024 * 1024

P_GDN_SCORE = 1
P_GDN_INV = 3
P_GDN_SOLVE = 1
P_GDN_STATE = 1
P_RWKV_SMALL = 1
P_RWKV_INV = 1
P_RWKV_STATE = 1
P_ROUTER = 3


def _cparams(sem, vmem_mib):
    return pltpu.CompilerParams(dimension_semantics=sem, vmem_limit_bytes=vmem_mib * MIB)


def _split_bf16(x):
    hi = x.astype(BF16)
    return hi, (x - hi.astype(F32)).astype(BF16)


def _split3_bf16(x):
    hi = x.astype(BF16)
    r1 = x - hi.astype(F32)
    mid = r1.astype(BF16)
    return hi, mid, (r1 - mid.astype(F32)).astype(BF16)


_CONTRACT = {"nn": (1, 0), "nt": (1, 1), "tn": (0, 0)}


def _dg(a, b, kind, prec):
    off = a.ndim - 2
    ca, cb = _CONTRACT[kind]
    dn = (((ca + off,), (cb + off,)), (((0,), (0,)) if off else ((), ())))
    dot = lambda x, y: lax.dot_general(x, y, dn, preferred_element_type=F32)
    if prec is None or prec is HI:
        return lax.dot_general(a, b, dn, preferred_element_type=F32, precision=prec)
    if prec == 1:
        return dot(a.astype(BF16), b.astype(BF16))
    if prec == "l3":
        bb = b.astype(BF16)
        a0, a1, a2 = _split3_bf16(a)
        return dot(a0, bb) + dot(a1, bb) + dot(a2, bb)
    if prec == "r3":
        ab = a.astype(BF16)
        b0, b1, b2 = _split3_bf16(b)
        return dot(ab, b0) + dot(ab, b1) + dot(ab, b2)
    ah, al = _split_bf16(a)
    bh, bl = _split_bf16(b)
    return dot(ah, bh) + dot(ah, bl) + dot(al, bh)


def _mm(a, b, prec=None):
    return _dg(a, b, "nn", prec)


def _mm_nt(a, b, prec=None):
    return _dg(a, b, "nt", prec)


def _mm_tn(a, b, prec=None):
    return _dg(a, b, "tn", prec)


def _silu(x):
    return x * jax.nn.sigmoid(x)


def _softplus(x):
    return jnp.maximum(x, 0.0) + jnp.log1p(jnp.exp(-jnp.abs(x)))


def _iota2(shape, dim):
    return lax.broadcasted_iota(jnp.int32, shape, dim)


def _per_segment(x, fn, *mod_refs):
    mods = [r[0] for r in mod_refs]
    nseg = mods[0].shape[0]
    if nseg == 1:
        return fn(x, *mods)
    rows = x.shape[0] // nseg
    return jnp.concatenate([fn(x[s * rows:(s + 1) * rows], *(m[s:s + 1] for m in mods)) for s in range(nseg)], axis=0)


def _scale_shift(xn, sc_ref, sh_ref):
    return _per_segment(xn, lambda x, sc, sh: x * (1.0 + sc) + sh, sc_ref, sh_ref)


def _gated(f, gt_ref):
    return _per_segment(f, lambda x, gt: gt * x, gt_ref)


def _shift_rows(x, prev8, k):
    head = jnp.where(_iota2(prev8.shape, 0) < k, pltpu.roll(prev8, k, 0), pltpu.roll(x[0:SUBLANES], k, 0))
    return jnp.concatenate([head, pltpu.roll(x, k, 0)[SUBLANES:]], axis=0)


def _causal_taps(x, w, prev8, ntap):
    y = x * w[ntap - 1:ntap, :]
    for k in range(1, ntap):
        y = y + _shift_rows(x, prev8, k) * w[ntap - 1 - k:ntap - k, :]
    return y


def _neumann_inverse(x, eye, prec):
    c = x.shape[-1]
    z = jnp.concatenate([x, jnp.broadcast_to(eye, x.shape)], axis=-1)
    keep_s = _iota2((c, 2 * c), 1) >= c
    for _ in range(6):
        z = _mm(z[..., :c], z, prec) + jnp.where(keep_s, z, 0.0)
    return z[..., c:]


def _ada_kernel(c_ref, w_ref, b_ref, o_ref):
    o_ref[0] = _mm(_silu(c_ref[...]), w_ref[0], 3) + b_ref[0]


def _ada_mod(c_all, w_ada, b_ada):
    depth, d, n6 = w_ada.shape
    bp = c_all.shape[0]
    tn = 1536
    return pl.pallas_call(
        _ada_kernel,
        grid=(depth, n6 // tn),
        in_specs=[pl.BlockSpec((bp, d), lambda l, j: (0, 0)),
                  pl.BlockSpec((1, d, tn), lambda l, j: (l, 0, j)),
                  pl.BlockSpec((1, 1, tn), lambda l, j: (l, 0, j))],
        out_specs=pl.BlockSpec((1, bp, tn), lambda l, j: (l, 0, j)),
        out_shape=jax.ShapeDtypeStruct((depth, bp, n6), F32),
        compiler_params=_cparams(("arbitrary", "arbitrary"), 40),
        name="ada_mod",
    )(c_all, w_ada, b_ada.reshape(depth, 1, n6))


def _norm_proj_kernel(x_ref, sh_ref, sc_ref, g_ref, wa_ref, wab_ref, wb_ref, pa_ref, pab_ref, pb_ref):
    x = x_ref[0]
    rs = lax.rsqrt(jnp.mean(x * x, axis=-1, keepdims=True) + NORM_EPS)
    hb = _scale_shift(x * rs * g_ref[...], sc_ref, sh_ref).astype(BF16)
    pa_ref[0] = _mm(hb, wa_ref[...])
    pab_ref[0] = _mm(hb, wab_ref[...])
    pb_ref[0] = _mm(hb, wb_ref[...])


def _norm_proj(x, mod, g, wa, wab, wb, tm):
    b, t, d = x.shape
    na, nab, nb = wa.shape[1], wab.shape[1], wb.shape[1]
    nseg = mod.shape[1]
    row = lambda i, j: (i, j, 0)
    const = lambda i, j: (0, 0)
    return pl.pallas_call(
        _norm_proj_kernel,
        grid=(b, t // tm),
        in_specs=[pl.BlockSpec((1, tm, d), row),
                  pl.BlockSpec((1, nseg, d), lambda i, j: (i, 0, 0)),
                  pl.BlockSpec((1, nseg, d), lambda i, j: (i, 0, 1)),
                  pl.BlockSpec((1, d), const),
                  pl.BlockSpec((d, na), const),
                  pl.BlockSpec((d, nab), const),
                  pl.BlockSpec((d, nb), const)],
        out_specs=[pl.BlockSpec((1, tm, na), row), pl.BlockSpec((1, tm, nab), row), pl.BlockSpec((1, tm, nb), row)],
        out_shape=[jax.ShapeDtypeStruct((b, t, na), F32), jax.ShapeDtypeStruct((b, t, nab), F32),
                   jax.ShapeDtypeStruct((b, t, nb), F32)],
        compiler_params=_cparams(("arbitrary", "arbitrary"), 48),
        name="norm_proj",
    )(x, mod, mod, g, wa, wab, wb)


def _gdn_kernel(pa_ref, pab_ref, cbuf_ref, s0_ref, cw_ref, alog_ref, dtb_ref, og_ref,
                o_ref, sfin_ref, xc_ref, s_ref, *, tc):
    t = pl.program_id(1)
    nqk = GDN_HEADS * GDN_D
    nconv = 3 * nqk

    nsq = pa_ref.shape[0]
    nh = GDN_HEADS

    @pl.when(t == 0)
    def _():
        xc_ref[...] = cbuf_ref[...]
        s_ref[...] = s0_ref[...].reshape(s_ref.shape)

    ys = []
    for sq in range(nsq):
        x = pa_ref[sq, :, 0:nconv]
        ys.append(_causal_taps(x, cw_ref[...], xc_ref[sq], GDN_CONV))
        xc_ref[sq] = x[tc - SUBLANES:tc, :]
    qkv = _silu(ys[0] if nsq == 1 else jnp.concatenate(ys, axis=0))

    ab = pab_ref[...].reshape(nsq * tc, LANES)
    gmat = -jnp.exp(alog_ref[...]) * _softplus(ab + dtb_ref[...])
    bmat = jax.nn.sigmoid(ab)

    c = CHUNK
    nch = nsq * tc // c
    nb = nch * nh
    ri = _iota2((c, c), 0)
    ci = _iota2((c, c), 1)
    incl = ri >= ci
    strict = ri > ci
    eye_c = (ri == ci).astype(F32)

    def heads(x, width):
        x = x.reshape(nch, c, nh * width)
        return jnp.stack([x[:, :, h * width:(h + 1) * width] for h in range(nh)], axis=1).reshape(nb, c, width)

    q = heads(qkv[:, 0:nqk], GDN_D)
    k = heads(qkv[:, nqk:2 * nqk], GDN_D)
    v = heads(qkv[:, 2 * nqk:3 * nqk], GDN_D)
    ones_d = jnp.ones((GDN_D, GDN_D), BF16)

    def l2n(x, scale):
        ss = _mm((x * x).reshape(nb * c, GDN_D), ones_d, 1).reshape(nb, c, GDN_D)
        return x * (lax.rsqrt(ss + L2_EPS) * scale)

    q = l2n(q, GDN_D ** -0.5)
    k = l2n(k, 1.0)
    beta = heads(bmat[:, nh:2 * nh], 1)

    g4 = gmat.reshape(nch, c, LANES)
    gcol4 = _mm(jnp.broadcast_to(incl.astype(F32), (nch, c, c)), g4, "r3")
    grow4 = _mm_tn(g4, jnp.broadcast_to((ri <= ci).astype(F32), (nch, c, c)), "l3")
    gc = jnp.stack([gcol4[:, :, h:h + 1] for h in range(nh)], axis=1).reshape(nb, c, 1)
    gr = jnp.stack([grow4[:, h:h + 1, :] for h in range(nh)], axis=1).reshape(nb, 1, c)
    decay = jnp.exp(jnp.where(incl, gc - gr, NEG_BIG))

    kb = k * beta
    m = jnp.where(strict, _mm_nt(kb, k, P_GDN_SCORE) * decay, 0.0)
    t0 = _neumann_inverse(-m, eye_c, 1)
    tinv = t0 + _mm(t0, (eye_c - t0) - _mm(m, t0, P_GDN_INV), 1)
    egc = jnp.exp(gc)
    sol = _mm(tinv, jnp.concatenate([v * beta, kb * egc], axis=2), P_GDN_SOLVE)
    u, w = sol[:, :, :GDN_D], sol[:, :, GDN_D:]
    aqk = _mm_nt(q, k, P_GDN_SCORE) * decay
    gl = gc[:, c - 1:c, :]
    kd = k * jnp.exp(gl - gc)
    qg = q * egc
    egl = jnp.exp(gl)

    s = s_ref[...]
    outs = []
    for ch in range(tc // c):
        bs = slice(ch * nh, (ch + 1) * nh) if nsq == 1 else slice(None)
        v_new = u[bs] - _mm(w[bs], s, P_GDN_STATE)
        outs.append(_mm(qg[bs], s, P_GDN_STATE) + _mm(aqk[bs], v_new, P_GDN_STATE))
        s = s * egl[bs] + _mm_tn(kd[bs], v_new, P_GDN_STATE)
    s_ref[...] = s

    o = jnp.stack(outs, axis=0).reshape(nch, nh, c, GDN_D)
    o = o * lax.rsqrt(jnp.mean(o * o, axis=-1, keepdims=True) + NORM_EPS) * og_ref[...]
    for h in range(nh):
        sl = slice(h * GDN_D, (h + 1) * GDN_D)
        z = pa_ref[:, :, nconv + h * GDN_D: nconv + (h + 1) * GDN_D]
        o_ref[:, :, sl] = (o[:, h].reshape(nsq, tc, GDN_D) * _silu(z)).astype(o_ref.dtype)

    @pl.when(t == pl.num_programs(1) - 1)
    def _():
        sfin_ref[...] = s_ref[...].reshape(sfin_ref.shape)


def _seqs_per_step(b, t, tc):
    return max(n for n in (SHORT_SEQ_BATCH, 2, 1) if b % n == 0) if t == tc == CHUNK else 1


def _gdn(pa, pab, cbuf8, s0, cw, alog, dtb, og, tc):
    b, t, na = pa.shape
    nconv = cw.shape[1]
    nsq = _seqs_per_step(b, t, tc)
    row = lambda i, j: (i, j, 0)
    const2 = lambda i, j: (0, 0)
    return pl.pallas_call(
        functools.partial(_gdn_kernel, tc=tc),
        grid=(b // nsq, t // tc),
        in_specs=[pl.BlockSpec((nsq, tc, na), row),
                  pl.BlockSpec((nsq, tc, LANES), row),
                  pl.BlockSpec((nsq, SUBLANES, nconv), lambda i, j: (i, 0, 0)),
                  pl.BlockSpec((nsq, GDN_HEADS, GDN_D, GDN_D), lambda i, j: (i, 0, 0, 0)),
                  pl.BlockSpec((GDN_CONV, nconv), const2),
                  pl.BlockSpec((1, LANES), const2),
                  pl.BlockSpec((1, LANES), const2),
                  pl.BlockSpec((1, GDN_D), const2)],
        out_specs=[pl.BlockSpec((nsq, tc, GDN_HEADS * GDN_D), row),
                   pl.BlockSpec((nsq, GDN_HEADS, GDN_D, GDN_D), lambda i, j: (i, 0, 0, 0))],
        out_shape=[jax.ShapeDtypeStruct((b, t, GDN_HEADS * GDN_D), BF16),
                   jax.ShapeDtypeStruct((b, GDN_HEADS, GDN_D, GDN_D), F32)],
        scratch_shapes=[pltpu.VMEM((nsq, SUBLANES, nconv), F32),
                        pltpu.VMEM((nsq * GDN_HEADS, GDN_D, GDN_D), F32)],
        compiler_params=_cparams(("arbitrary", "arbitrary"), 40),
        name="gdn_mixer",
    )(pa, pab, cbuf8, s0, cw, alog, dtb, og)


def _rwkv_kernel(pb_ref, sbuf_ref, s0_ref, mu_ref, w0_ref, w2_ref, a0_ref, a2_ref, g2_ref, kk_ref, ka_ref,
                 rk_ref, lnw_ref, lnb_ref, bd_ref, o_ref, sfin_ref, xc_ref, s_ref, y_ref, *, tc):
    t = pl.program_id(1)
    nw = RWKV_HEADS * RWKV_N

    nsq = pb_ref.shape[0]
    nh = RWKV_HEADS

    @pl.when(t == 0)
    def _():
        xc_ref[...] = sbuf_ref[...]
        s_ref[...] = s0_ref[...].reshape(s_ref.shape)

    xl, pl_ = [], []
    for sq in range(nsq):
        xq = pb_ref[sq]
        xl.append(xq)
        pl_.append(_shift_rows(xq, xc_ref[sq], 1))
        xc_ref[sq] = xq[tc - SUBLANES:tc, :]
    x = xl[0] if nsq == 1 else jnp.concatenate(xl, axis=0)
    prev = pl_[0] if nsq == 1 else jnp.concatenate(pl_, axis=0)
    xs = x + (prev - x) * mu_ref[...]
    r = xs[:, 0:nw]
    kr = xs[:, nw:2 * nw]
    vr = xs[:, 2 * nw:3 * nw]
    wa = xs[:, 3 * nw:3 * nw + LANES]
    gd = xs[:, 3 * nw + LANES:3 * nw + 2 * LANES]

    bd = bd_ref[...]
    ps = P_RWKV_SMALL
    lw = -RWKV_DECAY_SCALE * jax.nn.sigmoid(w0_ref[...] + _mm(jnp.tanh(wa), w2_ref[...], ps))
    a = jax.nn.sigmoid(a0_ref[...] + _mm(wa, a2_ref[...], ps))
    gate = _mm(jax.nn.sigmoid(gd), g2_ref[...], ps)
    kkr = kr * kk_ref[...]
    kk = kkr * lax.rsqrt(_mm(kkr * kkr, bd, ps) + L2_EPS)
    kb = kr * (1.0 + (a - 1.0) * ka_ref[...])
    aa = -kk
    bb = kk * a
    bonus = _mm(r * kb * rk_ref[...], bd, ps) * vr

    c = CHUNK
    n = RWKV_N
    nch = nsq * tc // c
    nb = nch * nh
    ri = _iota2((c, c), 0)
    ci = _iota2((c, c), 1)
    incl = ri >= ci
    eye = ri == ci
    eye_c = eye.astype(F32)

    def heads(x):
        return jnp.stack([x[:, :, h * n:(h + 1) * n] for h in range(nh)], axis=1).reshape(nb, x.shape[1], n)

    chunks = lambda x: x.reshape(nch, c, nw)
    lw4 = chunks(lw)
    g = _mm(jnp.broadcast_to(incl.astype(F32), (nch, c, c)), lw4, "r3")
    gfin = g[:, c - 1:c, :]
    eng = jnp.exp(-g)
    efin = jnp.exp(gfin)
    tail = efin * eng
    aa4, bb4, kb4 = chunks(aa), chunks(bb), chunks(kb)
    at = heads(aa4 * jnp.exp(g - lw4))
    bt = heads(bb4 * eng)
    kt = heads(kb4 * eng)
    rt = heads(chunks(r) * jnp.exp(g))
    bh = heads(bb4 * tail)
    kh = heads(kb4 * tail)
    egf = heads(efin)
    v4 = chunks(vr)
    upper = _iota2((c, 2 * n), 1) >= n
    vcols = [v4[:, :, (h // 2) * 2 * n:(h // 2 + 1) * 2 * n] for h in range(nh)]
    vhi = jnp.stack([jnp.where(upper, col if h % 2 else pltpu.roll(col, n, 2), 0.0)
                     for h, col in enumerate(vcols)], axis=1).reshape(nb, c, 2 * n)
    zeros_lo = jnp.zeros((nb, c, n), F32)
    pad_hi = lambda x: jnp.concatenate([x, zeros_lo], axis=-1)

    h16 = lambda x: x.astype(BF16)
    colj = _iota2((c, 2 * c), 1) % c
    rowi = _iota2((c, 2 * c), 0)
    vhi16 = h16(vhi)
    ar = jnp.concatenate([h16(at), h16(rt)], axis=1)
    xbk = _mm_nt(ar, jnp.concatenate([h16(bt), h16(kt)], axis=1))
    l2 = jnp.where(rowi > colj, xbk[:, :c], 0.0)
    m2 = jnp.where(rowi >= colj, xbk[:, c:], 0.0)
    tinv = _neumann_inverse(l2[..., :c], eye_c, P_RWKV_INV)
    lv = _mm(h16(l2), jnp.concatenate([jnp.zeros_like(vhi16), vhi16], axis=1))
    wu = _mm(h16(tinv), h16(lv + pad_hi(at)))
    wv = jnp.concatenate([h16(wu), vhi16], axis=1)
    qy = _mm(h16(m2), wv) + pad_hi(rt)
    pi = _mm_tn(wv, jnp.concatenate([h16(bh), h16(kh)], axis=1))
    pt = jnp.where(eye, egf, 0.0) + pi[:, :n]
    inc = pi[:, n:]

    s = s_ref[...]
    zeros_s = jnp.zeros_like(s)
    ys = []
    for ch in range(tc // c):
        bs = slice(ch * nh, (ch + 1) * nh) if nsq == 1 else slice(None)
        ys.append(_mm_nt(qy[bs][..., :n], jnp.concatenate([zeros_s, s], axis=1), P_RWKV_STATE) + qy[bs])
        s = _mm(s, pt[bs], P_RWKV_STATE) + inc[bs]
    s_ref[...] = s
    y4 = jnp.stack(ys, axis=0).reshape(nch, nh, c, 2 * n)
    for h in range(nh):
        y_ref[:, h * n:(h + 1) * n] = y4[:, h, :, n:].reshape(nsq * tc, n)

    y = y_ref[...]
    mean = _mm(y, bd, ps) * (1.0 / n)
    dlt = y - mean
    var = _mm(dlt * dlt, bd, ps) * (1.0 / n)
    yn = dlt * lax.rsqrt(var + GN_EPS) * lnw_ref[...] + lnb_ref[...]
    o_ref[...] = ((yn + bonus) * gate).reshape(o_ref.shape).astype(o_ref.dtype)

    @pl.when(t == pl.num_programs(1) - 1)
    def _():
        sfin_ref[...] = s_ref[...].reshape(sfin_ref.shape)


def _rwkv(pb, sbuf8, s0, vecs, w2p, a2p, g2, bd, tc):
    b, t, nb = pb.shape
    nw = RWKV_HEADS * RWKV_N
    nsq = _seqs_per_step(b, t, tc)
    row = lambda i, j: (i, j, 0)
    const2 = lambda i, j: (0, 0)
    mu, w0, a0, kk, ka, rk, lnw, lnb = vecs
    vec_spec = pl.BlockSpec((1, nw), const2)
    return pl.pallas_call(
        functools.partial(_rwkv_kernel, tc=tc),
        grid=(b // nsq, t // tc),
        in_specs=[pl.BlockSpec((nsq, tc, nb), row),
                  pl.BlockSpec((nsq, SUBLANES, nb), lambda i, j: (i, 0, 0)),
                  pl.BlockSpec((nsq, RWKV_HEADS, RWKV_N, RWKV_N), lambda i, j: (i, 0, 0, 0)),
                  pl.BlockSpec((1, nb), const2),
                  vec_spec,
                  pl.BlockSpec((LANES, nw), const2),
                  vec_spec,
                  pl.BlockSpec((LANES, nw), const2),
                  pl.BlockSpec((LANES, nw), const2),
                  vec_spec, vec_spec, vec_spec, vec_spec, vec_spec,
                  pl.BlockSpec((nw, nw), const2)],
        out_specs=[pl.BlockSpec((nsq, tc, nw), row),
                   pl.BlockSpec((nsq, RWKV_HEADS, RWKV_N, RWKV_N), lambda i, j: (i, 0, 0, 0))],
        out_shape=[jax.ShapeDtypeStruct((b, t, nw), BF16),
                   jax.ShapeDtypeStruct((b, RWKV_HEADS, RWKV_N, RWKV_N), F32)],
        scratch_shapes=[pltpu.VMEM((nsq, SUBLANES, nb), F32),
                        pltpu.VMEM((nsq * RWKV_HEADS, RWKV_N, RWKV_N), F32),
                        pltpu.VMEM((nsq * tc, nw), F32)],
        compiler_params=_cparams(("arbitrary", "arbitrary"), 40),
        name="rwkv_mixer",
    )(pb, sbuf8, s0, mu, w0, w2p, a0, a2p, g2, kk, ka, rk, lnw, lnb, bd)


def _out_proj_kernel(oa_ref, ob_ref, x_ref, gt_ref, sh_ref, sc_ref, g_ref, woa_ref, wob_ref, *rest, with_router, nt):
    if with_router:
        wr_ref = rest[0]
        x1_ref, h_ref, lg_ref = rest[-3:]
    else:
        x1_ref, h_ref = rest

    @pl.when(pl.program_id(1) < nt)
    def _():
        mix = _mm(oa_ref[0], woa_ref[...]) + _mm(ob_ref[0], wob_ref[...])
        x1 = x_ref[0] + _gated(mix, gt_ref)
        x1_ref[0] = x1
        rs = lax.rsqrt(jnp.mean(x1 * x1, axis=-1, keepdims=True) + NORM_EPS)
        h = _scale_shift(x1 * rs * g_ref[...], sc_ref, sh_ref)
        if with_router:
            h_ref[...] = h
            lg_ref[...] = _mm(h, wr_ref[...], P_ROUTER)
        else:
            h_ref[0] = h.astype(h_ref.dtype)

    if with_router:
        @pl.when(pl.program_id(1) >= nt)
        def _():
            h_ref[...] = jnp.zeros_like(h_ref)
            lg_ref[...] = jnp.zeros_like(lg_ref)


def _out_proj(oa, ob, x, mod, g, woa, wob, tm, router=None, flat_rows=None, row_off=0, flat_prev=None):
    b, t, d = x.shape
    nh = oa.shape[2]
    nt = t // tm
    nseg = mod.shape[1]
    tail = 0
    if router is not None and flat_prev is None:
        assert b == 1 and row_off == 0 and (flat_rows - t) % tm == 0
        tail = (flat_rows - t) // tm
    row = lambda i, j: (i, jnp.minimum(j, nt - 1), 0)
    const = lambda i, j: (0, 0)
    in_specs = [pl.BlockSpec((1, tm, nh), row), pl.BlockSpec((1, tm, nh), row), pl.BlockSpec((1, tm, d), row),
                pl.BlockSpec((1, nseg, d), lambda i, j: (i, 0, 2)),
                pl.BlockSpec((1, nseg, d), lambda i, j: (i, 0, 3)),
                pl.BlockSpec((1, nseg, d), lambda i, j: (i, 0, 4)),
                pl.BlockSpec((1, d), const), pl.BlockSpec((nh, d), const), pl.BlockSpec((nh, d), const)]
    args = [oa, ob, x, mod, mod, mod, g, woa, wob]
    aliases = {}
    if router is None:
        out_specs = [pl.BlockSpec((1, tm, d), row), pl.BlockSpec((1, tm, d), row)]
        out_shape = [jax.ShapeDtypeStruct((b, t, d), F32), jax.ShapeDtypeStruct((b, t, d), BF16)]
    else:
        blk_off = row_off // tm
        flat = lambda i, j: (blk_off + i * nt + j, 0)
        in_specs.append(pl.BlockSpec((d, LANES), const))
        args.append(router)
        if flat_prev is not None:
            aliases = {len(args): 1, len(args) + 1: 2}
            in_specs += [pl.BlockSpec(memory_space=pl.ANY)] * 2
            args += list(flat_prev)
        out_specs = [pl.BlockSpec((1, tm, d), row), pl.BlockSpec((tm, d), flat), pl.BlockSpec((tm, LANES), flat)]
        out_shape = [jax.ShapeDtypeStruct((b, t, d), F32), jax.ShapeDtypeStruct((flat_rows, d), F32),
                     jax.ShapeDtypeStruct((flat_rows, LANES), F32)]
    return pl.pallas_call(
        functools.partial(_out_proj_kernel, with_router=router is not None, nt=nt),
        grid=(b, nt + tail),
        in_specs=in_specs, out_specs=out_specs, out_shape=out_shape, input_output_aliases=aliases,
        compiler_params=_cparams(("arbitrary", "arbitrary"), 56),
        name="out_proj" if router is None else "out_proj_router",
    )(*args)


def _final_norm(x2, fg_ref):
    rs = lax.rsqrt(jnp.mean(x2 * x2, axis=-1, keepdims=True) + NORM_EPS)
    return x2 * rs * fg_ref[...]


def _ffn_kernel(h_ref, x_ref, gt_ref, wg_ref, wu_ref, wd_ref, fg_ref, o_ref, *, nf, final):
    h = h_ref[0]
    ff = wg_ref.shape[1]
    tf = ff // nf
    acc = None
    for f in range(nf):
        cols = slice(f * tf, (f + 1) * tf)
        act = (_silu(_mm(h, wg_ref[:, cols])) * _mm(h, wu_ref[:, cols])).astype(BF16)
        part = _mm(act, wd_ref[cols, :])
        acc = part if acc is None else acc + part
    x2 = x_ref[0] + _gated(acc, gt_ref)
    o_ref[0] = _final_norm(x2, fg_ref) if final else x2


def _ffn(h, x, mod, wg, wu, wd, fg, final, tm):
    b, t, d = x.shape
    ff = wg.shape[1]
    row = lambda i, j: (i, j, 0)
    const = lambda i, j: (0, 0)
    once = pl.Buffered(1)
    return pl.pallas_call(
        functools.partial(_ffn_kernel, nf=1, final=final),
        grid=(b, t // tm),
        in_specs=[pl.BlockSpec((1, tm, d), row), pl.BlockSpec((1, tm, d), row),
                  pl.BlockSpec((1, mod.shape[1], d), lambda i, j: (i, 0, 5)),
                  pl.BlockSpec((d, ff), const, pipeline_mode=once),
                  pl.BlockSpec((d, ff), const, pipeline_mode=once),
                  pl.BlockSpec((ff, d), const, pipeline_mode=once),
                  pl.BlockSpec((1, d), const)],
        out_specs=pl.BlockSpec((1, tm, d), row),
        out_shape=jax.ShapeDtypeStruct((b, t, d), F32),
        compiler_params=_cparams(("arbitrary", "arbitrary"), 56),
        name="ffn_dense",
    )(h, x, mod, wg, wu, wd, fg)


def _route_kernel(lg_ref, info_ref, p_ref, cnt_ref, carry_ref, *, tr):
    i = pl.program_id(0)

    @pl.when(i == 0)
    def _():
        carry_ref[...] = jnp.zeros_like(carry_ref)

    lane = _iota2((tr, LANES), 1)
    lg = jnp.where(lane < N_EXPERTS, lg_ref[...], NEG_BIG)
    m1 = jnp.max(lg, axis=-1, keepdims=True)
    e1 = jnp.min(jnp.where(lg == m1, lane, LANES), axis=-1, keepdims=True)
    lg2 = jnp.where(lane == e1, NEG_BIG, lg)
    m2 = jnp.max(lg2, axis=-1, keepdims=True)
    e2 = jnp.min(jnp.where(lg2 == m2, lane, LANES), axis=-1, keepdims=True)
    ex = jnp.exp(m2 - m1)
    den = 1.0 + ex
    p1 = 1.0 / den
    p2 = ex / den
    oh1 = lane == e1
    oh2 = lane == e2
    oh = jnp.where(oh1 | oh2, 1.0, 0.0)
    lstrict = (_iota2((tr, tr), 0) > _iota2((tr, tr), 1)).astype(BF16)
    cex = _mm(lstrict, oh.astype(BF16)) + carry_ref[...]
    rank1 = jnp.sum(jnp.where(oh1, cex, 0.0), axis=-1, keepdims=True).astype(jnp.int32)
    rank2 = jnp.sum(jnp.where(oh2, cex, 0.0), axis=-1, keepdims=True).astype(jnp.int32)
    carry_ref[...] = carry_ref[...] + jnp.sum(oh, axis=0, keepdims=True)
    info_ref[...] = jnp.where(lane == 0, e1, jnp.where(lane == 1, e2, jnp.where(lane == 2, rank1,
                              jnp.where(lane == 3, rank2, 0))))
    p_ref[...] = jnp.where(lane == 0, p1, jnp.where(lane == 1, p2, 0.0))

    @pl.when(i == pl.num_programs(0) - 1)
    def _():
        cnt_ref[...] = carry_ref[...].astype(jnp.int32)


def _route(logits, tr):
    n = logits.shape[0]
    blk = pl.BlockSpec((tr, LANES), lambda i: (i, 0))
    return pl.pallas_call(
        functools.partial(_route_kernel, tr=tr),
        grid=(n // tr,),
        in_specs=[blk],
        out_specs=[blk, blk, pl.BlockSpec((1, LANES), lambda i: (0, 0))],
        out_shape=[jax.ShapeDtypeStruct((n, LANES), jnp.int32), jax.ShapeDtypeStruct((n, LANES), F32),
                   jax.ShapeDtypeStruct((1, LANES), jnp.int32)],
        scratch_shapes=[pltpu.VMEM((1, LANES), F32)],
        compiler_params=_cparams(("arbitrary",), 24),
        name="moe_route",
    )(logits)


def _dest_kernel(info_ref, sp_ref, d_ref):
    info = info_ref[...]
    lane = _iota2(info.shape, 1)
    sp = sp_ref[...].astype(F32)
    start1 = jnp.sum(jnp.where(lane == info[:, 0:1], sp, 0.0), axis=-1, keepdims=True).astype(jnp.int32)
    start2 = jnp.sum(jnp.where(lane == info[:, 1:2], sp, 0.0), axis=-1, keepdims=True).astype(jnp.int32)
    d_ref[...] = jnp.where(lane == 0, start1 + info[:, 2:3], jnp.where(lane == 1, start2 + info[:, 3:4], 0))


def _dest(info, sp_row, tr):
    n = info.shape[0]
    blk = pl.BlockSpec((tr, LANES), lambda i: (i, 0))
    return pl.pallas_call(
        _dest_kernel,
        grid=(n // tr,),
        in_specs=[blk, pl.BlockSpec((1, LANES), lambda i: (0, 0))],
        out_specs=blk,
        out_shape=jax.ShapeDtypeStruct((n, LANES), jnp.int32),
        compiler_params=_cparams(("arbitrary",), 24),
        name="moe_dest",
    )(info, sp_row)


ROW_DMA_UNROLL = 16


def _dispatch_kernel(d1_ref, d2_ref, sp_ref, ep_ref, h_ref, xb_ref, zero_ref, sem, zsem, *, td):
    i = pl.program_id(0)

    @pl.when(i == 0)
    def _():
        zero_ref[...] = jnp.zeros_like(zero_ref)

        def zero_block(start):
            cp = pltpu.make_async_copy(zero_ref, xb_ref.at[pl.ds(pl.multiple_of(start, MOE_ROWS), MOE_ROWS), :], zsem)
            cp.start()
            cp.wait()

        for e in range(N_EXPERTS):
            @pl.when(ep_ref[e] > sp_ref[e])
            def _():
                zero_block(ep_ref[e] - MOE_ROWS)

        def tail(blk, carry):
            zero_block(blk * MOE_ROWS)
            return carry

        lax.fori_loop(ep_ref[N_EXPERTS - 1] // MOE_ROWS, xb_ref.shape[0] // MOE_ROWS, tail, 0)

    base = i * td

    def row_copy(j, dst):
        return pltpu.make_async_copy(h_ref.at[pl.ds(j, 1), :], xb_ref.at[pl.ds(dst, 1), :], sem)

    def issue(j8, carry):
        for u in range(ROW_DMA_UNROLL):
            j = j8 * ROW_DMA_UNROLL + u
            row_copy(j, d1_ref[base + j]).start(priority=0)
            row_copy(j, d2_ref[base + j]).start(priority=1)
        return carry

    lax.fori_loop(0, td // ROW_DMA_UNROLL, issue, 0)
    for _ in range(2):
        pltpu.make_async_copy(h_ref, xb_ref.at[pl.ds(0, td), :], sem).wait()


def _dispatch(d1, d2, sp, ep, h2, n_rows, td):
    n, d = h2.shape
    return pl.pallas_call(
        functools.partial(_dispatch_kernel, td=td),
        grid_spec=pltpu.PrefetchScalarGridSpec(
            num_scalar_prefetch=4,
            grid=(n // td,),
            in_specs=[pl.BlockSpec((td, d), lambda i, *_: (i, 0))],
            out_specs=pl.BlockSpec(memory_space=pl.ANY),
            scratch_shapes=[pltpu.VMEM((MOE_ROWS, d), F32), pltpu.SemaphoreType.DMA, pltpu.SemaphoreType.DMA]),
        out_shape=jax.ShapeDtypeStruct((n_rows, d), F32),
        compiler_params=_cparams(("arbitrary",), 24),
        name="moe_dispatch",
    )(d1, d2, sp, ep, h2)


def _expert_kernel(be_ref, nu_ref, xb_ref, wg_ref, wu_ref, wd_ref, yb_ref):
    j = pl.program_id(0)
    f = pl.program_id(1)

    @pl.when(j < nu_ref[0])
    def _():
        xg = xb_ref[...].astype(BF16)
        act = (_silu(_mm(xg, wg_ref[0])) * _mm(xg, wu_ref[0])).astype(BF16)
        part = _mm(act, wd_ref[0])

        @pl.when(f == 0)
        def _():
            yb_ref[...] = part

        @pl.when(f > 0)
        def _():
            yb_ref[...] = yb_ref[...] + part

    @pl.when((j >= nu_ref[0]) & (f == 0))
    def _():
        yb_ref[...] = jnp.zeros_like(yb_ref)


def _experts(block_e, n_used, xb, wg, wu, wd, tf):
    n_rows, d = xb.shape
    ff = wg.shape[2]
    nf = ff // tf
    n_blocks = n_rows // MOE_ROWS

    def blk(j, f, be, nu):
        return (jnp.minimum(j, nu[0] - 1), 0)

    def fcol(j, f, be, nu):
        return jnp.where(j < nu[0], f, nf - 1)

    return pl.pallas_call(
        _expert_kernel,
        grid_spec=pltpu.PrefetchScalarGridSpec(
            num_scalar_prefetch=2,
            grid=(n_blocks, nf),
            in_specs=[pl.BlockSpec((MOE_ROWS, d), blk),
                      pl.BlockSpec((1, d, tf), lambda j, f, be, nu: (be[j], 0, fcol(j, f, be, nu))),
                      pl.BlockSpec((1, d, tf), lambda j, f, be, nu: (be[j], 0, fcol(j, f, be, nu))),
                      pl.BlockSpec((1, tf, d), lambda j, f, be, nu: (be[j], fcol(j, f, be, nu), 0))],
            out_specs=pl.BlockSpec((MOE_ROWS, d), lambda j, f, be, nu: (j, 0))),
        out_shape=jax.ShapeDtypeStruct((n_rows, d), F32),
        compiler_params=_cparams(("arbitrary", "arbitrary"), 56),
        name="moe_experts",
    )(block_e, n_used, xb, wg, wu, wd)


def _combine_kernel(d1_ref, d2_ref, x_ref, gt_ref, p_ref, fg_ref, yb_ref, o_ref, y1_ref, y2_ref, sem, *,
                    tm, final, row_off):
    base = row_off + (pl.program_id(0) * pl.num_programs(1) + pl.program_id(1)) * tm

    def row_copy(src, buf_ref, j):
        return pltpu.make_async_copy(yb_ref.at[pl.ds(src, 1), :], buf_ref.at[pl.ds(j, 1), :], sem)

    def issue(j8, carry):
        for u in range(min(ROW_DMA_UNROLL, tm)):
            j = j8 * ROW_DMA_UNROLL + u
            row_copy(d1_ref[base + j], y1_ref, j).start(priority=0)
            row_copy(d2_ref[base + j], y2_ref, j).start(priority=1)
        return carry

    lax.fori_loop(0, tm // ROW_DMA_UNROLL, issue, 0)
    for buf_ref in (y1_ref, y2_ref):
        pltpu.make_async_copy(yb_ref.at[pl.ds(0, tm), :], buf_ref, sem).wait()

    p = p_ref[...]
    f = y1_ref[...] * p[:, 0:1] + y2_ref[...] * p[:, 1:2]
    x2 = x_ref[0] + _gated(f, gt_ref)
    o_ref[0] = _final_norm(x2, fg_ref) if final else x2


def _combine(d1, d2, x, mod, probs, fg, yb, final, tm, row_off):
    b, t, d = x.shape
    nt = t // tm
    blk_off = row_off // tm
    return pl.pallas_call(
        functools.partial(_combine_kernel, tm=tm, final=final, row_off=row_off),
        grid_spec=pltpu.PrefetchScalarGridSpec(
            num_scalar_prefetch=2,
            grid=(b, nt),
            in_specs=[pl.BlockSpec((1, tm, d), lambda i, j, *_: (i, j, 0)),
                      pl.BlockSpec((1, mod.shape[1], d), lambda i, j, *_: (i, 0, 5)),
                      pl.BlockSpec((tm, LANES), lambda i, j, *_: (blk_off + i * nt + j, 0)),
                      pl.BlockSpec((1, d), lambda i, j, *_: (0, 0)),
                      pl.BlockSpec(memory_space=pl.ANY)],
            out_specs=pl.BlockSpec((1, tm, d), lambda i, j, *_: (i, j, 0)),
            scratch_shapes=[pltpu.VMEM((tm, d), F32), pltpu.VMEM((tm, d), F32), pltpu.SemaphoreType.DMA]),
        out_shape=jax.ShapeDtypeStruct((b, t, d), F32),
        compiler_params=_cparams(("arbitrary", "arbitrary"), 40),
        name="moe_combine",
    )(d1, d2, x, mod, probs, fg, yb)


def _moe(h2, logits, trunks, wg, wu, wd, fg, final):
    n, d = h2.shape
    tile = lambda cap: max(r for r in (cap, cap // 2, cap // 4, cap // 8, cap // 16) if n % r == 0)
    info, probs, cnt = _route(logits, tile(512))
    counts = cnt[0, :N_EXPERTS]
    padded = (counts + MOE_ROWS - 1) // MOE_ROWS * MOE_ROWS
    ep = jnp.cumsum(padded).astype(jnp.int32)
    sp = ep - padded
    n_blocks = -(-(2 * n) // MOE_ROWS) + N_EXPERTS
    n_used = jnp.maximum(ep[-1] // MOE_ROWS, 1).astype(jnp.int32)
    blk_start = jnp.minimum(jnp.arange(n_blocks, dtype=jnp.int32), n_used - 1) * MOE_ROWS
    block_e = jnp.minimum(jnp.sum(blk_start[:, None] >= ep[None, :], axis=1), N_EXPERTS - 1).astype(jnp.int32)
    dest = _dest(info, jnp.pad(sp, (0, LANES - N_EXPERTS))[None, :], tile(1024))
    d1, d2 = dest[:, 0], dest[:, 1]
    xb = _dispatch(d1, d2, sp, ep, h2, n_blocks * MOE_ROWS, tile(1024))
    yb = _experts(block_e, n_used.reshape(1), xb, wg, wu, wd, MOE_FF_TILE)
    return [_combine(d1, d2, x1, mod, probs, fg, yb, final, tm, off) for x1, mod, tm, off in trunks]


def _pad_rows_front(a, rows):
    pad = [(0, 0)] * a.ndim
    pad[-2] = (rows - a.shape[-2], 0)
    return jnp.pad(a, pad)


def _trunks(xs, mods, states, w):
    depth = w["wa"].shape[0]
    n_tr = len(xs)
    shapes = [x.shape for x in xs]
    d = shapes[0][2]
    offs = [sum(s[0] * s[1] for s in shapes[:i]) for i in range(n_tr)]
    n_all = sum(s[0] * s[1] for s in shapes)
    tms, mvs = [], []
    for (b, t, _), m in zip(shapes, mods):
        if t < TOKEN_TILE and TOKEN_TILE % t == 0 and (b * t) % TOKEN_TILE == 0:
            nseg = TOKEN_TILE // t
            tms.append(TOKEN_TILE)
            mvs.append(m.reshape(depth, b // nseg, nseg, m.shape[-1]))
        else:
            tms.append(min(t, TOKEN_TILE))
            mvs.append(m)
    wide = [2 * tm if m.shape[2] == 1 and s[1] % (2 * tm) == 0 and (n_all - s[0] * s[1]) % (2 * tm) == 0 else tm
            for tm, m, s in zip(tms, mvs, shapes)]
    tok = lambda a, i: a.reshape(-1, tms[i] if shapes[i][1] < tms[i] else shapes[i][1], a.shape[-1])
    seq = lambda a, i: a.reshape(shapes[i][0], shapes[i][1], a.shape[-1])
    xs = [tok(x, i) for i, x in enumerate(xs)]
    new_states = [([], [], [], []) for _ in xs]
    for l in range(depth):
        final = l == depth - 1
        j = l // 2
        x1s, flat = [], None
        for i, x in enumerate(xs):
            t = shapes[i][1]
            ml = mvs[i][l]
            conv0, gdn0, shift0, rwkv0 = states[i]
            pa, pab, pb = (seq(a, i) for a in _norm_proj(x, ml, w["norm1_g"][l], w["wa"][l], w["wab"][l],
                                                         w["wb"][l], tms[i]))
            oa, sg = _gdn(pa, pab, _pad_rows_front(conv0[l], SUBLANES), gdn0[l], w["conv_w"][l], w["alog"][l],
                          w["dtb"][l], w["onorm_g"][l], min(t, GDN_TILE))
            ob, sr = _rwkv(pb, _pad_rows_front(shift0[l], SUBLANES), rwkv0[l], [v[l] for v in w["rwkv_vecs"]],
                           w["w2p"][l], w["a2p"][l], w["g2"][l], w["bd"], min(t, RWKV_TILE))
            for lst, val in zip(new_states[i], (pa[:, t - (GDN_CONV - 1):, :w["conv_w"].shape[2]], sg,
                                                pb[:, t - 1:, :], sr)):
                lst.append(val)
            oa, ob = tok(oa, i), tok(ob, i)
            if l % 2 == 0:
                x1, h2 = _out_proj(oa, ob, x, ml, w["norm2_g"][l], w["woa"][l], w["wob"][l], wide[i])
                xs[i] = _ffn(h2, x1, ml, w["ffn_g"][j], w["ffn_u"][j], w["ffn_d"][j], w["final_g"], final, tms[i])
            else:
                x1, *flat = _out_proj(oa, ob, x, ml, w["norm2_g"][l], w["woa"][l], w["wob"][l], wide[i],
                                      router=w["router"][j], flat_rows=n_all, row_off=offs[i], flat_prev=flat)
                x1s.append(x1)
        if l % 2 == 1:
            trunks = [(x1s[i], mvs[i][l], wide[i], offs[i]) for i in range(n_tr)]
            xs = _moe(flat[0], flat[1], trunks, w["moe_g"][j], w["moe_u"][j], w["moe_d"][j], w["final_g"], final)
    return [seq(x, i) for i, x in enumerate(xs)], [tuple(jnp.stack(s) for s in st) for st in new_states]


def kernel(x_prompt, x_sample, c_prompt, c_sample, state_gdn_conv, state_gdn, state_rwkv_shift, state_rwkv, w_ada, b_ada, norm1_g, norm2_g, w_in, gdn_conv_w, gdn_a_log, gdn_dt_bias, gdn_onorm_g, rwkv_mu, rwkv_w0, rwkv_w2, rwkv_a0, rwkv_a2, rwkv_g2, rwkv_k_k, rwkv_k_a, rwkv_r_k, rwkv_ln_w, rwkv_ln_b, w_out, ffn_w_gate, ffn_w_up, ffn_w_down, moe_router, moe_w_gate, moe_w_up, moe_w_down, final_g):
    depth, d, _ = w_in.shape
    nbp = x_prompt.shape[0]
    nbs = x_sample.shape[0]
    nqkvz = 4 * GDN_HEADS * GDN_D
    nab = 2 * GDN_HEADS
    nw = RWKV_HEADS * RWKV_N
    lora_w = rwkv_w2.shape[1]

    def lane_pad(v):
        return jnp.pad(v, ((0, 0), (0, LANES - v.shape[1])))[:, None, :]

    rows = lambda v: v[:, None, :]
    hi = jnp.arange(nw) // RWKV_N
    w = dict(
        norm1_g=rows(norm1_g), norm2_g=rows(norm2_g), final_g=final_g[None, :],
        wa=w_in[:, :, :nqkvz].astype(BF16),
        wab=jnp.pad(w_in[:, :, nqkvz:nqkvz + nab], ((0, 0), (0, 0), (0, LANES - nab))).astype(BF16),
        wb=w_in[:, :, nqkvz + nab:].astype(BF16),
        conv_w=gdn_conv_w, alog=lane_pad(gdn_a_log), dtb=lane_pad(gdn_dt_bias), onorm_g=rows(gdn_onorm_g),
        rwkv_vecs=[rows(rwkv_mu), rows(rwkv_w0), rows(rwkv_a0), rows(rwkv_k_k), rows(rwkv_k_a),
                   rwkv_r_k.reshape(depth, 1, nw), rows(rwkv_ln_w), rows(rwkv_ln_b)],
        w2p=jnp.pad(rwkv_w2, ((0, 0), (0, LANES - lora_w), (0, 0))),
        a2p=jnp.pad(rwkv_a2, ((0, 0), (lora_w, LANES - lora_w - rwkv_a2.shape[1]), (0, 0))),
        g2=rwkv_g2,
        bd=(hi[:, None] == hi[None, :]).astype(F32),
        woa=w_out[:, :GDN_HEADS * GDN_D, :].astype(BF16), wob=w_out[:, GDN_HEADS * GDN_D:, :].astype(BF16),
        ffn_g=ffn_w_gate.astype(BF16), ffn_u=ffn_w_up.astype(BF16), ffn_d=ffn_w_down.astype(BF16),
        router=jnp.pad(moe_router, ((0, 0), (0, 0), (0, LANES - moe_router.shape[2]))),
        moe_g=moe_w_gate.astype(BF16), moe_u=moe_w_up.astype(BF16), moe_d=moe_w_down.astype(BF16),
    )

    nb_all = nbp + nbs
    bp = -(-nb_all // SUBLANES) * SUBLANES
    c_all = jnp.pad(jnp.concatenate([c_prompt, c_sample], axis=0), ((0, bp - nb_all), (0, 0)))
    mod = _ada_mod(c_all, w_ada, b_ada)[:, :, None, :]
    mod_p, mod_s = mod[:, :nbp], mod[:, nbp:nb_all]

    dt = x_prompt.dtype
    zc = jnp.zeros((depth, nbp) + state_gdn_conv.shape[2:], dt)
    zg = jnp.zeros((depth, nbp) + state_gdn.shape[2:], dt)
    zs = jnp.zeros((depth, nbp) + state_rwkv_shift.shape[2:], dt)
    zr = jnp.zeros((depth, nbp) + state_rwkv.shape[2:], dt)
    (y_p, y_s), (st_p, st_s) = _trunks(
        [x_prompt, x_sample], [mod_p, mod_s],
        [(zc, zg, zs, zr), (state_gdn_conv, state_gdn, state_rwkv_shift, state_rwkv)], w)
    return (y_p, y_s) + st_p + st_s
```

```python
import functools

import jax
import jax.numpy as jnp
from jax import lax
from jax.experimental import pallas as pl
from jax.experimental.pallas import tpu as pltpu

F32 = jnp.float32
BF16 = jnp.bfloat16
HI = lax.Precision.HIGHEST

LANES = 128
SUBLANES = 8
CHUNK = 64
GDN_HEADS = 4
GDN_D = 128
RWKV_HEADS = 8
RWKV_N = 64
GDN_CONV = 4
N_EXPERTS = 8
NORM_EPS = 1e-6
L2_EPS = 1e-6
GN_EPS = 64e-5
NEG_BIG = -1e30
RWKV_DECAY_SCALE = 0.6065306597126334
TOKEN_TILE = 512
GDN_TILE = 512
RWKV_TILE = 256
SHORT_SEQ_BATCH = 4
MOE_ROWS = 512
MOE_FF_TILE = 1792
MIB = 1024 * 1024

P_GDN_SCORE = 1
P_GDN_INV = 3
P_GDN_SOLVE = 1
P_GDN_STATE = 1
P_RWKV_SMALL = 1
P_RWKV_INV = 1
P_RWKV_STATE = 1
P_ROUTER = 3


def _cparams(sem, vmem_mib):
    return pltpu.CompilerParams(dimension_semantics=sem, vmem_limit_bytes=vmem_mib * MIB)


def _split_bf16(x):
    hi = x.astype(BF16)
    return hi, (x - hi.astype(F32)).astype(BF16)


def _split3_bf16(x):
    hi = x.astype(BF16)
    r1 = x - hi.astype(F32)
    mid = r1.astype(BF16)
    return hi, mid, (r1 - mid.astype(F32)).astype(BF16)


_CONTRACT = {"nn": (1, 0), "nt": (1, 1), "tn": (0, 0)}


def _dg(a, b, kind, prec):
    off = a.ndim - 2
    ca, cb = _CONTRACT[kind]
    dn = (((ca + off,), (cb + off,)), (((0,), (0,)) if off else ((), ())))
    dot = lambda x, y: lax.dot_general(x, y, dn, preferred_element_type=F32)
    if prec is None or prec is HI:
        return lax.dot_general(a, b, dn, preferred_element_type=F32, precision=prec)
    if prec == 1:
        return dot(a.astype(BF16), b.astype(BF16))
    if prec == "l3":
        bb = b.astype(BF16)
        a0, a1, a2 = _split3_bf16(a)
        return dot(a0, bb) + dot(a1, bb) + dot(a2, bb)
    if prec == "r3":
        ab = a.astype(BF16)
        b0, b1, b2 = _split3_bf16(b)
        return dot(ab, b0) + dot(ab, b1) + dot(ab, b2)
    ah, al = _split_bf16(a)
    bh, bl = _split_bf16(b)
    return dot(ah, bh) + dot(ah, bl) + dot(al, bh)


def _mm(a, b, prec=None):
    return _dg(a, b, "nn", prec)


def _mm_nt(a, b, prec=None):
    return _dg(a, b, "nt", prec)


def _mm_tn(a, b, prec=None):
    return _dg(a, b, "tn", prec)


def _silu(x):
    return x * jax.nn.sigmoid(x)


def _softplus(x):
    return jnp.maximum(x, 0.0) + jnp.log1p(jnp.exp(-jnp.abs(x)))


def _iota2(shape, dim):
    return lax.broadcasted_iota(jnp.int32, shape, dim)


def _per_segment(x, fn, *mod_refs):
    mods = [r[0] for r in mod_refs]
    nseg = mods[0].shape[0]
    if nseg == 1:
        return fn(x, *mods)
    rows = x.shape[0] // nseg
    return jnp.concatenate([fn(x[s * rows:(s + 1) * rows], *(m[s:s + 1] for m in mods)) for s in range(nseg)], axis=0)


def _scale_shift(xn, sc_ref, sh_ref):
    return _per_segment(xn, lambda x, sc, sh: x * (1.0 + sc) + sh, sc_ref, sh_ref)


def _gated(f, gt_ref):
    return _per_segment(f, lambda x, gt: gt * x, gt_ref)


def _shift_rows(x, prev8, k):
    head = jnp.where(_iota2(prev8.shape, 0) < k, pltpu.roll(prev8, k, 0), pltpu.roll(x[0:SUBLANES], k, 0))
    return jnp.concatenate([head, pltpu.roll(x, k, 0)[SUBLANES:]], axis=0)


def _causal_taps(x, w, prev8, ntap):
    y = x * w[ntap - 1:ntap, :]
    for k in range(1, ntap):
        y = y + _shift_rows(x, prev8, k) * w[ntap - 1 - k:ntap - k, :]
    return y


def _neumann_inverse(x, eye, prec):
    c = x.shape[-1]
    z = jnp.concatenate([x, jnp.broadcast_to(eye, x.shape)], axis=-1)
    keep_s = _iota2((c, 2 * c), 1) >= c
    for _ in range(6):
        z = _mm(z[..., :c], z, prec) + jnp.where(keep_s, z, 0.0)
    return z[..., c:]


def _ada_kernel(c_ref, w_ref, b_ref, o_ref):
    o_ref[0] = _mm(_silu(c_ref[...]), w_ref[0], 3) + b_ref[0]


def _ada_mod(c_all, w_ada, b_ada):
    depth, d, n6 = w_ada.shape
    bp = c_all.shape[0]
    tn = 1536
    return pl.pallas_call(
        _ada_kernel,
        grid=(depth, n6 // tn),
        in_specs=[pl.BlockSpec((bp, d), lambda l, j: (0, 0)),
                  pl.BlockSpec((1, d, tn), lambda l, j: (l, 0, j)),
                  pl.BlockSpec((1, 1, tn), lambda l, j: (l, 0, j))],
        out_specs=pl.BlockSpec((1, bp, tn), lambda l, j: (l, 0, j)),
        out_shape=jax.ShapeDtypeStruct((depth, bp, n6), F32),
        compiler_params=_cparams(("arbitrary", "arbitrary"), 40),
        name="ada_mod",
    )(c_all, w_ada, b_ada.reshape(depth, 1, n6))


def _norm_proj_kernel(x_ref, sh_ref, sc_ref, g_ref, wa_ref, wab_ref, wb_ref, pa_ref, pab_ref, pb_ref):
    x = x_ref[0]
    rs = lax.rsqrt(jnp.mean(x * x, axis=-1, keepdims=True) + NORM_EPS)
    hb = _scale_shift(x * rs * g_ref[...], sc_ref, sh_ref).astype(BF16)
    pa_ref[0] = _mm(hb, wa_ref[...])
    pab_ref[0] = _mm(hb, wab_ref[...])
    pb_ref[0] = _mm(hb, wb_ref[...])


def _norm_proj(x, mod, g, w_in16, layer, wab, wb, tm):
    b, t, d = x.shape
    na, nab, nb = 4 * GDN_HEADS * GDN_D, wab.shape[1], wb.shape[1]
    nseg = mod.shape[1]
    row = lambda i, j: (i, j, 0)
    const = lambda i, j: (0, 0)
    return pl.pallas_call(
        _norm_proj_kernel,
        grid=(b, t // tm),
        in_specs=[pl.BlockSpec((1, tm, d), row),
                  pl.BlockSpec((1, nseg, d), lambda i, j: (i, 0, 0)),
                  pl.BlockSpec((1, nseg, d), lambda i, j: (i, 0, 1)),
                  pl.BlockSpec((1, d), const),
                  pl.BlockSpec((None, d, na), lambda i, j: (layer, 0, 0)),
                  pl.BlockSpec((d, nab), const),
                  pl.BlockSpec((d, nb), const)],
        out_specs=[pl.BlockSpec((1, tm, na), row), pl.BlockSpec((1, tm, nab), row), pl.BlockSpec((1, tm, nb), row)],
        out_shape=[jax.ShapeDtypeStruct((b, t, na), F32), jax.ShapeDtypeStruct((b, t, nab), F32),
                   jax.ShapeDtypeStruct((b, t, nb), F32)],
        compiler_params=_cparams(("arbitrary", "arbitrary"), 48),
        name="norm_proj",
    )(x, mod, mod, g, w_in16, wab, wb)


def _gdn_kernel(pa_ref, pab_ref, cbuf_ref, s0_ref, cw_ref, alog_ref, dtb_ref, og_ref,
                o_ref, sfin_ref, xc_ref, s_ref, *, tc):
    t = pl.program_id(1)
    nqk = GDN_HEADS * GDN_D
    nconv = 3 * nqk

    nsq = pa_ref.shape[0]
    nh = GDN_HEADS

    @pl.when(t == 0)
    def _():
        xc_ref[...] = cbuf_ref[...]
        s_ref[...] = s0_ref[...].reshape(s_ref.shape)

    ys = []
    for sq in range(nsq):
        x = pa_ref[sq, :, 0:nconv]
        ys.append(_causal_taps(x, cw_ref[...], xc_ref[sq], GDN_CONV))
        xc_ref[sq] = x[tc - SUBLANES:tc, :]
    qkv = _silu(ys[0] if nsq == 1 else jnp.concatenate(ys, axis=0))

    ab = pab_ref[...].reshape(nsq * tc, LANES)
    gmat = -jnp.exp(alog_ref[...]) * _softplus(ab + dtb_ref[...])
    bmat = jax.nn.sigmoid(ab)

    c = CHUNK
    nch = nsq * tc // c
    nb = nch * nh
    ri = _iota2((c, c), 0)
    ci = _iota2((c, c), 1)
    incl = ri >= ci
    strict = ri > ci
    eye_c = (ri == ci).astype(F32)

    def heads(x, width):
        x = x.reshape(nch, c, nh * width)
        return jnp.stack([x[:, :, h * width:(h + 1) * width] for h in range(nh)], axis=1).reshape(nb, c, width)

    q = heads(qkv[:, 0:nqk], GDN_D)
    k = heads(qkv[:, nqk:2 * nqk], GDN_D)
    v = heads(qkv[:, 2 * nqk:3 * nqk], GDN_D)
    ones_d = jnp.ones((GDN_D, GDN_D), BF16)

    def l2n(x, scale):
        ss = _mm((x * x).reshape(nb * c, GDN_D), ones_d, 1).reshape(nb, c, GDN_D)
        return x * (lax.rsqrt(ss + L2_EPS) * scale)

    q = l2n(q, GDN_D ** -0.5)
    k = l2n(k, 1.0)
    beta = heads(bmat[:, nh:2 * nh], 1)

    g4 = gmat.reshape(nch, c, LANES)
    gcol4 = _mm(jnp.broadcast_to(incl.astype(F32), (nch, c, c)), g4, "r3")
    grow4 = _mm_tn(g4, jnp.broadcast_to((ri <= ci).astype(F32), (nch, c, c)), "l3")
    gc = jnp.stack([gcol4[:, :, h:h + 1] for h in range(nh)], axis=1).reshape(nb, c, 1)
    gr = jnp.stack([grow4[:, h:h + 1, :] for h in range(nh)], axis=1).reshape(nb, 1, c)
    decay = jnp.exp(jnp.where(incl, gc - gr, NEG_BIG))

    kb = k * beta
    m = jnp.where(strict, _mm_nt(kb, k, P_GDN_SCORE) * decay, 0.0)
    t0 = _neumann_inverse(-m, eye_c, 1)
    tinv = t0 + _mm(t0, (eye_c - t0) - _mm(m, t0, P_GDN_INV), 1)
    egc = jnp.exp(gc)
    sol = _mm(tinv, jnp.concatenate([v * beta, kb * egc], axis=2), P_GDN_SOLVE)
    u, w = sol[:, :, :GDN_D], sol[:, :, GDN_D:]
    aqk = _mm_nt(q, k, P_GDN_SCORE) * decay
    gl = gc[:, c - 1:c, :]
    kd = k * jnp.exp(gl - gc)
    qg = q * egc
    egl = jnp.exp(gl)

    s = s_ref[...]
    outs = []
    for ch in range(tc // c):
        bs = slice(ch * nh, (ch + 1) * nh) if nsq == 1 else slice(None)
        v_new = u[bs] - _mm(w[bs], s, P_GDN_STATE)
        outs.append(_mm(qg[bs], s, P_GDN_STATE) + _mm(aqk[bs], v_new, P_GDN_STATE))
        s = s * egl[bs] + _mm_tn(kd[bs], v_new, P_GDN_STATE)
    s_ref[...] = s

    o = jnp.stack(outs, axis=0).reshape(nch, nh, c, GDN_D)
    o = o * lax.rsqrt(jnp.mean(o * o, axis=-1, keepdims=True) + NORM_EPS) * og_ref[...]
    for h in range(nh):
        sl = slice(h * GDN_D, (h + 1) * GDN_D)
        z = pa_ref[:, :, nconv + h * GDN_D: nconv + (h + 1) * GDN_D]
        o_ref[:, :, sl] = (o[:, h].reshape(nsq, tc, GDN_D) * _silu(z)).astype(o_ref.dtype)

    @pl.when(t == pl.num_programs(1) - 1)
    def _():
        sfin_ref[...] = s_ref[...].reshape(sfin_ref.shape)


def _seqs_per_step(b, t, tc):
    return max(n for n in (SHORT_SEQ_BATCH, 2, 1) if b % n == 0) if t == tc == CHUNK else 1


def _gdn(pa, pab, cbuf8, s0, layer, cw, alog, dtb, og, tc):
    b, t, na = pa.shape
    nconv = cw.shape[1]
    nsq = _seqs_per_step(b, t, tc)
    row = lambda i, j: (i, j, 0)
    const2 = lambda i, j: (0, 0)
    return pl.pallas_call(
        functools.partial(_gdn_kernel, tc=tc),
        grid=(b // nsq, t // tc),
        in_specs=[pl.BlockSpec((nsq, tc, na), row),
                  pl.BlockSpec((nsq, tc, LANES), row),
                  pl.BlockSpec((nsq, SUBLANES, nconv), lambda i, j: (i, 0, 0)),
                  pl.BlockSpec((None, nsq, GDN_HEADS, GDN_D, GDN_D), lambda i, j: (layer, i, 0, 0, 0)),
                  pl.BlockSpec((GDN_CONV, nconv), const2),
                  pl.BlockSpec((1, LANES), const2),
                  pl.BlockSpec((1, LANES), const2),
                  pl.BlockSpec((1, GDN_D), const2)],
        out_specs=[pl.BlockSpec((nsq, tc, GDN_HEADS * GDN_D), row),
                   pl.BlockSpec((nsq, GDN_HEADS, GDN_D, GDN_D), lambda i, j: (i, 0, 0, 0))],
        out_shape=[jax.ShapeDtypeStruct((b, t, GDN_HEADS * GDN_D), BF16),
                   jax.ShapeDtypeStruct((b, GDN_HEADS, GDN_D, GDN_D), F32)],
        scratch_shapes=[pltpu.VMEM((nsq, SUBLANES, nconv), F32),
                        pltpu.VMEM((nsq * GDN_HEADS, GDN_D, GDN_D), F32)],
        compiler_params=_cparams(("arbitrary", "arbitrary"), 40),
        name="gdn_mixer",
    )(pa, pab, cbuf8, s0, cw, alog, dtb, og)


def _rwkv_kernel(pb_ref, sbuf_ref, s0_ref, mu_ref, w0_ref, w2_ref, a0_ref, a2_ref, g2_ref, kk_ref, ka_ref,
                 rk_ref, lnw_ref, lnb_ref, bd_ref, o_ref, sfin_ref, xc_ref, s_ref, y_ref, *, tc):
    t = pl.program_id(1)
    nw = RWKV_HEADS * RWKV_N

    nsq = pb_ref.shape[0]
    nh = RWKV_HEADS

    @pl.when(t == 0)
    def _():
        xc_ref[...] = sbuf_ref[...]
        s_ref[...] = s0_ref[...].reshape(s_ref.shape)

    xl, pl_ = [], []
    for sq in range(nsq):
        xq = pb_ref[sq]
        xl.append(xq)
        pl_.append(_shift_rows(xq, xc_ref[sq], 1))
        xc_ref[sq] = xq[tc - SUBLANES:tc, :]
    x = xl[0] if nsq == 1 else jnp.concatenate(xl, axis=0)
    prev = pl_[0] if nsq == 1 else jnp.concatenate(pl_, axis=0)
    xs = x + (prev - x) * mu_ref[...]
    r = xs[:, 0:nw]
    kr = xs[:, nw:2 * nw]
    vr = xs[:, 2 * nw:3 * nw]
    wa = xs[:, 3 * nw:3 * nw + LANES]
    gd = xs[:, 3 * nw + LANES:3 * nw + 2 * LANES]

    bd = bd_ref[...]
    ps = P_RWKV_SMALL
    lw = -RWKV_DECAY_SCALE * jax.nn.sigmoid(w0_ref[...] + _mm(jnp.tanh(wa), w2_ref[...], ps))
    a = jax.nn.sigmoid(a0_ref[...] + _mm(wa, a2_ref[...], ps))
    gate = _mm(jax.nn.sigmoid(gd), g2_ref[...], ps)
    kkr = kr * kk_ref[...]
    kk = kkr * lax.rsqrt(_mm(kkr * kkr, bd, ps) + L2_EPS)
    kb = kr * (1.0 + (a - 1.0) * ka_ref[...])
    aa = -kk
    bb = kk * a
    bonus = _mm(r * kb * rk_ref[...], bd, ps) * vr

    c = CHUNK
    n = RWKV_N
    nch = nsq * tc // c
    nb = nch * nh
    ri = _iota2((c, c), 0)
    ci = _iota2((c, c), 1)
    incl = ri >= ci
    eye = ri == ci
    eye_c = eye.astype(F32)

    def heads(x):
        return jnp.stack([x[:, :, h * n:(h + 1) * n] for h in range(nh)], axis=1).reshape(nb, x.shape[1], n)

    chunks = lambda x: x.reshape(nch, c, nw)
    lw4 = chunks(lw)
    g = _mm(jnp.broadcast_to(incl.astype(F32), (nch, c, c)), lw4, "r3")
    gfin = g[:, c - 1:c, :]
    eng = jnp.exp(-g)
    efin = jnp.exp(gfin)
    tail = efin * eng
    aa4, bb4, kb4 = chunks(aa), chunks(bb), chunks(kb)
    at = heads(aa4 * jnp.exp(g - lw4))
    bt = heads(bb4 * eng)
    kt = heads(kb4 * eng)
    rt = heads(chunks(r) * jnp.exp(g))
    bh = heads(bb4 * tail)
    kh = heads(kb4 * tail)
    egf = heads(efin)
    v4 = chunks(vr)
    upper = _iota2((c, 2 * n), 1) >= n
    vcols = [v4[:, :, (h // 2) * 2 * n:(h // 2 + 1) * 2 * n] for h in range(nh)]
    vhi = jnp.stack([jnp.where(upper, col if h % 2 else pltpu.roll(col, n, 2), 0.0)
                     for h, col in enumerate(vcols)], axis=1).reshape(nb, c, 2 * n)
    zeros_lo = jnp.zeros((nb, c, n), F32)
    pad_hi = lambda x: jnp.concatenate([x, zeros_lo], axis=-1)

    h16 = lambda x: x.astype(BF16)
    colj = _iota2((c, 2 * c), 1) % c
    rowi = _iota2((c, 2 * c), 0)
    vhi16 = h16(vhi)
    ar = jnp.concatenate([h16(at), h16(rt)], axis=1)
    xbk = _mm_nt(ar, jnp.concatenate([h16(bt), h16(kt)], axis=1))
    l2 = jnp.where(rowi > colj, xbk[:, :c], 0.0)
    m2 = jnp.where(rowi >= colj, xbk[:, c:], 0.0)
    tinv = _neumann_inverse(l2[..., :c], eye_c, P_RWKV_INV)
    lv = _mm(h16(l2), jnp.concatenate([jnp.zeros_like(vhi16), vhi16], axis=1))
    wu = _mm(h16(tinv), h16(lv + pad_hi(at)))
    wv = jnp.concatenate([h16(wu), vhi16], axis=1)
    qy = _mm(h16(m2), wv) + pad_hi(rt)
    pi = _mm_tn(wv, jnp.concatenate([h16(bh), h16(kh)], axis=1))
    pt = jnp.where(eye, egf, 0.0) + pi[:, :n]
    inc = pi[:, n:]

    s = s_ref[...]
    zeros_s = jnp.zeros_like(s)
    ys = []
    for ch in range(tc // c):
        bs = slice(ch * nh, (ch + 1) * nh) if nsq == 1 else slice(None)
        ys.append(_mm_nt(qy[bs][..., :n], jnp.concatenate([zeros_s, s], axis=1), P_RWKV_STATE) + qy[bs])
        s = _mm(s, pt[bs], P_RWKV_STATE) + inc[bs]
    s_ref[...] = s
    y4 = jnp.stack(ys, axis=0).reshape(nch, nh, c, 2 * n)
    for h in range(nh):
        y_ref[:, h * n:(h + 1) * n] = y4[:, h, :, n:].reshape(nsq * tc, n)

    y = y_ref[...]
    mean = _mm(y, bd, ps) * (1.0 / n)
    dlt = y - mean
    var = _mm(dlt * dlt, bd, ps) * (1.0 / n)
    yn = dlt * lax.rsqrt(var + GN_EPS) * lnw_ref[...] + lnb_ref[...]
    o_ref[...] = ((yn + bonus) * gate).reshape(o_ref.shape).astype(o_ref.dtype)

    @pl.when(t == pl.num_programs(1) - 1)
    def _():
        sfin_ref[...] = s_ref[...].reshape(sfin_ref.shape)


def _rwkv(pb, sbuf8, s0, layer, vecs, w2p, a2p, g2, bd, tc):
    b, t, nb = pb.shape
    nw = RWKV_HEADS * RWKV_N
    nsq = _seqs_per_step(b, t, tc)
    row = lambda i, j: (i, j, 0)
    const2 = lambda i, j: (0, 0)
    mu, w0, a0, kk, ka, rk, lnw, lnb = vecs
    vec_spec = pl.BlockSpec((1, nw), const2)
    return pl.pallas_call(
        functools.partial(_rwkv_kernel, tc=tc),
        grid=(b // nsq, t // tc),
        in_specs=[pl.BlockSpec((nsq, tc, nb), row),
                  pl.BlockSpec((nsq, SUBLANES, nb), lambda i, j: (i, 0, 0)),
                  pl.BlockSpec((None, nsq, RWKV_HEADS, RWKV_N, RWKV_N), lambda i, j: (layer, i, 0, 0, 0)),
                  pl.BlockSpec((1, nb), const2),
                  vec_spec,
                  pl.BlockSpec((LANES, nw), const2),
                  vec_spec,
                  pl.BlockSpec((LANES, nw), const2),
                  pl.BlockSpec((LANES, nw), const2),
                  vec_spec, vec_spec, vec_spec, vec_spec, vec_spec,
                  pl.BlockSpec((nw, nw), const2)],
        out_specs=[pl.BlockSpec((nsq, tc, nw), row),
                   pl.BlockSpec((nsq, RWKV_HEADS, RWKV_N, RWKV_N), lambda i, j: (i, 0, 0, 0))],
        out_shape=[jax.ShapeDtypeStruct((b, t, nw), BF16),
                   jax.ShapeDtypeStruct((b, RWKV_HEADS, RWKV_N, RWKV_N), F32)],
        scratch_shapes=[pltpu.VMEM((nsq, SUBLANES, nb), F32),
                        pltpu.VMEM((nsq * RWKV_HEADS, RWKV_N, RWKV_N), F32),
                        pltpu.VMEM((nsq * tc, nw), F32)],
        compiler_params=_cparams(("arbitrary", "arbitrary"), 40),
        name="rwkv_mixer",
    )(pb, sbuf8, s0, mu, w0, w2p, a0, a2p, g2, kk, ka, rk, lnw, lnb, bd)


def _out_proj_kernel(oa_ref, ob_ref, x_ref, gt_ref, sh_ref, sc_ref, g_ref, woa_ref, wob_ref, *rest, with_router, nt):
    if with_router:
        wr_ref = rest[0]
        x1_ref, h_ref, lg_ref = rest[-3:]
    else:
        x1_ref, h_ref = rest

    @pl.when(pl.program_id(1) < nt)
    def _():
        mix = _mm(oa_ref[0], woa_ref[...]) + _mm(ob_ref[0], wob_ref[...])
        x1 = x_ref[0] + _gated(mix, gt_ref)
        x1_ref[0] = x1
        rs = lax.rsqrt(jnp.mean(x1 * x1, axis=-1, keepdims=True) + NORM_EPS)
        h = _scale_shift(x1 * rs * g_ref[...], sc_ref, sh_ref)
        if with_router:
            h_ref[...] = h
            lg_ref[...] = _mm(h, wr_ref[...], P_ROUTER)
        else:
            h_ref[0] = h.astype(h_ref.dtype)

    if with_router:
        @pl.when(pl.program_id(1) >= nt)
        def _():
            h_ref[...] = jnp.zeros_like(h_ref)
            lg_ref[...] = jnp.zeros_like(lg_ref)


def _out_proj(oa, ob, x, mod, g, woa, wob, tm, router=None, flat_rows=None, row_off=0, flat_prev=None):
    b, t, d = x.shape
    nh = oa.shape[2]
    nt = t // tm
    nseg = mod.shape[1]
    tail = 0
    if router is not None and flat_prev is None:
        assert b == 1 and row_off == 0 and (flat_rows - t) % tm == 0
        tail = (flat_rows - t) // tm
    row = lambda i, j: (i, jnp.minimum(j, nt - 1), 0)
    const = lambda i, j: (0, 0)
    in_specs = [pl.BlockSpec((1, tm, nh), row), pl.BlockSpec((1, tm, nh), row), pl.BlockSpec((1, tm, d), row),
                pl.BlockSpec((1, nseg, d), lambda i, j: (i, 0, 2)),
                pl.BlockSpec((1, nseg, d), lambda i, j: (i, 0, 3)),
                pl.BlockSpec((1, nseg, d), lambda i, j: (i, 0, 4)),
                pl.BlockSpec((1, d), const), pl.BlockSpec((nh, d), const), pl.BlockSpec((nh, d), const)]
    args = [oa, ob, x, mod, mod, mod, g, woa, wob]
    aliases = {}
    if router is None:
        out_specs = [pl.BlockSpec((1, tm, d), row), pl.BlockSpec((1, tm, d), row)]
        out_shape = [jax.ShapeDtypeStruct((b, t, d), F32), jax.ShapeDtypeStruct((b, t, d), BF16)]
    else:
        blk_off = row_off // tm
        flat = lambda i, j: (blk_off + i * nt + j, 0)
        in_specs.append(pl.BlockSpec((d, LANES), const))
        args.append(router)
        if flat_prev is not None:
            aliases = {len(args): 1, len(args) + 1: 2}
            in_specs += [pl.BlockSpec(memory_space=pl.ANY)] * 2
            args += list(flat_prev)
        out_specs = [pl.BlockSpec((1, tm, d), row), pl.BlockSpec((tm, d), flat), pl.BlockSpec((tm, LANES), flat)]
        out_shape = [jax.ShapeDtypeStruct((b, t, d), F32), jax.ShapeDtypeStruct((flat_rows, d), F32),
                     jax.ShapeDtypeStruct((flat_rows, LANES), F32)]
    return pl.pallas_call(
        functools.partial(_out_proj_kernel, with_router=router is not None, nt=nt),
        grid=(b, nt + tail),
        in_specs=in_specs, out_specs=out_specs, out_shape=out_shape, input_output_aliases=aliases,
        compiler_params=_cparams(("arbitrary", "arbitrary"), 56),
        name="out_proj" if router is None else "out_proj_router",
    )(*args)


def _final_norm(x2, fg_ref):
    rs = lax.rsqrt(jnp.mean(x2 * x2, axis=-1, keepdims=True) + NORM_EPS)
    return x2 * rs * fg_ref[...]


def _ffn_kernel(h_ref, x_ref, gt_ref, wg_ref, wu_ref, wd_ref, fg_ref, o_ref, *, nf, final):
    h = h_ref[0]
    ff = wg_ref.shape[1]
    tf = ff // nf
    acc = None
    for f in range(nf):
        cols = slice(f * tf, (f + 1) * tf)
        act = (_silu(_mm(h, wg_ref[:, cols])) * _mm(h, wu_ref[:, cols])).astype(BF16)
        part = _mm(act, wd_ref[cols, :])
        acc = part if acc is None else acc + part
    x2 = x_ref[0] + _gated(acc, gt_ref)
    o_ref[0] = _final_norm(x2, fg_ref) if final else x2


def _ffn(h, x, mod, wg, wu, wd, fg, final, tm):
    b, t, d = x.shape
    ff = wg.shape[1]
    row = lambda i, j: (i, j, 0)
    const = lambda i, j: (0, 0)
    once = pl.Buffered(1)
    return pl.pallas_call(
        functools.partial(_ffn_kernel, nf=1, final=final),
        grid=(b, t // tm),
        in_specs=[pl.BlockSpec((1, tm, d), row), pl.BlockSpec((1, tm, d), row),
                  pl.BlockSpec((1, mod.shape[1], d), lambda i, j: (i, 0, 5)),
                  pl.BlockSpec((d, ff), const, pipeline_mode=once),
                  pl.BlockSpec((d, ff), const, pipeline_mode=once),
                  pl.BlockSpec((ff, d), const, pipeline_mode=once),
                  pl.BlockSpec((1, d), const)],
        out_specs=pl.BlockSpec((1, tm, d), row),
        out_shape=jax.ShapeDtypeStruct((b, t, d), F32),
        compiler_params=_cparams(("arbitrary", "arbitrary"), 56),
        name="ffn_dense",
    )(h, x, mod, wg, wu, wd, fg)


def _route_kernel(lg_ref, info_ref, p_ref, cnt_ref, carry_ref, *, tr):
    i = pl.program_id(0)

    @pl.when(i == 0)
    def _():
        carry_ref[...] = jnp.zeros_like(carry_ref)

    lane = _iota2((tr, LANES), 1)
    lg = jnp.where(lane < N_EXPERTS, lg_ref[...], NEG_BIG)
    m1 = jnp.max(lg, axis=-1, keepdims=True)
    e1 = jnp.min(jnp.where(lg == m1, lane, LANES), axis=-1, keepdims=True)
    lg2 = jnp.where(lane == e1, NEG_BIG, lg)
    m2 = jnp.max(lg2, axis=-1, keepdims=True)
    e2 = jnp.min(jnp.where(lg2 == m2, lane, LANES), axis=-1, keepdims=True)
    ex = jnp.exp(m2 - m1)
    den = 1.0 + ex
    p1 = 1.0 / den
    p2 = ex / den
    oh1 = lane == e1
    oh2 = lane == e2
    oh = jnp.where(oh1 | oh2, 1.0, 0.0)
    lstrict = (_iota2((tr, tr), 0) > _iota2((tr, tr), 1)).astype(BF16)
    cex = _mm(lstrict, oh.astype(BF16)) + carry_ref[...]
    rank1 = jnp.sum(jnp.where(oh1, cex, 0.0), axis=-1, keepdims=True).astype(jnp.int32)
    rank2 = jnp.sum(jnp.where(oh2, cex, 0.0), axis=-1, keepdims=True).astype(jnp.int32)
    carry_ref[...] = carry_ref[...] + jnp.sum(oh, axis=0, keepdims=True)
    info_ref[...] = jnp.where(lane == 0, e1, jnp.where(lane == 1, e2, jnp.where(lane == 2, rank1,
                              jnp.where(lane == 3, rank2, 0))))
    p_ref[...] = jnp.where(lane == 0, p1, jnp.where(lane == 1, p2, 0.0))

    @pl.when(i == pl.num_programs(0) - 1)
    def _():
        cnt_ref[...] = carry_ref[...].astype(jnp.int32)


def _route(logits, tr):
    n = logits.shape[0]
    blk = pl.BlockSpec((tr, LANES), lambda i: (i, 0))
    return pl.pallas_call(
        functools.partial(_route_kernel, tr=tr),
        grid=(n // tr,),
        in_specs=[blk],
        out_specs=[blk, blk, pl.BlockSpec((1, LANES), lambda i: (0, 0))],
        out_shape=[jax.ShapeDtypeStruct((n, LANES), jnp.int32), jax.ShapeDtypeStruct((n, LANES), F32),
                   jax.ShapeDtypeStruct((1, LANES), jnp.int32)],
        scratch_shapes=[pltpu.VMEM((1, LANES), F32)],
        compiler_params=_cparams(("arbitrary",), 24),
        name="moe_route",
    )(logits)


def _dest_kernel(info_ref, sp_ref, d_ref):
    info = info_ref[...]
    lane = _iota2(info.shape, 1)
    sp = sp_ref[...].astype(F32)
    start1 = jnp.sum(jnp.where(lane == info[:, 0:1], sp, 0.0), axis=-1, keepdims=True).astype(jnp.int32)
    start2 = jnp.sum(jnp.where(lane == info[:, 1:2], sp, 0.0), axis=-1, keepdims=True).astype(jnp.int32)
    d_ref[...] = jnp.where(lane == 0, start1 + info[:, 2:3], jnp.where(lane == 1, start2 + info[:, 3:4], 0))


def _dest(info, sp_row, tr):
    n = info.shape[0]
    blk = pl.BlockSpec((tr, LANES), lambda i: (i, 0))
    return pl.pallas_call(
        _dest_kernel,
        grid=(n // tr,),
        in_specs=[blk, pl.BlockSpec((1, LANES), lambda i: (0, 0))],
        out_specs=blk,
        out_shape=jax.ShapeDtypeStruct((n, LANES), jnp.int32),
        compiler_params=_cparams(("arbitrary",), 24),
        name="moe_dest",
    )(info, sp_row)


ROW_DMA_UNROLL = 16


def _dispatch_kernel(d1_ref, d2_ref, sp_ref, ep_ref, h_ref, xb_ref, zero_ref, sem, zsem, *, td):
    i = pl.program_id(0)

    @pl.when(i == 0)
    def _():
        zero_ref[...] = jnp.zeros_like(zero_ref)

        def zero_block(start):
            cp = pltpu.make_async_copy(zero_ref, xb_ref.at[pl.ds(pl.multiple_of(start, MOE_ROWS), MOE_ROWS), :], zsem)
            cp.start()
            cp.wait()

        for e in range(N_EXPERTS):
            @pl.when(ep_ref[e] > sp_ref[e])
            def _():
                zero_block(ep_ref[e] - MOE_ROWS)

        def tail(blk, carry):
            zero_block(blk * MOE_ROWS)
            return carry

        lax.fori_loop(ep_ref[N_EXPERTS - 1] // MOE_ROWS, xb_ref.shape[0] // MOE_ROWS, tail, 0)

    base = i * td

    def row_copy(j, dst):
        return pltpu.make_async_copy(h_ref.at[pl.ds(j, 1), :], xb_ref.at[pl.ds(dst, 1), :], sem)

    def issue(j8, carry):
        for u in range(ROW_DMA_UNROLL):
            j = j8 * ROW_DMA_UNROLL + u
            row_copy(j, d1_ref[base + j]).start(priority=0)
            row_copy(j, d2_ref[base + j]).start(priority=1)
        return carry

    lax.fori_loop(0, td // ROW_DMA_UNROLL, issue, 0)
    for _ in range(2):
        pltpu.make_async_copy(h_ref, xb_ref.at[pl.ds(0, td), :], sem).wait()


def _dispatch(d1, d2, sp, ep, h2, n_rows, td):
    n, d = h2.shape
    return pl.pallas_call(
        functools.partial(_dispatch_kernel, td=td),
        grid_spec=pltpu.PrefetchScalarGridSpec(
            num_scalar_prefetch=4,
            grid=(n // td,),
            in_specs=[pl.BlockSpec((td, d), lambda i, *_: (i, 0))],
            out_specs=pl.BlockSpec(memory_space=pl.ANY),
            scratch_shapes=[pltpu.VMEM((MOE_ROWS, d), F32), pltpu.SemaphoreType.DMA, pltpu.SemaphoreType.DMA]),
        out_shape=jax.ShapeDtypeStruct((n_rows, d), F32),
        compiler_params=_cparams(("arbitrary",), 24),
        name="moe_dispatch",
    )(d1, d2, sp, ep, h2)


def _expert_kernel(be_ref, nu_ref, xb_ref, wg_ref, wu_ref, wd_ref, yb_ref):
    j = pl.program_id(0)
    f = pl.program_id(1)

    @pl.when(j < nu_ref[0])
    def _():
        xg = xb_ref[...].astype(BF16)
        act = (_silu(_mm(xg, wg_ref[0])) * _mm(xg, wu_ref[0])).astype(BF16)
        part = _mm(act, wd_ref[0])

        @pl.when(f == 0)
        def _():
            yb_ref[...] = part

        @pl.when(f > 0)
        def _():
            yb_ref[...] = yb_ref[...] + part

    @pl.when((j >= nu_ref[0]) & (f == 0))
    def _():
        yb_ref[...] = jnp.zeros_like(yb_ref)


def _experts(block_e, n_used, xb, wg, wu, wd, tf):
    n_rows, d = xb.shape
    ff = wg.shape[2]
    nf = ff // tf
    n_blocks = n_rows // MOE_ROWS

    def blk(j, f, be, nu):
        return (jnp.minimum(j, nu[0] - 1), 0)

    def fcol(j, f, be, nu):
        return jnp.where(j < nu[0], f, nf - 1)

    return pl.pallas_call(
        _expert_kernel,
        grid_spec=pltpu.PrefetchScalarGridSpec(
            num_scalar_prefetch=2,
            grid=(n_blocks, nf),
            in_specs=[pl.BlockSpec((MOE_ROWS, d), blk),
                      pl.BlockSpec((1, d, tf), lambda j, f, be, nu: (be[j], 0, fcol(j, f, be, nu))),
                      pl.BlockSpec((1, d, tf), lambda j, f, be, nu: (be[j], 0, fcol(j, f, be, nu))),
                      pl.BlockSpec((1, tf, d), lambda j, f, be, nu: (be[j], fcol(j, f, be, nu), 0))],
            out_specs=pl.BlockSpec((MOE_ROWS, d), lambda j, f, be, nu: (j, 0))),
        out_shape=jax.ShapeDtypeStruct((n_rows, d), F32),
        compiler_params=_cparams(("arbitrary", "arbitrary"), 56),
        name="moe_experts",
    )(block_e, n_used, xb, wg, wu, wd)


def _combine_kernel(d1_ref, d2_ref, x_ref, gt_ref, p_ref, fg_ref, yb_ref, o_ref, y1_ref, y2_ref, sem, *,
                    tm, final, row_off):
    base = row_off + (pl.program_id(0) * pl.num_programs(1) + pl.program_id(1)) * tm

    def row_copy(src, buf_ref, j):
        return pltpu.make_async_copy(yb_ref.at[pl.ds(src, 1), :], buf_ref.at[pl.ds(j, 1), :], sem)

    def issue(j8, carry):
        for u in range(min(ROW_DMA_UNROLL, tm)):
            j = j8 * ROW_DMA_UNROLL + u
            row_copy(d1_ref[base + j], y1_ref, j).start(priority=0)
            row_copy(d2_ref[base + j], y2_ref, j).start(priority=1)
        return carry

    lax.fori_loop(0, tm // ROW_DMA_UNROLL, issue, 0)
    for buf_ref in (y1_ref, y2_ref):
        pltpu.make_async_copy(yb_ref.at[pl.ds(0, tm), :], buf_ref, sem).wait()

    p = p_ref[...]
    f = y1_ref[...] * p[:, 0:1] + y2_ref[...] * p[:, 1:2]
    x2 = x_ref[0] + _gated(f, gt_ref)
    o_ref[0] = _final_norm(x2, fg_ref) if final else x2


def _combine(d1, d2, x, mod, probs, fg, yb, final, tm, row_off):
    b, t, d = x.shape
    nt = t // tm
    blk_off = row_off // tm
    return pl.pallas_call(
        functools.partial(_combine_kernel, tm=tm, final=final, row_off=row_off),
        grid_spec=pltpu.PrefetchScalarGridSpec(
            num_scalar_prefetch=2,
            grid=(b, nt),
            in_specs=[pl.BlockSpec((1, tm, d), lambda i, j, *_: (i, j, 0)),
                      pl.BlockSpec((1, mod.shape[1], d), lambda i, j, *_: (i, 0, 5)),
                      pl.BlockSpec((tm, LANES), lambda i, j, *_: (blk_off + i * nt + j, 0)),
                      pl.BlockSpec((1, d), lambda i, j, *_: (0, 0)),
                      pl.BlockSpec(memory_space=pl.ANY)],
            out_specs=pl.BlockSpec((1, tm, d), lambda i, j, *_: (i, j, 0)),
            scratch_shapes=[pltpu.VMEM((tm, d), F32), pltpu.VMEM((tm, d), F32), pltpu.SemaphoreType.DMA]),
        out_shape=jax.ShapeDtypeStruct((b, t, d), F32),
        compiler_params=_cparams(("arbitrary", "arbitrary"), 40),
        name="moe_combine",
    )(d1, d2, x, mod, probs, fg, yb)


def _moe(h2, logits, trunks, wg, wu, wd, fg, final):
    n, d = h2.shape
    tile = lambda cap: max(r for r in (cap, cap // 2, cap // 4, cap // 8, cap // 16) if n % r == 0)
    info, probs, cnt = _route(logits, tile(512))
    counts = cnt[0, :N_EXPERTS]
    padded = (counts + MOE_ROWS - 1) // MOE_ROWS * MOE_ROWS
    ep = jnp.cumsum(padded).astype(jnp.int32)
    sp = ep - padded
    n_blocks = -(-(2 * n) // MOE_ROWS) + N_EXPERTS
    n_used = jnp.maximum(ep[-1] // MOE_ROWS, 1).astype(jnp.int32)
    blk_start = jnp.minimum(jnp.arange(n_blocks, dtype=jnp.int32), n_used - 1) * MOE_ROWS
    block_e = jnp.minimum(jnp.sum(blk_start[:, None] >= ep[None, :], axis=1), N_EXPERTS - 1).astype(jnp.int32)
    dest = _dest(info, jnp.pad(sp, (0, LANES - N_EXPERTS))[None, :], tile(1024))
    d1, d2 = dest[:, 0], dest[:, 1]
    xb = _dispatch(d1, d2, sp, ep, h2, n_blocks * MOE_ROWS, tile(1024))
    yb = _experts(block_e, n_used.reshape(1), xb, wg, wu, wd, MOE_FF_TILE)
    return [_combine(d1, d2, x1, mod, probs, fg, yb, final, tm, off) for x1, mod, tm, off in trunks]


def _pad_rows_front(a, rows):
    pad = [(0, 0)] * a.ndim
    pad[-2] = (rows - a.shape[-2], 0)
    return jnp.pad(a, pad)


def _trunks(xs, mods, states, w):
    depth = w["w_in16"].shape[0]
    n_tr = len(xs)
    shapes = [x.shape for x in xs]
    d = shapes[0][2]
    offs = [sum(s[0] * s[1] for s in shapes[:i]) for i in range(n_tr)]
    n_all = sum(s[0] * s[1] for s in shapes)
    tms, mvs = [], []
    for (b, t, _), m in zip(shapes, mods):
        if t < TOKEN_TILE and TOKEN_TILE % t == 0 and (b * t) % TOKEN_TILE == 0:
            nseg = TOKEN_TILE // t
            tms.append(TOKEN_TILE)
            mvs.append(m.reshape(depth, b // nseg, nseg, m.shape[-1]))
        else:
            tms.append(min(t, TOKEN_TILE))
            mvs.append(m)
    wide = [2 * tm if m.shape[2] == 1 and s[1] % (2 * tm) == 0 and (n_all - s[0] * s[1]) % (2 * tm) == 0 else tm
            for tm, m, s in zip(tms, mvs, shapes)]
    tok = lambda a, i: a.reshape(-1, tms[i] if shapes[i][1] < tms[i] else shapes[i][1], a.shape[-1])
    seq = lambda a, i: a.reshape(shapes[i][0], shapes[i][1], a.shape[-1])
    xs = [tok(x, i) for i, x in enumerate(xs)]
    new_states = [([], [], [], []) for _ in xs]
    for l in range(depth):
        final = l == depth - 1
        j = l // 2
        x1s, flat = [], None
        for i, x in enumerate(xs):
            t = shapes[i][1]
            ml = mvs[i][l]
            conv0, gdn0, shift0, rwkv0 = states[i]
            pa, pab, pb = (seq(a, i) for a in _norm_proj(x, ml, w["norm1_g"][l], w["w_in16"], l, w["wab"][l],
                                                         w["wb"][l], tms[i]))
            oa, sg = _gdn(pa, pab, _pad_rows_front(conv0[l], SUBLANES), gdn0, l, w["conv_w"][l], w["alog"][l],
                          w["dtb"][l], w["onorm_g"][l], min(t, GDN_TILE))
            ob, sr = _rwkv(pb, _pad_rows_front(shift0[l], SUBLANES), rwkv0, l, [v[l] for v in w["rwkv_vecs"]],
                           w["w2p"][l], w["a2p"][l], w["g2"][l], w["bd"], min(t, RWKV_TILE))
            for lst, val in zip(new_states[i], (pa[:, t - (GDN_CONV - 1):, :w["conv_w"].shape[2]], sg,
                                                pb[:, t - 1:, :], sr)):
                lst.append(val)
            oa, ob = tok(oa, i), tok(ob, i)
            if l % 2 == 0:
                x1, h2 = _out_proj(oa, ob, x, ml, w["norm2_g"][l], w["woa"][l], w["wob"][l], wide[i])
                xs[i] = _ffn(h2, x1, ml, w["ffn_g"][j], w["ffn_u"][j], w["ffn_d"][j], w["final_g"], final, tms[i])
            else:
                x1, *flat = _out_proj(oa, ob, x, ml, w["norm2_g"][l], w["woa"][l], w["wob"][l], wide[i],
                                      router=w["router"][j], flat_rows=n_all, row_off=offs[i], flat_prev=flat)
                x1s.append(x1)
        if l % 2 == 1:
            trunks = [(x1s[i], mvs[i][l], wide[i], offs[i]) for i in range(n_tr)]
            xs = _moe(flat[0], flat[1], trunks, w["moe_g"][j], w["moe_u"][j], w["moe_d"][j], w["final_g"], final)
    return [seq(x, i) for i, x in enumerate(xs)], [tuple(jnp.stack(s) for s in st) for st in new_states]


def kernel(x_prompt, x_sample, c_prompt, c_sample, state_gdn_conv, state_gdn, state_rwkv_shift, state_rwkv, w_ada, b_ada, norm1_g, norm2_g, w_in, gdn_conv_w, gdn_a_log, gdn_dt_bias, gdn_onorm_g, rwkv_mu, rwkv_w0, rwkv_w2, rwkv_a0, rwkv_a2, rwkv_g2, rwkv_k_k, rwkv_k_a, rwkv_r_k, rwkv_ln_w, rwkv_ln_b, w_out, ffn_w_gate, ffn_w_up, ffn_w_down, moe_router, moe_w_gate, moe_w_up, moe_w_down, final_g):
    depth, d, _ = w_in.shape
    nbp = x_prompt.shape[0]
    nbs = x_sample.shape[0]
    nqkvz = 4 * GDN_HEADS * GDN_D
    nab = 2 * GDN_HEADS
    nw = RWKV_HEADS * RWKV_N
    lora_w = rwkv_w2.shape[1]

    def lane_pad(v):
        return jnp.pad(v, ((0, 0), (0, LANES - v.shape[1])))[:, None, :]

    rows = lambda v: v[:, None, :]
    hi = jnp.arange(nw) // RWKV_N
    w = dict(
        norm1_g=rows(norm1_g), norm2_g=rows(norm2_g), final_g=final_g[None, :],
        w_in16=w_in.astype(BF16),
        wab=jnp.pad(w_in[:, :, nqkvz:nqkvz + nab], ((0, 0), (0, 0), (0, LANES - nab))).astype(BF16),
        wb=w_in[:, :, nqkvz + nab:].astype(BF16),
        conv_w=gdn_conv_w, alog=lane_pad(gdn_a_log), dtb=lane_pad(gdn_dt_bias), onorm_g=rows(gdn_onorm_g),
        rwkv_vecs=[rows(rwkv_mu), rows(rwkv_w0), rows(rwkv_a0), rows(rwkv_k_k), rows(rwkv_k_a),
                   rwkv_r_k.reshape(depth, 1, nw), rows(rwkv_ln_w), rows(rwkv_ln_b)],
        w2p=jnp.pad(rwkv_w2, ((0, 0), (0, LANES - lora_w), (0, 0))),
        a2p=jnp.pad(rwkv_a2, ((0, 0), (lora_w, LANES - lora_w - rwkv_a2.shape[1]), (0, 0))),
        g2=rwkv_g2,
        bd=(hi[:, None] == hi[None, :]).astype(F32),
        woa=w_out[:, :GDN_HEADS * GDN_D, :].astype(BF16), wob=w_out[:, GDN_HEADS * GDN_D:, :].astype(BF16),
        ffn_g=ffn_w_gate.astype(BF16), ffn_u=ffn_w_up.astype(BF16), ffn_d=ffn_w_down.astype(BF16),
        router=jnp.pad(moe_router, ((0, 0), (0, 0), (0, LANES - moe_router.shape[2]))),
        moe_g=moe_w_gate.astype(BF16), moe_u=moe_w_up.astype(BF16), moe_d=moe_w_down.astype(BF16),
    )

    nb_all = nbp + nbs
    bp = -(-nb_all // SUBLANES) * SUBLANES
    c_all = jnp.pad(jnp.concatenate([c_prompt, c_sample], axis=0), ((0, bp - nb_all), (0, 0)))
    mod = _ada_mod(c_all, w_ada, b_ada)[:, :, None, :]
    mod_p, mod_s = mod[:, :nbp], mod[:, nbp:nb_all]

    dt = x_prompt.dtype
    zc = jnp.zeros((depth, nbp) + state_gdn_conv.shape[2:], dt)
    zg = jnp.zeros((depth, nbp) + state_gdn.shape[2:], dt)
    zs = jnp.zeros((depth, nbp) + state_rwkv_shift.shape[2:], dt)
    zr = jnp.zeros((depth, nbp) + state_rwkv.shape[2:], dt)
    (y_p, y_s), (st_p, st_s) = _trunks(
        [x_prompt, x_sample], [mod_p, mod_s],
        [(zc, zg, zs, zr), (state_gdn_conv, state_gdn, state_rwkv_shift, state_rwkv)], w)
    return (y_p, y_s) + st_p + st_s
```

```python
import functools

import jax
import jax.numpy as jnp
from jax import lax
from jax.experimental import pallas as pl
from jax.experimental.pallas import tpu as pltpu

F32 = jnp.float32
BF16 = jnp.bfloat16

LANES = 128
SUBLANES = 8
CHUNK = 64
GDN_HEADS = 4
GDN_D = 128
RWKV_HEADS = 8
RWKV_N = 64
GDN_CONV = 4
N_EXPERTS = 8
NORM_EPS = 1e-6
L2_EPS = 1e-6
GN_EPS = 64e-5
NEG_BIG = -1e30
RWKV_DECAY_SCALE = 0.6065306597126334
TOKEN_TILE = 512
GDN_TILE = 512
RWKV_TILE = 256
SHORT_SEQ_BATCH = 8
MOE_ROWS = 512
MOE_FF_TILE = 1792
MIB = 1024 * 1024

P_GDN_SCORE = 1
P_GDN_INV = 3
P_GDN_SOLVE = 1
P_GDN_STATE = 1
P_RWKV_SMALL = 1
P_RWKV_INV = 1
P_RWKV_STATE = 1
P_ROUTER = 3


def _cparams(sem, vmem_mib):
    return pltpu.CompilerParams(dimension_semantics=sem, vmem_limit_bytes=vmem_mib * MIB)


def _split_bf16(x):
    hi = x.astype(BF16)
    return hi, (x - hi.astype(F32)).astype(BF16)


def _split3_bf16(x):
    hi = x.astype(BF16)
    r1 = x - hi.astype(F32)
    mid = r1.astype(BF16)
    return hi, mid, (r1 - mid.astype(F32)).astype(BF16)


_CONTRACT = {"nn": (1, 0), "nt": (1, 1), "tn": (0, 0)}


def _dg(a, b, kind, prec):
    off = a.ndim - 2
    ca, cb = _CONTRACT[kind]
    dn = (((ca + off,), (cb + off,)), (((0,), (0,)) if off else ((), ())))
    dot = lambda x, y: lax.dot_general(x, y, dn, preferred_element_type=F32)
    if prec is None:
        return dot(a, b)
    if prec == 1:
        return dot(a.astype(BF16), b.astype(BF16))
    if prec == "l3":
        bb = b.astype(BF16)
        a0, a1, a2 = _split3_bf16(a)
        return dot(a0, bb) + dot(a1, bb) + dot(a2, bb)
    if prec == "r3":
        ab = a.astype(BF16)
        b0, b1, b2 = _split3_bf16(b)
        return dot(ab, b0) + dot(ab, b1) + dot(ab, b2)
    ah, al = _split_bf16(a)
    bh, bl = _split_bf16(b)
    return dot(ah, bh) + dot(ah, bl) + dot(al, bh)


def _mm(a, b, prec=None):
    return _dg(a, b, "nn", prec)


def _mm_nt(a, b, prec=None):
    return _dg(a, b, "nt", prec)


def _mm_tn(a, b, prec=None):
    return _dg(a, b, "tn", prec)


def _silu(x):
    return x * jax.nn.sigmoid(x)


def _softplus(x):
    return jnp.maximum(x, 0.0) + jnp.log1p(jnp.exp(-jnp.abs(x)))


def _iota2(shape, dim):
    return lax.broadcasted_iota(jnp.int32, shape, dim)


def _per_segment(x, fn, *mod_refs):
    mods = [r[0] for r in mod_refs]
    nseg = mods[0].shape[0]
    if nseg == 1:
        return fn(x, *mods)
    rows = x.shape[0] // nseg
    return jnp.concatenate([fn(x[s * rows:(s + 1) * rows], *(m[s:s + 1] for m in mods)) for s in range(nseg)], axis=0)


def _scale_shift(xn, sc_ref, sh_ref):
    return _per_segment(xn, lambda x, sc, sh: x * (1.0 + sc) + sh, sc_ref, sh_ref)


def _gated(f, gt_ref):
    return _per_segment(f, lambda x, gt: gt * x, gt_ref)


def _shift_rows(x, prev8, k):
    head = jnp.where(_iota2(prev8.shape, 0) < k, pltpu.roll(prev8, k, 0), pltpu.roll(x[0:SUBLANES], k, 0))
    return jnp.concatenate([head, pltpu.roll(x, k, 0)[SUBLANES:]], axis=0)


def _causal_taps(x, w, prev8, ntap):
    y = x * w[ntap - 1:ntap, :]
    for k in range(1, ntap):
        y = y + _shift_rows(x, prev8, k) * w[ntap - 1 - k:ntap - k, :]
    return y


def _neumann_inverse(x, eye, prec):
    c = x.shape[-1]
    z = jnp.concatenate([x, jnp.broadcast_to(eye, x.shape)], axis=-1)
    keep_s = _iota2((c, 2 * c), 1) >= c
    for _ in range(6):
        z = _mm(z[..., :c], z, prec) + jnp.where(keep_s, z, 0.0)
    return z[..., c:]


def _ada_kernel(c_ref, w_ref, b_ref, o_ref):
    o_ref[0] = _mm(_silu(c_ref[...]), w_ref[0], 3) + b_ref[0]


def _ada_mod(c_all, w_ada, b_ada):
    depth, d, n6 = w_ada.shape
    bp = c_all.shape[0]
    tn = 1536
    return pl.pallas_call(
        _ada_kernel,
        grid=(depth, n6 // tn),
        in_specs=[pl.BlockSpec((bp, d), lambda l, j: (0, 0)),
                  pl.BlockSpec((1, d, tn), lambda l, j: (l, 0, j)),
                  pl.BlockSpec((1, 1, tn), lambda l, j: (l, 0, j))],
        out_specs=pl.BlockSpec((1, bp, tn), lambda l, j: (l, 0, j)),
        out_shape=jax.ShapeDtypeStruct((depth, bp, n6), F32),
        compiler_params=_cparams(("arbitrary", "arbitrary"), 40),
        name="ada_mod",
    )(c_all, w_ada, b_ada.reshape(depth, 1, n6))


def _norm_proj_kernel(x_ref, sh_ref, sc_ref, g_ref, wa_ref, wab_ref, wb_ref, pa_ref, pab_ref, pb_ref):
    x = x_ref[0]
    rs = lax.rsqrt(jnp.mean(x * x, axis=-1, keepdims=True) + NORM_EPS)
    hb = _scale_shift(x * rs * g_ref[...], sc_ref, sh_ref).astype(BF16)
    pa_ref[0] = _mm(hb, wa_ref[...])
    pab_ref[0] = _mm(hb, wab_ref[...])
    pb_ref[0] = _mm(hb, wb_ref[...])


def _norm_proj(x, mod, g, w_in16, layer, wab, wb, tm):
    b, t, d = x.shape
    na, nab, nb = 4 * GDN_HEADS * GDN_D, wab.shape[1], wb.shape[1]
    nseg = mod.shape[1]
    row = lambda i, j: (i, j, 0)
    const = lambda i, j: (0, 0)
    return pl.pallas_call(
        _norm_proj_kernel,
        grid=(b, t // tm),
        in_specs=[pl.BlockSpec((1, tm, d), row),
                  pl.BlockSpec((1, nseg, d), lambda i, j: (i, 0, 0)),
                  pl.BlockSpec((1, nseg, d), lambda i, j: (i, 0, 1)),
                  pl.BlockSpec((1, d), const),
                  pl.BlockSpec((None, d, na), lambda i, j: (layer, 0, 0)),
                  pl.BlockSpec((d, nab), const),
                  pl.BlockSpec((d, nb), const)],
        out_specs=[pl.BlockSpec((1, tm, na), row), pl.BlockSpec((1, tm, nab), row), pl.BlockSpec((1, tm, nb), row)],
        out_shape=[jax.ShapeDtypeStruct((b, t, na), F32), jax.ShapeDtypeStruct((b, t, nab), F32),
                   jax.ShapeDtypeStruct((b, t, nb), F32)],
        compiler_params=_cparams(("arbitrary", "arbitrary"), 48),
        name="norm_proj",
    )(x, mod, mod, g, w_in16, wab, wb)


def _gdn_kernel(pa_ref, pab_ref, cbuf_ref, s0_ref, cw_ref, alog_ref, dtb_ref, og_ref,
                o_ref, sfin_ref, xc_ref, s_ref, *, tc):
    t = pl.program_id(1)
    nqk = GDN_HEADS * GDN_D
    nconv = 3 * nqk

    nsq = pa_ref.shape[0]
    nh = GDN_HEADS

    @pl.when(t == 0)
    def _():
        xc_ref[...] = cbuf_ref[...]
        s_ref[...] = s0_ref[...].reshape(s_ref.shape)

    ys = []
    for sq in range(nsq):
        x = pa_ref[sq, :, 0:nconv]
        ys.append(_causal_taps(x, cw_ref[...], xc_ref[sq], GDN_CONV))
        xc_ref[sq] = x[tc - SUBLANES:tc, :]
    qkv = _silu(ys[0] if nsq == 1 else jnp.concatenate(ys, axis=0))

    ab = pab_ref[...].reshape(nsq * tc, LANES)
    gmat = -jnp.exp(alog_ref[...]) * _softplus(ab + dtb_ref[...])
    bmat = jax.nn.sigmoid(ab)

    c = CHUNK
    nch = nsq * tc // c
    nb = nch * nh
    ri = _iota2((c, c), 0)
    ci = _iota2((c, c), 1)
    incl = ri >= ci
    strict = ri > ci
    eye_c = (ri == ci).astype(F32)

    def heads(x, width):
        x = x.reshape(nch, c, nh * width)
        return jnp.stack([x[:, :, h * width:(h + 1) * width] for h in range(nh)], axis=1).reshape(nb, c, width)

    q = heads(qkv[:, 0:nqk], GDN_D)
    k = heads(qkv[:, nqk:2 * nqk], GDN_D)
    v = heads(qkv[:, 2 * nqk:3 * nqk], GDN_D)
    ones_d = jnp.ones((GDN_D, GDN_D), BF16)

    def l2n(x, scale):
        ss = _mm((x * x).reshape(nb * c, GDN_D), ones_d, 1).reshape(nb, c, GDN_D)
        return x * (lax.rsqrt(ss + L2_EPS) * scale)

    q = l2n(q, GDN_D ** -0.5)
    k = l2n(k, 1.0)
    beta = heads(bmat[:, nh:2 * nh], 1)

    g4 = gmat.reshape(nch, c, LANES)
    gcol4 = _mm(jnp.broadcast_to(incl.astype(F32), (nch, c, c)), g4, "r3")
    grow4 = _mm_tn(g4, jnp.broadcast_to((ri <= ci).astype(F32), (nch, c, c)), "l3")
    gc = jnp.stack([gcol4[:, :, h:h + 1] for h in range(nh)], axis=1).reshape(nb, c, 1)
    gr = jnp.stack([grow4[:, h:h + 1, :] for h in range(nh)], axis=1).reshape(nb, 1, c)
    decay = jnp.exp(jnp.where(incl, gc - gr, NEG_BIG))

    kb = k * beta
    m = jnp.where(strict, _mm_nt(kb, k, P_GDN_SCORE) * decay, 0.0)
    t0 = _neumann_inverse(-m, eye_c, 1)
    tinv = t0 + _mm(t0, (eye_c - t0) - _mm(m, t0, P_GDN_INV), 1)
    egc = jnp.exp(gc)
    sol = _mm(tinv, jnp.concatenate([v * beta, kb * egc], axis=2), P_GDN_SOLVE)
    u, w = sol[:, :, :GDN_D], sol[:, :, GDN_D:]
    aqk = _mm_nt(q, k, P_GDN_SCORE) * decay
    gl = gc[:, c - 1:c, :]
    kd = k * jnp.exp(gl - gc)
    qg = q * egc
    egl = jnp.exp(gl)

    s = s_ref[...]
    outs = []
    for ch in range(tc // c):
        bs = slice(ch * nh, (ch + 1) * nh) if nsq == 1 else slice(None)
        v_new = u[bs] - _mm(w[bs], s, P_GDN_STATE)
        outs.append(_mm(qg[bs], s, P_GDN_STATE) + _mm(aqk[bs], v_new, P_GDN_STATE))
        s = s * egl[bs] + _mm_tn(kd[bs], v_new, P_GDN_STATE)
    s_ref[...] = s

    o = jnp.stack(outs, axis=0).reshape(nch, nh, c, GDN_D)
    o = o * lax.rsqrt(jnp.mean(o * o, axis=-1, keepdims=True) + NORM_EPS) * og_ref[...]
    for h in range(nh):
        sl = slice(h * GDN_D, (h + 1) * GDN_D)
        z = pa_ref[:, :, nconv + h * GDN_D: nconv + (h + 1) * GDN_D]
        o_ref[:, :, sl] = (o[:, h].reshape(nsq, tc, GDN_D) * _silu(z)).astype(o_ref.dtype)

    @pl.when(t == pl.num_programs(1) - 1)
    def _():
        sfin_ref[...] = s_ref[...].reshape(sfin_ref.shape)


def _seqs_per_step(b, t, tc):
    return max(n for n in (SHORT_SEQ_BATCH, 2, 1) if b % n == 0) if t == tc == CHUNK else 1


def _gdn(pa, pab, cbuf8, s0, layer, cw, alog, dtb, og, tc):
    b, t, na = pa.shape
    nconv = cw.shape[1]
    nsq = _seqs_per_step(b, t, tc)
    row = lambda i, j: (i, j, 0)
    const2 = lambda i, j: (0, 0)
    return pl.pallas_call(
        functools.partial(_gdn_kernel, tc=tc),
        grid=(b // nsq, t // tc),
        in_specs=[pl.BlockSpec((nsq, tc, na), row),
                  pl.BlockSpec((nsq, tc, LANES), row),
                  pl.BlockSpec((nsq, SUBLANES, nconv), lambda i, j: (i, 0, 0)),
                  pl.BlockSpec((None, nsq, GDN_HEADS, GDN_D, GDN_D), lambda i, j: (layer, i, 0, 0, 0)),
                  pl.BlockSpec((GDN_CONV, nconv), const2),
                  pl.BlockSpec((1, LANES), const2),
                  pl.BlockSpec((1, LANES), const2),
                  pl.BlockSpec((1, GDN_D), const2)],
        out_specs=[pl.BlockSpec((nsq, tc, GDN_HEADS * GDN_D), row),
                   pl.BlockSpec((nsq, GDN_HEADS, GDN_D, GDN_D), lambda i, j: (i, 0, 0, 0))],
        out_shape=[jax.ShapeDtypeStruct((b, t, GDN_HEADS * GDN_D), BF16),
                   jax.ShapeDtypeStruct((b, GDN_HEADS, GDN_D, GDN_D), F32)],
        scratch_shapes=[pltpu.VMEM((nsq, SUBLANES, nconv), F32),
                        pltpu.VMEM((nsq * GDN_HEADS, GDN_D, GDN_D), F32)],
        compiler_params=_cparams(("arbitrary", "arbitrary"), 40),
        name="gdn_mixer",
    )(pa, pab, cbuf8, s0, cw, alog, dtb, og)


def _rwkv_kernel(pb_ref, sbuf_ref, s0_ref, mu_ref, w0_ref, w2_ref, a0_ref, a2_ref, g2_ref, kk_ref, ka_ref,
                 rk_ref, lnw_ref, lnb_ref, bd_ref, o_ref, sfin_ref, xc_ref, s_ref, y_ref, *, tc):
    t = pl.program_id(1)
    nw = RWKV_HEADS * RWKV_N

    nsq = pb_ref.shape[0]
    nh = RWKV_HEADS

    @pl.when(t == 0)
    def _():
        xc_ref[...] = sbuf_ref[...]
        s_ref[...] = s0_ref[...].reshape(s_ref.shape)

    xl, pl_ = [], []
    for sq in range(nsq):
        xq = pb_ref[sq]
        xl.append(xq)
        pl_.append(_shift_rows(xq, xc_ref[sq], 1))
        xc_ref[sq] = xq[tc - SUBLANES:tc, :]
    x = xl[0] if nsq == 1 else jnp.concatenate(xl, axis=0)
    prev = pl_[0] if nsq == 1 else jnp.concatenate(pl_, axis=0)
    xs = x + (prev - x) * mu_ref[...]
    r = xs[:, 0:nw]
    kr = xs[:, nw:2 * nw]
    vr = xs[:, 2 * nw:3 * nw]
    wa = xs[:, 3 * nw:3 * nw + LANES]
    gd = xs[:, 3 * nw + LANES:3 * nw + 2 * LANES]

    bd = bd_ref[...]
    ps = P_RWKV_SMALL
    lw = -RWKV_DECAY_SCALE * jax.nn.sigmoid(w0_ref[...] + _mm(jnp.tanh(wa), w2_ref[...], ps))
    a = jax.nn.sigmoid(a0_ref[...] + _mm(wa, a2_ref[...], ps))
    gate = _mm(jax.nn.sigmoid(gd), g2_ref[...], ps)
    kkr = kr * kk_ref[...]
    kk = kkr * lax.rsqrt(_mm(kkr * kkr, bd, ps) + L2_EPS)
    kb = kr * (1.0 + (a - 1.0) * ka_ref[...])
    aa = -kk
    bb = kk * a
    bonus = _mm(r * kb * rk_ref[...], bd, ps) * vr

    c = CHUNK
    n = RWKV_N
    nch = nsq * tc // c
    nb = nch * nh
    ri = _iota2((c, c), 0)
    ci = _iota2((c, c), 1)
    incl = ri >= ci
    eye = ri == ci
    eye_c = eye.astype(F32)

    def heads(x):
        return jnp.stack([x[:, :, h * n:(h + 1) * n] for h in range(nh)], axis=1).reshape(nb, x.shape[1], n)

    chunks = lambda x: x.reshape(nch, c, nw)
    lw4 = chunks(lw)
    g = _mm(jnp.broadcast_to(incl.astype(F32), (nch, c, c)), lw4, "r3")
    gfin = g[:, c - 1:c, :]
    eng = jnp.exp(-g)
    efin = jnp.exp(gfin)
    tail = efin * eng
    aa4, bb4, kb4 = chunks(aa), chunks(bb), chunks(kb)
    at = heads(aa4 * jnp.exp(g - lw4))
    bt = heads(bb4 * eng)
    kt = heads(kb4 * eng)
    rt = heads(chunks(r) * jnp.exp(g))
    bh = heads(bb4 * tail)
    kh = heads(kb4 * tail)
    egf = heads(efin)
    v4 = chunks(vr)
    upper = _iota2((c, 2 * n), 1) >= n
    vcols = [v4[:, :, (h // 2) * 2 * n:(h // 2 + 1) * 2 * n] for h in range(nh)]
    vhi = jnp.stack([jnp.where(upper, col if h % 2 else pltpu.roll(col, n, 2), 0.0)
                     for h, col in enumerate(vcols)], axis=1).reshape(nb, c, 2 * n)
    zeros_lo = jnp.zeros((nb, c, n), F32)
    pad_hi = lambda x: jnp.concatenate([x, zeros_lo], axis=-1)

    h16 = lambda x: x.astype(BF16)
    colj = _iota2((c, 2 * c), 1) % c
    rowi = _iota2((c, 2 * c), 0)
    vhi16 = h16(vhi)
    ar = jnp.concatenate([h16(at), h16(rt)], axis=1)
    xbk = _mm_nt(ar, jnp.concatenate([h16(bt), h16(kt)], axis=1))
    l2 = jnp.where(rowi > colj, xbk[:, :c], 0.0)
    m2 = jnp.where(rowi >= colj, xbk[:, c:], 0.0)
    tinv = _neumann_inverse(l2[..., :c], eye_c, P_RWKV_INV)
    lv = _mm(h16(l2), jnp.concatenate([jnp.zeros_like(vhi16), vhi16], axis=1))
    wu = _mm(h16(tinv), h16(lv + pad_hi(at)))
    wv = jnp.concatenate([h16(wu), vhi16], axis=1)
    qy = _mm(h16(m2), wv) + pad_hi(rt)
    pi = _mm_tn(wv, jnp.concatenate([h16(bh), h16(kh)], axis=1))
    pt = jnp.where(eye, egf, 0.0) + pi[:, :n]
    inc = pi[:, n:]

    s = s_ref[...]
    zeros_s = jnp.zeros_like(s)
    ys = []
    for ch in range(tc // c):
        bs = slice(ch * nh, (ch + 1) * nh) if nsq == 1 else slice(None)
        ys.append(_mm_nt(qy[bs][..., :n], jnp.concatenate([zeros_s, s], axis=1), P_RWKV_STATE) + qy[bs])
        s = _mm(s, pt[bs], P_RWKV_STATE) + inc[bs]
    s_ref[...] = s
    y4 = jnp.stack(ys, axis=0).reshape(nch, nh, c, 2 * n)
    for h in range(nh):
        y_ref[:, h * n:(h + 1) * n] = y4[:, h, :, n:].reshape(nsq * tc, n)

    y = y_ref[...]
    mean = _mm(y, bd, ps) * (1.0 / n)
    dlt = y - mean
    var = _mm(dlt * dlt, bd, ps) * (1.0 / n)
    yn = dlt * lax.rsqrt(var + GN_EPS) * lnw_ref[...] + lnb_ref[...]
    o_ref[...] = ((yn + bonus) * gate).reshape(o_ref.shape).astype(o_ref.dtype)

    @pl.when(t == pl.num_programs(1) - 1)
    def _():
        sfin_ref[...] = s_ref[...].reshape(sfin_ref.shape)


def _rwkv(pb, sbuf8, s0, layer, vecs, w2p, a2p, g2, bd, tc):
    b, t, nb = pb.shape
    nw = RWKV_HEADS * RWKV_N
    nsq = _seqs_per_step(b, t, tc)
    row = lambda i, j: (i, j, 0)
    const2 = lambda i, j: (0, 0)
    mu, w0, a0, kk, ka, rk, lnw, lnb = vecs
    vec_spec = pl.BlockSpec((1, nw), const2)
    return pl.pallas_call(
        functools.partial(_rwkv_kernel, tc=tc),
        grid=(b // nsq, t // tc),
        in_specs=[pl.BlockSpec((nsq, tc, nb), row),
                  pl.BlockSpec((nsq, SUBLANES, nb), lambda i, j: (i, 0, 0)),
                  pl.BlockSpec((None, nsq, RWKV_HEADS, RWKV_N, RWKV_N), lambda i, j: (layer, i, 0, 0, 0)),
                  pl.BlockSpec((1, nb), const2),
                  vec_spec,
                  pl.BlockSpec((LANES, nw), const2),
                  vec_spec,
                  pl.BlockSpec((LANES, nw), const2),
                  pl.BlockSpec((LANES, nw), const2),
                  vec_spec, vec_spec, vec_spec, vec_spec, vec_spec,
                  pl.BlockSpec((nw, nw), const2)],
        out_specs=[pl.BlockSpec((nsq, tc, nw), row),
                   pl.BlockSpec((nsq, RWKV_HEADS, RWKV_N, RWKV_N), lambda i, j: (i, 0, 0, 0))],
        out_shape=[jax.ShapeDtypeStruct((b, t, nw), BF16),
                   jax.ShapeDtypeStruct((b, RWKV_HEADS, RWKV_N, RWKV_N), F32)],
        scratch_shapes=[pltpu.VMEM((nsq, SUBLANES, nb), F32),
                        pltpu.VMEM((nsq * RWKV_HEADS, RWKV_N, RWKV_N), F32),
                        pltpu.VMEM((nsq * tc, nw), F32)],
        compiler_params=_cparams(("arbitrary", "arbitrary"), 40),
        name="rwkv_mixer",
    )(pb, sbuf8, s0, mu, w0, w2p, a0, a2p, g2, kk, ka, rk, lnw, lnb, bd)


def _out_proj_kernel(oa_ref, ob_ref, x_ref, gt_ref, sh_ref, sc_ref, g_ref, woa_ref, wob_ref, *rest, with_router, nt):
    if with_router:
        wr_ref = rest[0]
        x1_ref, h_ref, lg_ref = rest[-3:]
    else:
        x1_ref, h_ref = rest

    @pl.when(pl.program_id(1) < nt)
    def _():
        mix = _mm(oa_ref[0], woa_ref[...]) + _mm(ob_ref[0], wob_ref[...])
        x1 = x_ref[0] + _gated(mix, gt_ref)
        x1_ref[0] = x1
        rs = lax.rsqrt(jnp.mean(x1 * x1, axis=-1, keepdims=True) + NORM_EPS)
        h = _scale_shift(x1 * rs * g_ref[...], sc_ref, sh_ref)
        if with_router:
            h_ref[...] = h
            lg_ref[...] = _mm(h, wr_ref[...], P_ROUTER)
        else:
            h_ref[0] = h.astype(h_ref.dtype)

    if with_router:
        @pl.when(pl.program_id(1) >= nt)
        def _():
            h_ref[...] = jnp.zeros_like(h_ref)
            lg_ref[...] = jnp.zeros_like(lg_ref)


def _out_proj(oa, ob, x, mod, g, woa, wob, tm, router=None, flat_rows=None, row_off=0, flat_prev=None):
    b, t, d = x.shape
    nh = oa.shape[2]
    nt = t // tm
    nseg = mod.shape[1]
    tail = 0
    if router is not None and flat_prev is None:
        assert b == 1 and row_off == 0 and (flat_rows - t) % tm == 0
        tail = (flat_rows - t) // tm
    row = lambda i, j: (i, jnp.minimum(j, nt - 1), 0)
    const = lambda i, j: (0, 0)
    in_specs = [pl.BlockSpec((1, tm, nh), row), pl.BlockSpec((1, tm, nh), row), pl.BlockSpec((1, tm, d), row),
                pl.BlockSpec((1, nseg, d), lambda i, j: (i, 0, 2)),
                pl.BlockSpec((1, nseg, d), lambda i, j: (i, 0, 3)),
                pl.BlockSpec((1, nseg, d), lambda i, j: (i, 0, 4)),
                pl.BlockSpec((1, d), const), pl.BlockSpec((nh, d), const), pl.BlockSpec((nh, d), const)]
    args = [oa, ob, x, mod, mod, mod, g, woa, wob]
    aliases = {}
    if router is None:
        out_specs = [pl.BlockSpec((1, tm, d), row), pl.BlockSpec((1, tm, d), row)]
        out_shape = [jax.ShapeDtypeStruct((b, t, d), F32), jax.ShapeDtypeStruct((b, t, d), BF16)]
    else:
        blk_off = row_off // tm
        flat = lambda i, j: (blk_off + i * nt + j, 0)
        in_specs.append(pl.BlockSpec((d, LANES), const))
        args.append(router)
        if flat_prev is not None:
            aliases = {len(args): 1, len(args) + 1: 2}
            in_specs += [pl.BlockSpec(memory_space=pl.ANY)] * 2
            args += list(flat_prev)
        out_specs = [pl.BlockSpec((1, tm, d), row), pl.BlockSpec((tm, d), flat), pl.BlockSpec((tm, LANES), flat)]
        out_shape = [jax.ShapeDtypeStruct((b, t, d), F32), jax.ShapeDtypeStruct((flat_rows, d), F32),
                     jax.ShapeDtypeStruct((flat_rows, LANES), F32)]
    return pl.pallas_call(
        functools.partial(_out_proj_kernel, with_router=router is not None, nt=nt),
        grid=(b, nt + tail),
        in_specs=in_specs, out_specs=out_specs, out_shape=out_shape, input_output_aliases=aliases,
        compiler_params=_cparams(("arbitrary", "arbitrary"), 56),
        name="out_proj" if router is None else "out_proj_router",
    )(*args)


def _final_norm(x2, fg_ref):
    rs = lax.rsqrt(jnp.mean(x2 * x2, axis=-1, keepdims=True) + NORM_EPS)
    return x2 * rs * fg_ref[...]


def _ffn_kernel(h_ref, x_ref, gt_ref, wg_ref, wu_ref, wd_ref, fg_ref, o_ref, *, nf, final):
    h = h_ref[0]
    ff = wg_ref.shape[1]
    tf = ff // nf
    acc = None
    for f in range(nf):
        cols = slice(f * tf, (f + 1) * tf)
        act = (_silu(_mm(h, wg_ref[:, cols])) * _mm(h, wu_ref[:, cols])).astype(BF16)
        part = _mm(act, wd_ref[cols, :])
        acc = part if acc is None else acc + part
    x2 = x_ref[0] + _gated(acc, gt_ref)
    o_ref[0] = _final_norm(x2, fg_ref) if final else x2


def _ffn(h, x, mod, wg, wu, wd, fg, final, tm):
    b, t, d = x.shape
    ff = wg.shape[1]
    row = lambda i, j: (i, j, 0)
    const = lambda i, j: (0, 0)
    once = pl.Buffered(1)
    return pl.pallas_call(
        functools.partial(_ffn_kernel, nf=1, final=final),
        grid=(b, t // tm),
        in_specs=[pl.BlockSpec((1, tm, d), row), pl.BlockSpec((1, tm, d), row),
                  pl.BlockSpec((1, mod.shape[1], d), lambda i, j: (i, 0, 5)),
                  pl.BlockSpec((d, ff), const, pipeline_mode=once),
                  pl.BlockSpec((d, ff), const, pipeline_mode=once),
                  pl.BlockSpec((ff, d), const, pipeline_mode=once),
                  pl.BlockSpec((1, d), const)],
        out_specs=pl.BlockSpec((1, tm, d), row),
        out_shape=jax.ShapeDtypeStruct((b, t, d), F32),
        compiler_params=_cparams(("arbitrary", "arbitrary"), 56),
        name="ffn_dense",
    )(h, x, mod, wg, wu, wd, fg)


def _route_kernel(lg_ref, info_ref, p_ref, cnt_ref, carry_ref, *, tr):
    i = pl.program_id(0)

    @pl.when(i == 0)
    def _():
        carry_ref[...] = jnp.zeros_like(carry_ref)

    lane = _iota2((tr, LANES), 1)
    lg = jnp.where(lane < N_EXPERTS, lg_ref[...], NEG_BIG)
    m1 = jnp.max(lg, axis=-1, keepdims=True)
    e1 = jnp.min(jnp.where(lg == m1, lane, LANES), axis=-1, keepdims=True)
    lg2 = jnp.where(lane == e1, NEG_BIG, lg)
    m2 = jnp.max(lg2, axis=-1, keepdims=True)
    e2 = jnp.min(jnp.where(lg2 == m2, lane, LANES), axis=-1, keepdims=True)
    ex = jnp.exp(m2 - m1)
    den = 1.0 + ex
    p1 = 1.0 / den
    p2 = ex / den
    oh1 = lane == e1
    oh2 = lane == e2
    oh = jnp.where(oh1 | oh2, 1.0, 0.0)
    lstrict = (_iota2((tr, tr), 0) > _iota2((tr, tr), 1)).astype(BF16)
    cex = _mm(lstrict, oh.astype(BF16)) + carry_ref[...]
    rank1 = jnp.sum(jnp.where(oh1, cex, 0.0), axis=-1, keepdims=True).astype(jnp.int32)
    rank2 = jnp.sum(jnp.where(oh2, cex, 0.0), axis=-1, keepdims=True).astype(jnp.int32)
    carry_ref[...] = carry_ref[...] + jnp.sum(oh, axis=0, keepdims=True)
    info_ref[...] = jnp.where(lane == 0, e1, jnp.where(lane == 1, e2, jnp.where(lane == 2, rank1,
                              jnp.where(lane == 3, rank2, 0))))
    p_ref[...] = jnp.where(lane == 0, p1, jnp.where(lane == 1, p2, 0.0))

    @pl.when(i == pl.num_programs(0) - 1)
    def _():
        cnt_ref[...] = carry_ref[...].astype(jnp.int32)


def _route(logits, tr):
    n = logits.shape[0]
    blk = pl.BlockSpec((tr, LANES), lambda i: (i, 0))
    return pl.pallas_call(
        functools.partial(_route_kernel, tr=tr),
        grid=(n // tr,),
        in_specs=[blk],
        out_specs=[blk, blk, pl.BlockSpec((1, LANES), lambda i: (0, 0))],
        out_shape=[jax.ShapeDtypeStruct((n, LANES), jnp.int32), jax.ShapeDtypeStruct((n, LANES), F32),
                   jax.ShapeDtypeStruct((1, LANES), jnp.int32)],
        scratch_shapes=[pltpu.VMEM((1, LANES), F32)],
        compiler_params=_cparams(("arbitrary",), 24),
        name="moe_route",
    )(logits)


def _dest_kernel(info_ref, sp_ref, d_ref):
    info = info_ref[...]
    lane = _iota2(info.shape, 1)
    sp = sp_ref[...].astype(F32)
    start1 = jnp.sum(jnp.where(lane == info[:, 0:1], sp, 0.0), axis=-1, keepdims=True).astype(jnp.int32)
    start2 = jnp.sum(jnp.where(lane == info[:, 1:2], sp, 0.0), axis=-1, keepdims=True).astype(jnp.int32)
    d_ref[...] = jnp.where(lane == 0, start1 + info[:, 2:3], jnp.where(lane == 1, start2 + info[:, 3:4], 0))


def _dest(info, sp_row, tr):
    n = info.shape[0]
    blk = pl.BlockSpec((tr, LANES), lambda i: (i, 0))
    return pl.pallas_call(
        _dest_kernel,
        grid=(n // tr,),
        in_specs=[blk, pl.BlockSpec((1, LANES), lambda i: (0, 0))],
        out_specs=blk,
        out_shape=jax.ShapeDtypeStruct((n, LANES), jnp.int32),
        compiler_params=_cparams(("arbitrary",), 24),
        name="moe_dest",
    )(info, sp_row)


ROW_DMA_UNROLL = 16


def _dispatch_kernel(d1_ref, d2_ref, sp_ref, ep_ref, h_ref, xb_ref, zero_ref, sem, zsem, *, td):
    i = pl.program_id(0)

    @pl.when(i == 0)
    def _():
        zero_ref[...] = jnp.zeros_like(zero_ref)

        def zero_block(start):
            cp = pltpu.make_async_copy(zero_ref, xb_ref.at[pl.ds(pl.multiple_of(start, MOE_ROWS), MOE_ROWS), :], zsem)
            cp.start()
            cp.wait()

        for e in range(N_EXPERTS):
            @pl.when(ep_ref[e] > sp_ref[e])
            def _():
                zero_block(ep_ref[e] - MOE_ROWS)

        def tail(blk, carry):
            zero_block(blk * MOE_ROWS)
            return carry

        lax.fori_loop(ep_ref[N_EXPERTS - 1] // MOE_ROWS, xb_ref.shape[0] // MOE_ROWS, tail, 0)

    base = i * td

    def row_copy(j, dst):
        return pltpu.make_async_copy(h_ref.at[pl.ds(j, 1), :], xb_ref.at[pl.ds(dst, 1), :], sem)

    def issue(j8, carry):
        for u in range(ROW_DMA_UNROLL):
            j = j8 * ROW_DMA_UNROLL + u
            row_copy(j, d1_ref[base + j]).start(priority=0)
            row_copy(j, d2_ref[base + j]).start(priority=1)
        return carry

    lax.fori_loop(0, td // ROW_DMA_UNROLL, issue, 0)
    for _ in range(2):
        pltpu.make_async_copy(h_ref, xb_ref.at[pl.ds(0, td), :], sem).wait()


def _dispatch(d1, d2, sp, ep, h2, n_rows, td):
    n, d = h2.shape
    return pl.pallas_call(
        functools.partial(_dispatch_kernel, td=td),
        grid_spec=pltpu.PrefetchScalarGridSpec(
            num_scalar_prefetch=4,
            grid=(n // td,),
            in_specs=[pl.BlockSpec((td, d), lambda i, *_: (i, 0))],
            out_specs=pl.BlockSpec(memory_space=pl.ANY),
            scratch_shapes=[pltpu.VMEM((MOE_ROWS, d), F32), pltpu.SemaphoreType.DMA, pltpu.SemaphoreType.DMA]),
        out_shape=jax.ShapeDtypeStruct((n_rows, d), F32),
        compiler_params=_cparams(("arbitrary",), 24),
        name="moe_dispatch",
    )(d1, d2, sp, ep, h2)


def _expert_kernel(be_ref, nu_ref, xb_ref, wg_ref, wu_ref, wd_ref, yb_ref):
    j = pl.program_id(0)
    f = pl.program_id(1)

    @pl.when(j < nu_ref[0])
    def _():
        xg = xb_ref[...].astype(BF16)
        act = (_silu(_mm(xg, wg_ref[0])) * _mm(xg, wu_ref[0])).astype(BF16)
        part = _mm(act, wd_ref[0])

        @pl.when(f == 0)
        def _():
            yb_ref[...] = part

        @pl.when(f > 0)
        def _():
            yb_ref[...] = yb_ref[...] + part

    @pl.when((j >= nu_ref[0]) & (f == 0))
    def _():
        yb_ref[...] = jnp.zeros_like(yb_ref)


def _experts(block_e, n_used, xb, wg, wu, wd, tf):
    n_rows, d = xb.shape
    ff = wg.shape[2]
    nf = ff // tf
    n_blocks = n_rows // MOE_ROWS

    def blk(j, f, be, nu):
        return (jnp.minimum(j, nu[0] - 1), 0)

    def fcol(j, f, be, nu):
        return jnp.where(j < nu[0], f, nf - 1)

    return pl.pallas_call(
        _expert_kernel,
        grid_spec=pltpu.PrefetchScalarGridSpec(
            num_scalar_prefetch=2,
            grid=(n_blocks, nf),
            in_specs=[pl.BlockSpec((MOE_ROWS, d), blk),
                      pl.BlockSpec((1, d, tf), lambda j, f, be, nu: (be[j], 0, fcol(j, f, be, nu))),
                      pl.BlockSpec((1, d, tf), lambda j, f, be, nu: (be[j], 0, fcol(j, f, be, nu))),
                      pl.BlockSpec((1, tf, d), lambda j, f, be, nu: (be[j], fcol(j, f, be, nu), 0))],
            out_specs=pl.BlockSpec((MOE_ROWS, d), lambda j, f, be, nu: (j, 0))),
        out_shape=jax.ShapeDtypeStruct((n_rows, d), F32),
        compiler_params=_cparams(("arbitrary", "arbitrary"), 56),
        name="moe_experts",
    )(block_e, n_used, xb, wg, wu, wd)


def _combine_kernel(d1_ref, d2_ref, x_ref, gt_ref, p_ref, fg_ref, yb_ref, o_ref, y1_ref, y2_ref, sem, *,
                    tm, final, row_off):
    base = row_off + (pl.program_id(0) * pl.num_programs(1) + pl.program_id(1)) * tm

    def row_copy(src, buf_ref, j):
        return pltpu.make_async_copy(yb_ref.at[pl.ds(src, 1), :], buf_ref.at[pl.ds(j, 1), :], sem)

    def issue(j8, carry):
        for u in range(min(ROW_DMA_UNROLL, tm)):
            j = j8 * ROW_DMA_UNROLL + u
            row_copy(d1_ref[base + j], y1_ref, j).start(priority=0)
            row_copy(d2_ref[base + j], y2_ref, j).start(priority=1)
        return carry

    lax.fori_loop(0, tm // ROW_DMA_UNROLL, issue, 0)
    for buf_ref in (y1_ref, y2_ref):
        pltpu.make_async_copy(yb_ref.at[pl.ds(0, tm), :], buf_ref, sem).wait()

    p = p_ref[...]
    f = y1_ref[...] * p[:, 0:1] + y2_ref[...] * p[:, 1:2]
    x2 = x_ref[0] + _gated(f, gt_ref)
    o_ref[0] = _final_norm(x2, fg_ref) if final else x2


def _combine(d1, d2, x, mod, probs, fg, yb, final, tm, row_off):
    b, t, d = x.shape
    nt = t // tm
    blk_off = row_off // tm
    return pl.pallas_call(
        functools.partial(_combine_kernel, tm=tm, final=final, row_off=row_off),
        grid_spec=pltpu.PrefetchScalarGridSpec(
            num_scalar_prefetch=2,
            grid=(b, nt),
            in_specs=[pl.BlockSpec((1, tm, d), lambda i, j, *_: (i, j, 0)),
                      pl.BlockSpec((1, mod.shape[1], d), lambda i, j, *_: (i, 0, 5)),
                      pl.BlockSpec((tm, LANES), lambda i, j, *_: (blk_off + i * nt + j, 0)),
                      pl.BlockSpec((1, d), lambda i, j, *_: (0, 0)),
                      pl.BlockSpec(memory_space=pl.ANY)],
            out_specs=pl.BlockSpec((1, tm, d), lambda i, j, *_: (i, j, 0)),
            scratch_shapes=[pltpu.VMEM((tm, d), F32), pltpu.VMEM((tm, d), F32), pltpu.SemaphoreType.DMA]),
        out_shape=jax.ShapeDtypeStruct((b, t, d), F32),
        compiler_params=_cparams(("arbitrary", "arbitrary"), 40),
        name="moe_combine",
    )(d1, d2, x, mod, probs, fg, yb)


def _moe(h2, logits, trunks, wg, wu, wd, fg, final):
    n, d = h2.shape
    tile = lambda cap: max(r for r in (cap, cap // 2, cap // 4, cap // 8, cap // 16) if n % r == 0)
    info, probs, cnt = _route(logits, tile(512))
    counts = cnt[0, :N_EXPERTS]
    padded = (counts + MOE_ROWS - 1) // MOE_ROWS * MOE_ROWS
    ep = jnp.cumsum(padded).astype(jnp.int32)
    sp = ep - padded
    n_blocks = -(-(2 * n) // MOE_ROWS) + N_EXPERTS
    n_used = jnp.maximum(ep[-1] // MOE_ROWS, 1).astype(jnp.int32)
    blk_start = jnp.minimum(jnp.arange(n_blocks, dtype=jnp.int32), n_used - 1) * MOE_ROWS
    block_e = jnp.minimum(jnp.sum(blk_start[:, None] >= ep[None, :], axis=1), N_EXPERTS - 1).astype(jnp.int32)
    dest = _dest(info, jnp.pad(sp, (0, LANES - N_EXPERTS))[None, :], tile(1024))
    d1, d2 = dest[:, 0], dest[:, 1]
    xb = _dispatch(d1, d2, sp, ep, h2, n_blocks * MOE_ROWS, tile(1024))
    yb = _experts(block_e, n_used.reshape(1), xb, wg, wu, wd, MOE_FF_TILE)
    return [_combine(d1, d2, x1, mod, probs, fg, yb, final, tm, off) for x1, mod, tm, off in trunks]


def _pad_rows_front(a, rows):
    pad = [(0, 0)] * a.ndim
    pad[-2] = (rows - a.shape[-2], 0)
    return jnp.pad(a, pad)


def _trunks(xs, mods, states, w):
    depth = w["w_in16"].shape[0]
    n_tr = len(xs)
    shapes = [x.shape for x in xs]
    d = shapes[0][2]
    offs = [sum(s[0] * s[1] for s in shapes[:i]) for i in range(n_tr)]
    n_all = sum(s[0] * s[1] for s in shapes)
    tms, mvs = [], []
    for (b, t, _), m in zip(shapes, mods):
        if t < TOKEN_TILE and TOKEN_TILE % t == 0 and (b * t) % TOKEN_TILE == 0:
            nseg = TOKEN_TILE // t
            tms.append(TOKEN_TILE)
            mvs.append(m.reshape(depth, b // nseg, nseg, m.shape[-1]))
        else:
            tms.append(min(t, TOKEN_TILE))
            mvs.append(m)
    wide = [2 * tm if m.shape[2] == 1 and s[1] % (2 * tm) == 0 and (n_all - s[0] * s[1]) % (2 * tm) == 0 else tm
            for tm, m, s in zip(tms, mvs, shapes)]
    tok = lambda a, i: a.reshape(-1, tms[i] if shapes[i][1] < tms[i] else shapes[i][1], a.shape[-1])
    seq = lambda a, i: a.reshape(shapes[i][0], shapes[i][1], a.shape[-1])
    xs = [tok(x, i) for i, x in enumerate(xs)]
    new_states = [([], [], [], []) for _ in xs]
    for l in range(depth):
        final = l == depth - 1
        j = l // 2
        x1s, flat = [], None
        for i, x in enumerate(xs):
            t = shapes[i][1]
            ml = mvs[i][l]
            conv0, gdn0, shift0, rwkv0 = states[i]
            pa, pab, pb = (seq(a, i) for a in _norm_proj(x, ml, w["norm1_g"][l], w["w_in16"], l, w["wab"][l],
                                                         w["wb"][l], tms[i]))
            oa, sg = _gdn(pa, pab, _pad_rows_front(conv0[l], SUBLANES), gdn0, l, w["conv_w"][l], w["alog"][l],
                          w["dtb"][l], w["onorm_g"][l], min(t, GDN_TILE))
            ob, sr = _rwkv(pb, _pad_rows_front(shift0[l], SUBLANES), rwkv0, l, [v[l] for v in w["rwkv_vecs"]],
                           w["w2p"][l], w["a2p"][l], w["g2"][l], w["bd"], min(t, RWKV_TILE))
            for lst, val in zip(new_states[i], (pa[:, t - (GDN_CONV - 1):, :w["conv_w"].shape[2]], sg,
                                                pb[:, t - 1:, :], sr)):
                lst.append(val)
            oa, ob = tok(oa, i), tok(ob, i)
            if l % 2 == 0:
                x1, h2 = _out_proj(oa, ob, x, ml, w["norm2_g"][l], w["woa"][l], w["wob"][l], wide[i])
                xs[i] = _ffn(h2, x1, ml, w["ffn_g"][j], w["ffn_u"][j], w["ffn_d"][j], w["final_g"], final, tms[i])
            else:
                x1, *flat = _out_proj(oa, ob, x, ml, w["norm2_g"][l], w["woa"][l], w["wob"][l], wide[i],
                                      router=w["router"][j], flat_rows=n_all, row_off=offs[i], flat_prev=flat)
                x1s.append(x1)
        if l % 2 == 1:
            trunks = [(x1s[i], mvs[i][l], wide[i], offs[i]) for i in range(n_tr)]
            xs = _moe(flat[0], flat[1], trunks, w["moe_g"][j], w["moe_u"][j], w["moe_d"][j], w["final_g"], final)
    return [seq(x, i) for i, x in enumerate(xs)], [tuple(jnp.stack(s) for s in st) for st in new_states]


def kernel(x_prompt, x_sample, c_prompt, c_sample, state_gdn_conv, state_gdn, state_rwkv_shift, state_rwkv, w_ada, b_ada, norm1_g, norm2_g, w_in, gdn_conv_w, gdn_a_log, gdn_dt_bias, gdn_onorm_g, rwkv_mu, rwkv_w0, rwkv_w2, rwkv_a0, rwkv_a2, rwkv_g2, rwkv_k_k, rwkv_k_a, rwkv_r_k, rwkv_ln_w, rwkv_ln_b, w_out, ffn_w_gate, ffn_w_up, ffn_w_down, moe_router, moe_w_gate, moe_w_up, moe_w_down, final_g):
    depth, d, _ = w_in.shape
    nbp = x_prompt.shape[0]
    nbs = x_sample.shape[0]
    nqkvz = 4 * GDN_HEADS * GDN_D
    nab = 2 * GDN_HEADS
    nw = RWKV_HEADS * RWKV_N
    lora_w = rwkv_w2.shape[1]

    def lane_pad(v):
        return jnp.pad(v, ((0, 0), (0, LANES - v.shape[1])))[:, None, :]

    rows = lambda v: v[:, None, :]
    hi = jnp.arange(nw) // RWKV_N
    w = dict(
        norm1_g=rows(norm1_g), norm2_g=rows(norm2_g), final_g=final_g[None, :],
        w_in16=w_in.astype(BF16),
        wab=jnp.pad(w_in[:, :, nqkvz:nqkvz + nab], ((0, 0), (0, 0), (0, LANES - nab))).astype(BF16),
        wb=w_in[:, :, nqkvz + nab:].astype(BF16),
        conv_w=gdn_conv_w, alog=lane_pad(gdn_a_log), dtb=lane_pad(gdn_dt_bias), onorm_g=rows(gdn_onorm_g),
        rwkv_vecs=[rows(rwkv_mu), rows(rwkv_w0), rows(rwkv_a0), rows(rwkv_k_k), rows(rwkv_k_a),
                   rwkv_r_k.reshape(depth, 1, nw), rows(rwkv_ln_w), rows(rwkv_ln_b)],
        w2p=jnp.pad(rwkv_w2, ((0, 0), (0, LANES - lora_w), (0, 0))),
        a2p=jnp.pad(rwkv_a2, ((0, 0), (lora_w, LANES - lora_w - rwkv_a2.shape[1]), (0, 0))),
        g2=rwkv_g2,
        bd=(hi[:, None] == hi[None, :]).astype(F32),
        woa=w_out[:, :GDN_HEADS * GDN_D, :].astype(BF16), wob=w_out[:, GDN_HEADS * GDN_D:, :].astype(BF16),
        ffn_g=ffn_w_gate.astype(BF16), ffn_u=ffn_w_up.astype(BF16), ffn_d=ffn_w_down.astype(BF16),
        router=jnp.pad(moe_router, ((0, 0), (0, 0), (0, LANES - moe_router.shape[2]))),
        moe_g=moe_w_gate.astype(BF16), moe_u=moe_w_up.astype(BF16), moe_d=moe_w_down.astype(BF16),
    )

    nb_all = nbp + nbs
    bp = -(-nb_all // SUBLANES) * SUBLANES
    c_all = jnp.pad(jnp.concatenate([c_prompt, c_sample], axis=0), ((0, bp - nb_all), (0, 0)))
    mod = _ada_mod(c_all, w_ada, b_ada)[:, :, None, :]
    mod_p, mod_s = mod[:, :nbp], mod[:, nbp:nb_all]

    dt = x_prompt.dtype
    zc = jnp.zeros((depth, nbp) + state_gdn_conv.shape[2:], dt)
    zg = jnp.zeros((depth, nbp) + state_gdn.shape[2:], dt)
    zs = jnp.zeros((depth, nbp) + state_rwkv_shift.shape[2:], dt)
    zr = jnp.zeros((depth, nbp) + state_rwkv.shape[2:], dt)
    (y_p, y_s), (st_p, st_s) = _trunks(
        [x_prompt, x_sample], [mod_p, mod_s],
        [(zc, zg, zs, zr), (state_gdn_conv, state_gdn, state_rwkv_shift, state_rwkv)], w)
    return (y_p, y_s) + st_p + st_s
```
